```python
import math
import jax
import jax.numpy as jnp
from jax import lax
import numpy as np

D_MODEL = 1024
BATCH = 4
SEQ = 4096
DEPTH = 2
DEC_BATCH = 128
DEC_SEQ = 8
PAST_LEN = 2048
PAGE_SIZE = 128

MIX_WIDTH = D_MODEL
NORM_EPS = 1e-6
A_HEADS = 4
A_HD = 64
A_W = A_HEADS * A_HD
A_DECAY_LORA = 64
A_ICL_LORA = 64
A_GATE_LORA = 128
A_COLS = 3 * A_W + A_DECAY_LORA + A_ICL_LORA + A_GATE_LORA
A_LN_EPS = 64e-5
B_HEADS = 4
B_HD = 64
B_W = B_HEADS * B_HD
B_CONV = 4
B_CHUNK = 64
B_COLS = 4 * B_W + 2 * B_HEADS
C_HEADS = 4
C_HD = 64
C_VD = 2 * C_HD
C_W = C_HEADS * C_VD
C_COLS = 3 * C_W
Q_BLOCK = 128
REL_BUCKETS = 32
REL_MAX_DIST = 128
P_COLS = A_COLS + B_COLS + C_COLS
PK_HEADS = 8
PK_NKEYS = 128
PK_EXPERTS = PK_NKEYS * PK_NKEYS
PK_DIM = 256
PK_HALF = PK_DIM // 2
PK_TOPK = 16
PK_TOKEN_BLOCK = 256

kernel_name = 'hymba_rwkv7_gdn_diffattn_peer_step'


def rms_norm(x, g, eps=NORM_EPS):
    xf = x.astype(jnp.float32)
    y = xf * lax.rsqrt(jnp.mean(xf * xf, axis=-1, keepdims=True) + eps)
    return (y * g.astype(jnp.float32)).astype(x.dtype)


def l2norm(x, eps=1e-6):
    xf = x.astype(jnp.float32)
    return xf * lax.rsqrt(jnp.sum(xf * xf, axis=-1, keepdims=True) + eps)


def t5_bucket(rel):
    n = jnp.maximum(-rel, 0)
    max_exact = REL_BUCKETS // 2
    nf = jnp.maximum(n, 1).astype(jnp.float32)
    large = max_exact + (jnp.log(nf / max_exact) / math.log(REL_MAX_DIST / max_exact)
                         * (REL_BUCKETS - max_exact)).astype(jnp.int32)
    large = jnp.minimum(large, REL_BUCKETS - 1)
    return jnp.where(n < max_exact, n, large)


def rwkv7_scan(r, decay, k, v, a_vec, b_vec, S0):
    def step(S, inp):
        r_t, w_t, k_t, v_t, a_t, b_t = inp
        sa = jnp.einsum('bhij,bhj->bhi', S, a_t)
        S = S * w_t[:, :, None, :] + sa[..., None] * b_t[:, :, None, :] + v_t[..., None] * k_t[:, :, None, :]
        return S, jnp.einsum('bhij,bhj->bhi', S, r_t)
    xs = tuple(jnp.moveaxis(t.astype(jnp.float32), 1, 0) for t in (r, decay, k, v, a_vec, b_vec))
    S, ys = lax.scan(step, S0.astype(jnp.float32), xs)
    return jnp.moveaxis(ys, 0, 1), S


def rwkv7_mixer(hA, shift_prev, S0, P):
    Bn, L, _ = hA.shape
    prev = jnp.concatenate([shift_prev[:, None].astype(hA.dtype), hA[:, :-1]], axis=1)
    hs = (hA + (prev - hA) * P['rw_mu']).astype(jnp.float32)
    r = hs[..., :A_W]
    k = hs[..., A_W:2 * A_W]
    v = hs[..., 2 * A_W:3 * A_W]
    o = 3 * A_W
    xw = hs[..., o:o + A_DECAY_LORA]
    o += A_DECAY_LORA
    xa = hs[..., o:o + A_ICL_LORA]
    o += A_ICL_LORA
    xg = hs[..., o:]
    w = -jax.nn.softplus(-(P['rw_w0'] + jnp.tanh(xw) @ P['rw_w2'])) - 0.5
    a = jax.nn.sigmoid(P['rw_a0'] + xa @ P['rw_a2'])
    g = jax.nn.sigmoid(xg) @ P['rw_g2']
    heads = lambda t: t.reshape(Bn, L, A_HEADS, A_HD)
    kk = l2norm(heads(k * P['rw_kk']))
    k = k * (1.0 + (a - 1.0) * P['rw_ka'])
    r, k, v, a = heads(r), heads(k), heads(v), heads(a)
    decay = jnp.exp(-jnp.exp(heads(w)))
    y, S = rwkv7_scan(r, decay, k, v, -kk, kk * a, S0)
    mu = jnp.mean(y, axis=-1, keepdims=True)
    var = jnp.mean(jnp.square(y - mu), axis=-1, keepdims=True)
    y = ((y - mu) * lax.rsqrt(var + A_LN_EPS)).reshape(Bn, L, A_W) * P['rw_ln_g'] + P['rw_ln_b']
    y = y + (jnp.sum(r * k * P['rw_rk'], axis=-1, keepdims=True) * v).reshape(Bn, L, A_W)
    return (y * g).astype(hA.dtype), S.astype(S0.dtype), hA[:, -1]


def gated_delta_chunked(q, k, v, g, beta, S0):
    f32 = jnp.float32
    q, k, v, g, beta, S = (t.astype(f32) for t in (q, k, v, g, beta, S0))
    Bn, L, H, dk = q.shape
    dv = v.shape[-1]
    C = min(B_CHUNK, L)
    n = -(-L // C)
    pad = n * C - L

    def chunk(t):
        t = jnp.pad(t, ((0, 0), (0, pad)) + ((0, 0),) * (t.ndim - 2))
        t = t.reshape((Bn, n, C) + t.shape[2:])
        return jnp.moveaxis(t, 3, 1)

    q = chunk(q) * dk ** -0.5
    k, v, g, beta = chunk(k), chunk(v), chunk(g), chunk(beta)
    gc = jnp.cumsum(g, axis=-1)
    idx = jnp.arange(C)
    incl = idx[:, None] >= idx[None, :]
    strict = idx[:, None] > idx[None, :]
    dec = jnp.exp(jnp.where(incl, gc[..., :, None] - gc[..., None, :], -jnp.inf))
    kb = k * beta[..., None]
    A = jnp.where(strict, jnp.einsum('bhnid,bhnjd->bhnij', kb, k) * dec, 0.0)
    eye = jnp.eye(C, dtype=f32)
    T = lax.linalg.triangular_solve(eye + A, jnp.broadcast_to(eye, A.shape),
                                    left_side=True, lower=True, unit_diagonal=True)
    u = T @ (v * beta[..., None])
    w = T @ (kb * jnp.exp(gc)[..., None])
    att = jnp.einsum('bhnid,bhnjd->bhnij', q, k) * dec
    q_dec = q * jnp.exp(gc)[..., None]
    k_tail = k * jnp.exp(gc[..., -1:] - gc)[..., None]
    g_last = jnp.exp(gc[..., -1])

    def step(S, inp):
        qd, kt, ui, wi, ai, gl = inp
        v_new = ui - wi @ S
        o = qd @ S + ai @ v_new
        S = S * gl[..., None, None] + jnp.swapaxes(kt, -1, -2) @ v_new
        return S, o

    xs = tuple(jnp.moveaxis(t, 2, 0) for t in (q_dec, k_tail, u, w, att, g_last))
    S, o = lax.scan(step, S, xs)
    o = jnp.moveaxis(o, 0, 2).reshape(Bn, H, n * C, dv)
    return jnp.moveaxis(o, 1, 2)[:, :L], S


def gdn_mixer(hB, conv_prev, S0, P):
    Bn, L, _ = hB.shape
    qkv = hB[..., :3 * B_W]
    xcat = jnp.concatenate([conv_prev.astype(hB.dtype), qkv], axis=1)
    cw = P['dn_conv_w']
    conv = sum(cw[j] * xcat[:, j:j + L] for j in range(B_CONV))
    conv = jax.nn.silu(conv.astype(jnp.float32))
    heads = lambda t: t.reshape(Bn, L, B_HEADS, B_HD)
    q = l2norm(heads(conv[..., :B_W]))
    k = l2norm(heads(conv[..., B_W:2 * B_W]))
    v = heads(conv[..., 2 * B_W:])
    o0 = 3 * B_W
    a_raw = hB[..., o0:o0 + B_HEADS].astype(jnp.float32)
    b_raw = hB[..., o0 + B_HEADS:o0 + 2 * B_HEADS].astype(jnp.float32)
    z = hB[..., o0 + 2 * B_HEADS:].astype(jnp.float32)
    g = -jnp.exp(P['dn_a_log'].astype(jnp.float32)) * jax.nn.softplus(a_raw + P['dn_dt_bias'])
    beta = jax.nn.sigmoid(b_raw)
    o, S = gated_delta_chunked(q, k, v, g, beta, S0)
    o = rms_norm(o, P['dn_norm_g']) * jax.nn.silu(heads(z))
    return o.reshape(Bn, L, B_W).astype(hB.dtype), S.astype(S0.dtype), xcat[:, -(B_CONV - 1):]


def diff_attention(q, k, v, lam, rel_bias, q_off):
    Bn, Lq = q.shape[:2]
    Lk = k.shape[1]
    QB = min(Q_BLOCK, Lq)
    nb = -(-Lq // QB)
    qp = jnp.pad(q, ((0, 0), (0, nb * QB - Lq), (0, 0), (0, 0), (0, 0)))
    qb = jnp.moveaxis(qp.reshape(Bn, nb, QB, C_HEADS, 2, C_HD), 1, 0)
    k_pos = jnp.arange(Lk)
    scale = C_HD ** -0.5

    def one_block(args):
        qi, bi = args
        q_pos = q_off + bi * QB + jnp.arange(QB)
        rel = k_pos[None, :] - q_pos[:, None]
        bias = jnp.transpose(rel_bias[t5_bucket(rel)], (2, 0, 1)).astype(jnp.float32)
        s = jnp.einsum('bqhcd,bkhcd->bchqk', qi, k).astype(jnp.float32) * scale + bias
        s = jnp.where(rel <= 0, s, -jnp.inf)
        p = jax.nn.softmax(s, axis=-1)
        pd = p[:, 0] - lam * p[:, 1]
        return jnp.einsum('bhqk,bkhe->bqhe', pd.astype(v.dtype), v)

    o = lax.map(one_block, (qb, jnp.arange(nb)))
    return jnp.moveaxis(o, 0, 1).reshape(Bn, nb * QB, C_HEADS, C_VD)[:, :Lq]


def peer_ffn(xn, wq, keys, u_tab, v_tab):
    Bn, L, D = xn.shape
    T = Bn * L
    TB = min(PK_TOKEN_BLOCK, T)
    nb = -(-T // TB)
    xt = jnp.pad(xn.reshape(T, D), ((0, nb * TB - T), (0, 0))).reshape(nb, TB, D)

    def route_and_mix(xb):
        q = (xb @ wq).reshape(TB, PK_HEADS, 2, PK_HALF)
        s = jnp.einsum('thcd,hcnd->thcn', q, keys).astype(jnp.float32)
        sv, si = lax.top_k(s, PK_TOPK)
        cand = (sv[:, :, 0, :, None] + sv[:, :, 1, None, :]).reshape(TB, PK_HEADS, PK_TOPK * PK_TOPK)
        cidx = (si[:, :, 0, :, None] * PK_NKEYS + si[:, :, 1, None, :]).reshape(TB, PK_HEADS, PK_TOPK * PK_TOPK)
        tv, ti = lax.top_k(cand, PK_TOPK)
        eidx = jnp.take_along_axis(cidx, ti, axis=-1).reshape(TB, PK_HEADS * PK_TOPK)
        gate = jax.nn.softmax(tv, axis=-1).reshape(TB, PK_HEADS * PK_TOPK)
        act = jax.nn.gelu(jnp.einsum('td,tjd->tj', xb, u_tab[eidx]).astype(jnp.float32), approximate=False)
        return jnp.einsum('tj,tjd->td', (gate * act).astype(xb.dtype), v_tab[eidx])

    out = lax.map(route_and_mix, xt).reshape(nb * TB, D)[:T]
    return out.reshape(Bn, L, D).astype(xn.dtype)


def trunk_layer(x, shiftA, SA, convB, SB, past_k, past_v, q_off, P, rel_bias, lam_init):
    Bn, L, _ = x.shape
    h = rms_norm(x, P['rms_mix_g']) @ P['w_in']
    hA = h[..., :A_COLS]
    hB = h[..., A_COLS:A_COLS + B_COLS]
    hC = h[..., A_COLS + B_COLS:]
    oA, SA_new, shift_new = rwkv7_mixer(hA, shiftA, SA, P)
    oB, SB_new, conv_new = gdn_mixer(hB, convB, SB, P)
    q = rms_norm(hC[..., :C_W].reshape(Bn, L, C_HEADS, 2, C_HD), P['df_qn_g'])
    k = rms_norm(hC[..., C_W:2 * C_W].reshape(Bn, L, C_HEADS, 2, C_HD), P['df_kn_g'])
    v = hC[..., 2 * C_W:].reshape(Bn, L, C_HEADS, C_VD)
    if past_k is None:
        k_all, v_all = k, v
    else:
        k_all = jnp.concatenate([past_k.reshape(Bn, -1, C_HEADS, 2, C_HD).astype(k.dtype), k], axis=1)
        v_all = jnp.concatenate([past_v.astype(v.dtype), v], axis=1)
    f32 = jnp.float32
    lam = (jnp.exp(jnp.sum(P['df_lq1'].astype(f32) * P['df_lk1'].astype(f32)))
           - jnp.exp(jnp.sum(P['df_lq2'].astype(f32) * P['df_lk2'].astype(f32))) + lam_init)
    oC = diff_attention(q, k_all, v_all, lam, rel_bias, q_off)
    oC = (rms_norm(oC, P['df_subln_g']) * (1.0 - lam_init)).reshape(Bn, L, C_W).astype(x.dtype)
    x = x + jnp.concatenate([oA, oB, oC], axis=-1) @ P['w_out']
    x = x + peer_ffn(rms_norm(x, P['rms_ffn_g']), P['pk_wq'], P['pk_keys'], P['pk_u'], P['pk_v'])
    return x, k.reshape(Bn, L, C_HEADS, 2 * C_HD), v, SA_new, shift_new, SB_new, conv_new


def setup_inputs(seed: int = 0) -> dict:
    key = jax.random.key(seed)
    ks = iter(jax.random.split(key, 64))
    f32 = jnp.float32
    nrm = lambda shape, s: jax.random.normal(next(ks), shape, f32) * s
    gain = lambda shape: 1.0 + 0.05 * jax.random.normal(next(ks), shape, f32)
    n_pages = PAST_LEN // PAGE_SIZE
    used = DEC_BATCH * n_pages
    n_pool = used + max(1, used // 4)
    page_table = jax.random.permutation(next(ks), n_pool)[:used].reshape(DEC_BATCH, n_pages).astype(jnp.int32)
    dt = jnp.exp(jax.random.uniform(next(ks), (DEPTH, B_HEADS), f32, math.log(1e-3), math.log(1e-1)))
    dt_bias = dt + jnp.log(-jnp.expm1(-dt))
    return {
        'x_prompt': nrm((BATCH, SEQ, D_MODEL), 1.0),
        'x_sample': nrm((DEC_BATCH, DEC_SEQ, D_MODEL), 1.0),
        'cache_k': nrm((n_pool, DEPTH, PAGE_SIZE, C_HEADS, 2 * C_HD), 1.0),
        'cache_v': nrm((n_pool, DEPTH, PAGE_SIZE, C_HEADS, C_VD), 1.0),
        'state_rwkv': nrm((DEPTH, DEC_BATCH, A_HEADS, A_HD, A_HD), 0.1),
        'state_rwkv_shift': nrm((DEPTH, DEC_BATCH, A_COLS), 1.0),
        'state_dn': nrm((DEPTH, DEC_BATCH, B_HEADS, B_HD, B_HD), 0.1),
        'state_dn_conv': nrm((DEPTH, DEC_BATCH, B_CONV - 1, 3 * B_W), 1.0),
        'page_table': page_table,
        'rms_mix_g': gain((DEPTH, D_MODEL)),
        'w_in': nrm((DEPTH, D_MODEL, P_COLS), D_MODEL ** -0.5),
        'w_out': nrm((DEPTH, MIX_WIDTH, D_MODEL), MIX_WIDTH ** -0.5),
        'rw_mu': jax.random.uniform(next(ks), (DEPTH, A_COLS), f32),
        'rw_w0': jax.random.uniform(next(ks), (DEPTH, A_W), f32, -6.0, -0.5),
        'rw_w2': nrm((DEPTH, A_DECAY_LORA, A_W), 0.1 * A_DECAY_LORA ** -0.5),
        'rw_a0': nrm((DEPTH, A_W), 0.1),
        'rw_a2': nrm((DEPTH, A_ICL_LORA, A_W), 0.5 * A_ICL_LORA ** -0.5),
        'rw_g2': nrm((DEPTH, A_GATE_LORA, A_W), A_GATE_LORA ** -0.5),
        'rw_kk': 0.85 + nrm((DEPTH, A_W), 0.05),
        'rw_ka': gain((DEPTH, A_W)),
        'rw_rk': nrm((DEPTH, A_HEADS, A_HD), 0.1),
        'rw_ln_g': gain((DEPTH, A_W)),
        'rw_ln_b': nrm((DEPTH, A_W), 0.01),
        'dn_conv_w': nrm((DEPTH, B_CONV, 3 * B_W), B_CONV ** -0.5),
        'dn_a_log': jnp.log(jax.random.uniform(next(ks), (DEPTH, B_HEADS), f32, 1.0, 16.0)),
        'dn_dt_bias': dt_bias,
        'dn_norm_g': gain((DEPTH, B_HD)),
        'df_qn_g': gain((DEPTH, 2, C_HD)),
        'df_kn_g': gain((DEPTH, 2, C_HD)),
        'df_lq1': nrm((DEPTH, C_HD), 0.1),
        'df_lk1': nrm((DEPTH, C_HD), 0.1),
        'df_lq2': nrm((DEPTH, C_HD), 0.1),
        'df_lk2': nrm((DEPTH, C_HD), 0.1),
        'df_subln_g': gain((DEPTH, C_VD)),
        'rel_bias': nrm((REL_BUCKETS, C_HEADS), 0.5),
        'rms_ffn_g': gain((DEPTH, D_MODEL)),
        'pk_wq': nrm((DEPTH, D_MODEL, PK_HEADS * PK_DIM), D_MODEL ** -0.5),
        'pk_keys': nrm((DEPTH, PK_HEADS, 2, PK_NKEYS, PK_HALF), PK_HALF ** -0.5),
        'pk_u': nrm((DEPTH, PK_EXPERTS, D_MODEL), D_MODEL ** -0.5),
        'pk_v': nrm((DEPTH, PK_EXPERTS, D_MODEL), 0.5 * PK_HEADS ** -0.5),
    }


def reference(x_prompt, x_sample, cache_k, cache_v, state_rwkv, state_rwkv_shift, state_dn, state_dn_conv,
              page_table, rms_mix_g, w_in, w_out, rw_mu, rw_w0, rw_w2, rw_a0, rw_a2, rw_g2, rw_kk, rw_ka,
              rw_rk, rw_ln_g, rw_ln_b, dn_conv_w, dn_a_log, dn_dt_bias, dn_norm_g, df_qn_g, df_kn_g,
              df_lq1, df_lk1, df_lq2, df_lk2, df_subln_g, rel_bias, rms_ffn_g, pk_wq, pk_keys, pk_u, pk_v):
    n_pages = PAST_LEN // PAGE_SIZE
    bp = x_prompt.shape[0]
    bs = x_sample.shape[0]
    dt = x_prompt.dtype
    xp, xs = x_prompt, x_sample
    kp_l, vp_l, ks_l, vs_l = [], [], [], []
    sap_l, sas_l, shp_l, shs_l, sbp_l, sbs_l, cvp_l, cvs_l = [], [], [], [], [], [], [], []
    for l in range(DEPTH):
        P = dict(rms_mix_g=rms_mix_g[l], w_in=w_in[l], w_out=w_out[l], rw_mu=rw_mu[l], rw_w0=rw_w0[l],
                 rw_w2=rw_w2[l], rw_a0=rw_a0[l], rw_a2=rw_a2[l], rw_g2=rw_g2[l], rw_kk=rw_kk[l],
                 rw_ka=rw_ka[l], rw_rk=rw_rk[l], rw_ln_g=rw_ln_g[l], rw_ln_b=rw_ln_b[l],
                 dn_conv_w=dn_conv_w[l], dn_a_log=dn_a_log[l], dn_dt_bias=dn_dt_bias[l],
                 dn_norm_g=dn_norm_g[l], df_qn_g=df_qn_g[l], df_kn_g=df_kn_g[l], df_lq1=df_lq1[l],
                 df_lk1=df_lk1[l], df_lq2=df_lq2[l], df_lk2=df_lk2[l], df_subln_g=df_subln_g[l],
                 rms_ffn_g=rms_ffn_g[l], pk_wq=pk_wq[l], pk_keys=pk_keys[l], pk_u=pk_u[l], pk_v=pk_v[l])
        lam_init = 0.8 - 0.6 * math.exp(-0.3 * l)
        xp, kp, vp, sap, shp, sbp, cvp = trunk_layer(
            xp, jnp.zeros((bp, A_COLS), dt), jnp.zeros((bp, A_HEADS, A_HD, A_HD), dt),
            jnp.zeros((bp, B_CONV - 1, 3 * B_W), dt), jnp.zeros((bp, B_HEADS, B_HD, B_HD), dt),
            None, None, 0, P, rel_bias, lam_init)
        past_k = cache_k[page_table, l].reshape(bs, n_pages * PAGE_SIZE, C_HEADS, 2 * C_HD)
        past_v = cache_v[page_table, l].reshape(bs, n_pages * PAGE_SIZE, C_HEADS, C_VD)
        xs, ksn, vsn, sas, shs, sbs, cvs = trunk_layer(
            xs, state_rwkv_shift[l], state_rwkv[l], state_dn_conv[l], state_dn[l],
            past_k, past_v, n_pages * PAGE_SIZE, P, rel_bias, lam_init)
        kp_l.append(kp); vp_l.append(vp); ks_l.append(ksn); vs_l.append(vsn)
        sap_l.append(sap); sas_l.append(sas); shp_l.append(shp); shs_l.append(shs)
        sbp_l.append(sbp); sbs_l.append(sbs); cvp_l.append(cvp); cvs_l.append(cvs)
    return (xp, xs,
            jnp.stack(kp_l, axis=1), jnp.stack(vp_l, axis=1),
            jnp.stack(ks_l, axis=1), jnp.stack(vs_l, axis=1),
            jnp.stack(sap_l, axis=0), jnp.stack(sas_l, axis=0),
            jnp.stack(shp_l, axis=0), jnp.stack(shs_l, axis=0),
            jnp.stack(sbp_l, axis=0), jnp.stack(sbs_l, axis=0),
            jnp.stack(cvp_l, axis=0), jnp.stack(cvs_l, axis=0))
```

```python
import functools
import math

import jax
import jax.numpy as jnp
from jax import lax
from jax.experimental import pallas as pl
from jax.experimental.pallas import tpu as pltpu

F32 = jnp.float32
BF16 = jnp.bfloat16

D_MODEL = 1024
N_HEADS = 4
HD = 64
MIX_W = N_HEADS * HD
A_COLS = 1024
B_QKV = 3 * MIX_W
B_COLS_PAD = 1152
C_W = 512
NORM_EPS = 1e-6
A_LN_EPS = 64e-5
REL_BUCKETS = 32
REL_MAX_DIST = 128
PK_HEADS = 8
PK_NKEYS = 128
PK_TOPK = 16
V7X_VMEM_LIMIT = 56 * 1024 * 1024


def _parts(a, n):
    out = []
    r = a
    for i in range(n):
        p = r.astype(BF16)
        out.append(p)
        if i + 1 < n:
            r = r - p.astype(F32)
    return out


_NN = (((1,), (0,)), ((), ()))
_NT = (((1,), (1,)), ((), ()))
_TN = (((0,), (0,)), ((), ()))


def _mm(a, b, dims=_NN, na=2, nb=2, order=1):
    ap = _parts(a, na)
    bp = _parts(b, nb)
    acc = None
    for s in range(order, -1, -1):
        for i in range(na):
            j = s - i
            if 0 <= j < nb:
                t = lax.dot_general(ap[i], bp[j], dims, preferred_element_type=F32)
                acc = t if acc is None else acc + t
    return acc


def _mm_f32(a, b, dims=_NN):
    return lax.dot_general(a, b, dims, precision=lax.Precision.HIGHEST, preferred_element_type=F32)


def _mm_mask_r(a, mask, dims=_NN):
    return _mm(a, mask, dims, na=3, nb=1, order=2)


def _mm_mask_l(mask, b, dims=_NN):
    return _mm(mask, b, dims, na=1, nb=3, order=2)


def _iota(shape, dim):
    return lax.broadcasted_iota(jnp.int32, shape, dim)


def _seg_ones(n, seg_shift):
    r = jnp.right_shift(_iota((n, n), 0), seg_shift)
    c = jnp.right_shift(_iota((n, n), 1), seg_shift)
    return (r == c).astype(F32)


def _tile4(z):
    return jnp.concatenate([z, z, z, z], axis=0)


def _fold4(y, c):
    return y[0:c] + y[c:2 * c] + y[2 * c:3 * c] + y[3 * c:4 * c]


def _neumann_inverse(n, steps):
    rows = n.shape[0]
    eye = (_iota((rows, rows), 0) == _iota((rows, rows), 1)).astype(F32)
    t = eye + n
    p = n
    for _ in range(steps - 1):
        p = _mm(p, p)
        t = t + _mm(t, p)
    return t


def _softplus(y):
    return jnp.maximum(y, 0.0) + jnp.log1p(jnp.exp(-jnp.abs(y)))


def _chunk_masks(c_len):
    r = 4 * c_len
    shift = int(math.log2(c_len))
    ri = _iota((r, r), 0)
    ci = _iota((r, r), 1)
    same = jnp.right_shift(ri, shift) == jnp.right_shift(ci, shift)
    incl = jnp.logical_and(same, ci <= ri)
    strict = jnp.logical_and(same, ci < ri)
    head = jnp.right_shift(_iota((r, MIX_W), 0), shift) == jnp.right_shift(_iota((r, MIX_W), 1), 6)
    return incl, strict, head


def _proj_in_kernel(x_ref, g_ref, w_ref, qg_ref, kg_ref, ha_ref, hb_ref, q_ref, k_ref, v_ref):
    x = x_ref[...]
    xn = x * lax.rsqrt(jnp.mean(x * x, axis=-1, keepdims=True) + NORM_EPS) * g_ref[...]
    h = jnp.dot(xn.astype(BF16), w_ref[...], preferred_element_type=F32)
    o = A_COLS
    ha_ref[...] = h[:, :o]
    hb_ref[...] = h[:, o:o + B_COLS_PAD]
    o += B_COLS_PAD
    seg = _seg_ones(C_W, 6)

    def qk_norm(t, g):
        ms = _mm_mask_r(t * t, seg) * (1.0 / HD)
        return t * lax.rsqrt(ms + NORM_EPS) * g

    q_ref[...] = qk_norm(h[:, o:o + C_W], qg_ref[...])
    k_ref[...] = qk_norm(h[:, o + C_W:o + 2 * C_W], kg_ref[...])
    v_ref[...] = h[:, o + 2 * C_W:o + 3 * C_W]


def _proj_in(x2d, g, w_cat, qg, kg, tm=256):
    t = x2d.shape[0]
    n = w_cat.shape[1]
    row = lambda w: pl.BlockSpec((tm, w), lambda i: (i, 0))
    full = lambda a: pl.BlockSpec(a.shape, lambda i: (0,) * a.ndim)
    return pl.pallas_call(
        _proj_in_kernel,
        grid=(t // tm,),
        in_specs=[row(D_MODEL), full(g), full(w_cat), full(qg), full(kg)],
        out_specs=[row(A_COLS), row(B_COLS_PAD), row(C_W), row(C_W), row(C_W)],
        out_shape=[jax.ShapeDtypeStruct((t, A_COLS), F32), jax.ShapeDtypeStruct((t, B_COLS_PAD), F32),
                   jax.ShapeDtypeStruct((t, C_W), F32), jax.ShapeDtypeStruct((t, C_W), F32),
                   jax.ShapeDtypeStruct((t, C_W), F32)],
        compiler_params=pltpu.CompilerParams(dimension_semantics=("parallel",),
                                             vmem_limit_bytes=V7X_VMEM_LIMIT),
        name="proj_in",
    )(x2d, g, w_cat, qg, kg)


def _rwkv_kernel(h_ref, shift_ref, s0_ref, mu_ref, w0_ref, w2_ref, a0_ref, a2_ref, g2_ref, kk_ref, ka_ref,
                 rk_ref, lng_ref, lnb_ref, o_ref, sout_ref, sbd_ref, buf_ref, *, c_len):
    c = pl.program_id(1)
    r_rows = 4 * c_len

    @pl.when(c == 0)
    def _():
        sbd_ref[...] = jnp.zeros_like(sbd_ref)
        for hh in range(N_HEADS):
            sbd_ref[hh * HD:(hh + 1) * HD, hh * HD:(hh + 1) * HD] = s0_ref[hh]
        buf_ref[0:8, :] = jnp.broadcast_to(shift_ref[...], (8, A_COLS))

    x = h_ref[...]
    buf_ref[8:8 + c_len, :] = x
    prev = buf_ref[7:7 + c_len, :]
    buf_ref[0:8, :] = x[c_len - 8:c_len, :]

    hs = x + (prev - x) * mu_ref[...]
    r = hs[:, 0:MIX_W]
    k = hs[:, MIX_W:2 * MIX_W]
    v = hs[:, 2 * MIX_W:3 * MIX_W]
    xw = hs[:, 768:832]
    xa = hs[:, 832:896]
    xg = hs[:, 896:1024]
    w = -_softplus(-(w0_ref[...] + _mm(jnp.tanh(xw), w2_ref[...]))) - 0.5
    a = jax.nn.sigmoid(a0_ref[...] + _mm(xa, a2_ref[...]))
    g = _mm(jax.nn.sigmoid(xg), g2_ref[...])
    seg = _seg_ones(MIX_W, 6)
    kkv = k * kk_ref[...]
    kkn = kkv * lax.rsqrt(_mm_mask_r(kkv * kkv, seg) + 1e-6)
    k2 = k * (1.0 + (a - 1.0) * ka_ref[...])
    logd = -jnp.exp(w)
    tri = (_iota((c_len, c_len), 1) <= _iota((c_len, c_len), 0)).astype(F32)
    cum = _mm_mask_l(tri, logd)
    p_in = jnp.exp(cum)
    p_inv = jnp.exp(-cum)
    p_prev = jnp.exp(cum - logd)

    incl, strict, head = _chunk_masks(c_len)
    hm = head.astype(F32)
    a_t = _tile4(-kkn * p_prev) * hm
    b_t = _tile4(kkn * a * p_inv) * hm
    k_t = _tile4(k2 * p_inv) * hm
    r_t = _tile4(r * p_in) * hm
    v_t = _tile4(v) * hm

    s0 = sbd_ref[...]
    ar = jnp.concatenate([a_t, r_t], axis=0)
    bk = jnp.concatenate([b_t, k_t], axis=0)
    gram = _mm(ar, bk, _NT)
    h0 = _mm(ar, s0, _NT)
    l_ab = jnp.where(strict, gram[:r_rows, :r_rows], 0.0)
    l_ak = jnp.where(strict, gram[:r_rows, r_rows:], 0.0)
    m_rb = jnp.where(incl, gram[r_rows:, :r_rows], 0.0)
    m_rk = jnp.where(incl, gram[r_rows:, r_rows:], 0.0)
    t_inv = _neumann_inverse(l_ab, int(math.log2(c_len)))
    u = _mm(t_inv, h0[:r_rows] + _mm(l_ak, v_t))
    uv = jnp.concatenate([u, v_t], axis=0)
    y = h0[r_rows:] + _mm(jnp.concatenate([m_rb, m_rk], axis=1), uv)
    sbd_ref[...] = (s0 + _mm(uv, bk, _TN)) * p_in[c_len - 1:c_len, :]

    y = _fold4(y, c_len)
    mean = _mm_mask_r(y, seg) * (1.0 / HD)
    yc = y - mean
    var = _mm_mask_r(yc * yc, seg) * (1.0 / HD)
    yn = yc * lax.rsqrt(var + A_LN_EPS) * lng_ref[...] + lnb_ref[...]
    bonus = _mm_mask_r(r * k2 * rk_ref[...], seg)
    o_ref[...] = (yn + bonus * v) * g

    @pl.when(c == pl.num_programs(1) - 1)
    def _():
        for hh in range(N_HEADS):
            sout_ref[hh] = sbd_ref[hh * HD:(hh + 1) * HD, hh * HD:(hh + 1) * HD]


def _rwkv(h_a, shift_prev, s0, p, c_len):
    bn, seq, _ = h_a.shape
    nc = seq // c_len
    params = [p[n] for n in ("rw_mu", "rw_w0", "rw_w2", "rw_a0", "rw_a2", "rw_g2", "rw_kk", "rw_ka", "rw_rk",
                             "rw_ln_g", "rw_ln_b")]
    full = lambda a: pl.BlockSpec(a.shape, lambda b, c: (0,) * a.ndim)
    return pl.pallas_call(
        functools.partial(_rwkv_kernel, c_len=c_len),
        grid=(bn, nc),
        in_specs=[pl.BlockSpec((None, c_len, A_COLS), lambda b, c: (b, c, 0)),
                  pl.BlockSpec((None, 1, A_COLS), lambda b, c: (b, 0, 0)),
                  pl.BlockSpec((None, N_HEADS, HD, HD), lambda b, c: (b, 0, 0, 0))] + [full(a) for a in params],
        out_specs=[pl.BlockSpec((None, c_len, MIX_W), lambda b, c: (b, c, 0)),
                   pl.BlockSpec((None, N_HEADS, HD, HD), lambda b, c: (b, 0, 0, 0))],
        out_shape=[jax.ShapeDtypeStruct((bn, seq, MIX_W), F32),
                   jax.ShapeDtypeStruct((bn, N_HEADS, HD, HD), F32)],
        scratch_shapes=[pltpu.VMEM((MIX_W, MIX_W), F32), pltpu.VMEM((8 + c_len, A_COLS), F32)],
        compiler_params=pltpu.CompilerParams(dimension_semantics=("parallel", "arbitrary")),
        name="rwkv",
    )(h_a, shift_prev.reshape(bn, 1, A_COLS), s0, *params)


def _gdn_kernel(h_ref, conv0_ref, s0_ref, cw_ref, alog_ref, dtb_ref, ng_ref, o_ref, sout_ref, sbd_ref, buf_ref,
                *, c_len):
    c = pl.program_id(1)
    r_rows = 4 * c_len

    @pl.when(c == 0)
    def _():
        sbd_ref[...] = jnp.zeros_like(sbd_ref)
        for hh in range(N_HEADS):
            sbd_ref[hh * HD:(hh + 1) * HD, hh * HD:(hh + 1) * HD] = s0_ref[hh]
        buf_ref[0:8, :] = conv0_ref[...]

    hb = h_ref[...]
    qkv = hb[:, :B_QKV]
    buf_ref[8:8 + c_len, :] = qkv
    cw = cw_ref[...]
    conv = cw[3:4, :] * qkv
    for j in range(3):
        conv = conv + cw[j:j + 1, :] * buf_ref[5 + j:5 + j + c_len, :]
    buf_ref[0:8, :] = qkv[c_len - 8:c_len, :]
    conv = conv * jax.nn.sigmoid(conv)
    seg = _seg_ones(MIX_W, 6)

    def l2n(t):
        return t * lax.rsqrt(_mm_mask_r(t * t, seg) + 1e-6)

    q = l2n(conv[:, :MIX_W]) * (HD ** -0.5)
    k = l2n(conv[:, MIX_W:2 * MIX_W])
    v = conv[:, 2 * MIX_W:]
    z = hb[:, B_QKV:B_QKV + MIX_W]
    ab = hb[:, B_QKV + MIX_W:]
    gfull = -jnp.exp(alog_ref[...]) * _softplus(ab + dtb_ref[...])
    bfull = jax.nn.sigmoid(ab)

    incl, strict, head = _chunk_masks(c_len)
    hm = head.astype(F32)
    shift = int(math.log2(c_len))
    hrow = jnp.right_shift(_iota((r_rows, 128), 0), shift)
    lane = _iota((r_rows, 128), 1)
    gcol = jnp.sum(jnp.where(lane == hrow, _tile4(gfull), 0.0), axis=1, keepdims=True)
    bcol = jnp.sum(jnp.where(lane == hrow + N_HEADS, _tile4(bfull), 0.0), axis=1, keepdims=True)

    incl_f = incl.astype(F32)
    strict_f = strict.astype(F32)
    dmat = _mm_mask_l(incl_f, jnp.broadcast_to(gcol, (r_rows, r_rows)) * strict_f)
    g128 = jnp.broadcast_to(gcol, (r_rows, 128))
    gc = _mm_mask_l(incl_f, g128)[:, 0:1]
    ri = _iota((r_rows, r_rows), 0)
    ci = _iota((r_rows, r_rows), 1)
    after_f = jnp.logical_and(jnp.right_shift(ri, shift) == jnp.right_shift(ci, shift), ci > ri).astype(F32)
    tail = _mm_mask_l(after_f, g128)[:, 0:1]
    e_gc = jnp.exp(gc)
    e_tail = jnp.exp(tail)
    first = jnp.bitwise_and(_iota((r_rows, MIX_W), 0), c_len - 1) == 0
    g_last = jnp.sum(jnp.where(first, jnp.exp(gc + tail) * hm, 0.0), axis=0, keepdims=True)

    q_h = _tile4(q) * hm
    k_h = _tile4(k) * hm
    v_h = _tile4(v) * hm
    kb = k_h * bcol
    e_d = jnp.exp(dmat)
    m1 = _mm(jnp.concatenate([kb, q_h], axis=0), k_h, _NT)
    a_mat = jnp.where(strict, m1[:r_rows] * e_d, 0.0)
    att = jnp.where(incl, m1[r_rows:] * e_d, 0.0)
    t_inv = _neumann_inverse(-a_mat, int(math.log2(c_len)))
    uw = _mm(t_inv, jnp.concatenate([v_h * bcol, kb * e_gc], axis=1))
    s0 = sbd_ref[...]
    v_new = uw[:, :MIX_W] - _mm(uw[:, MIX_W:], s0)
    o = _mm(q_h * e_gc, s0) + _mm(att, v_new)
    sbd_ref[...] = s0 * g_last + _mm(k_h * e_tail, v_new, _TN)

    o = _fold4(o, c_len)
    ms = _mm_mask_r(o * o, seg) * (1.0 / HD)
    o = o * lax.rsqrt(ms + NORM_EPS) * ng_ref[...]
    o_ref[...] = o * (z * jax.nn.sigmoid(z))

    @pl.when(c == pl.num_programs(1) - 1)
    def _():
        for hh in range(N_HEADS):
            sout_ref[hh] = sbd_ref[hh * HD:(hh + 1) * HD, hh * HD:(hh + 1) * HD]


def _gdn(h_b, conv_prev8, s0, cw, alog, dtb, ng, c_len):
    bn, seq, _ = h_b.shape
    nc = seq // c_len
    params = [cw, alog, dtb, ng]
    full = lambda a: pl.BlockSpec(a.shape, lambda b, c: (0,) * a.ndim)
    return pl.pallas_call(
        functools.partial(_gdn_kernel, c_len=c_len),
        grid=(bn, nc),
        in_specs=[pl.BlockSpec((None, c_len, B_COLS_PAD), lambda b, c: (b, c, 0)),
                  pl.BlockSpec((None, 8, B_QKV), lambda b, c: (b, 0, 0)),
                  pl.BlockSpec((None, N_HEADS, HD, HD), lambda b, c: (b, 0, 0, 0))] + [full(a) for a in params],
        out_specs=[pl.BlockSpec((None, c_len, MIX_W), lambda b, c: (b, c, 0)),
                   pl.BlockSpec((None, N_HEADS, HD, HD), lambda b, c: (b, 0, 0, 0))],
        out_shape=[jax.ShapeDtypeStruct((bn, seq, MIX_W), F32),
                   jax.ShapeDtypeStruct((bn, N_HEADS, HD, HD), F32)],
        scratch_shapes=[pltpu.VMEM((MIX_W, MIX_W), F32), pltpu.VMEM((8 + c_len, B_QKV), F32)],
        compiler_params=pltpu.CompilerParams(dimension_semantics=("parallel", "arbitrary")),
        name="gdn",
    )(h_b, conv_prev8, s0, *params)


def _attn_prompt_kernel(lam_ref, q_ref, k_ref, v_ref, bias_ref, sg_ref, o_ref, m_ref, l_ref, acc_ref,
                        *, tq, scale, out_scale):
    i = pl.program_id(1)
    j = pl.program_id(2)

    @pl.when(j == 0)
    def _():
        m_ref[...] = jnp.full_like(m_ref, -jnp.inf)
        l_ref[...] = jnp.zeros_like(l_ref)
        acc_ref[...] = jnp.zeros_like(acc_ref)

    @pl.when(j <= i)
    def _():
        lane = _iota((tq, 2 * HD), 1)
        causal = _iota((2 * tq, tq), 1) <= jnp.bitwise_and(_iota((2 * tq, tq), 0), tq - 1)
        for hh in range(N_HEADS):
            sl = slice(hh * 2 * HD, (hh + 1) * 2 * HD)
            qh = q_ref[:, sl]
            q2 = jnp.concatenate([jnp.where(lane < HD, qh, 0.0), jnp.where(lane >= HD, qh, 0.0)], axis=0)
            s = lax.dot_general(q2.astype(BF16), k_ref[:, sl].astype(BF16), _NT, preferred_element_type=F32)
            bias = bias_ref[hh, jnp.minimum(i - j, 2)]
            s = s * scale + jnp.concatenate([bias, bias], axis=0)
            s = jnp.where(jnp.logical_or(j < i, causal), s, -jnp.inf)
            m_old = m_ref[hh]
            m_new = jnp.maximum(m_old, jnp.max(s, axis=-1, keepdims=True))
            alpha = jnp.exp(m_old - m_new)
            p = jnp.exp(s - m_new)
            l_ref[hh] = alpha * l_ref[hh] + jnp.sum(p, axis=-1, keepdims=True)
            acc_ref[hh] = alpha * acc_ref[hh] + jnp.dot(p.astype(BF16), v_ref[:, sl].astype(BF16),
                                                        preferred_element_type=F32)
            m_ref[hh] = m_new

    @pl.when(j == i)
    def _():
        lam = lam_ref[0]
        for hh in range(N_HEADS):
            o = acc_ref[hh] / l_ref[hh]
            od = o[:tq] - lam * o[tq:]
            od = od * lax.rsqrt(jnp.mean(od * od, axis=-1, keepdims=True) + NORM_EPS) * sg_ref[...] * out_scale
            o_ref[:, hh * 2 * HD:(hh + 1) * 2 * HD] = od


def _attn_prompt(q, k, v, bias_near, lam, subln_g, out_scale, tq=256):
    bn, seq, _ = q.shape
    nq = seq // tq
    kv_spec = pl.BlockSpec((None, tq, C_W), lambda b, i, j: (b, jnp.minimum(i, j), 0))
    return pl.pallas_call(
        functools.partial(_attn_prompt_kernel, tq=tq, scale=HD ** -0.5, out_scale=out_scale),
        grid=(bn, nq, nq),
        in_specs=[pl.BlockSpec(memory_space=pltpu.SMEM),
                  pl.BlockSpec((None, tq, C_W), lambda b, i, j: (b, i, 0)),
                  kv_spec, kv_spec,
                  pl.BlockSpec(bias_near.shape, lambda b, i, j: (0, 0, 0, 0)),
                  pl.BlockSpec(subln_g.shape, lambda b, i, j: (0, 0))],
        out_specs=pl.BlockSpec((None, tq, C_W), lambda b, i, j: (b, i, 0)),
        out_shape=jax.ShapeDtypeStruct((bn, seq, C_W), F32),
        scratch_shapes=[pltpu.VMEM((N_HEADS, 2 * tq, 1), F32), pltpu.VMEM((N_HEADS, 2 * tq, 1), F32),
                        pltpu.VMEM((N_HEADS, 2 * tq, 2 * HD), F32)],
        compiler_params=pltpu.CompilerParams(dimension_semantics=("parallel", "parallel", "arbitrary")),
        name="attn_prompt",
    )(lam, q, k, v, bias_near, subln_g)


def _attn_sample_kernel(pt_ref, lam_ref, q_ref, kn_ref, vn_ref, kp_ref, vp_ref, bp_ref, bn_ref, sg_ref, o_ref,
                        qz_ref, m_ref, l_ref, acc_ref, *, lq, scale, out_scale):
    del pt_ref
    p_idx = pl.program_id(1)
    rows = 2 * N_HEADS * lq

    @pl.when(p_idx == 0)
    def _():
        q = q_ref[...]
        qt = jnp.concatenate([q] * (2 * N_HEADS), axis=0)
        seg_r = jnp.right_shift(_iota((rows, C_W), 0), int(math.log2(lq)))
        seg_c = jnp.right_shift(_iota((rows, C_W), 1), 6)
        qz = jnp.where(seg_r == seg_c, qt, 0.0)
        qz_ref[...] = qz
        s = _mm_f32(qz, kn_ref[...], _NT) * scale + bn_ref[...]
        m = jnp.max(s, axis=-1, keepdims=True)
        p = jnp.exp(s - m)
        m_ref[...] = m
        l_ref[...] = jnp.sum(p, axis=-1, keepdims=True)
        acc_ref[...] = _mm_f32(p, vn_ref[...])

    s = lax.dot_general(qz_ref[...].astype(BF16), kp_ref[...].astype(BF16), _NT, preferred_element_type=F32)
    s = s * scale + bp_ref[...]
    m_old = m_ref[...]
    m_new = jnp.maximum(m_old, jnp.max(s, axis=-1, keepdims=True))
    alpha = jnp.exp(m_old - m_new)
    p = jnp.exp(s - m_new)
    l_ref[...] = alpha * l_ref[...] + jnp.sum(p, axis=-1, keepdims=True)
    acc_ref[...] = alpha * acc_ref[...] + jnp.dot(p.astype(BF16), vp_ref[...].astype(BF16),
                                                  preferred_element_type=F32)
    m_ref[...] = m_new

    @pl.when(p_idx == pl.num_programs(1) - 1)
    def _():
        lam = lam_ref[0]
        o = acc_ref[...] / l_ref[...]
        for hh in range(N_HEADS):
            sl = slice(hh * 2 * HD, (hh + 1) * 2 * HD)
            o1 = o[(2 * hh) * lq:(2 * hh + 1) * lq, sl]
            o2 = o[(2 * hh + 1) * lq:(2 * hh + 2) * lq, sl]
            od = o1 - lam * o2
            od = od * lax.rsqrt(jnp.mean(od * od, axis=-1, keepdims=True) + NORM_EPS) * sg_ref[...] * out_scale
            o_ref[:, sl] = od


def _attn_sample(q, k_new, v_new, cache_k, cache_v, page_table, layer, bias_past, bias_new, lam, subln_g,
                 out_scale):
    bn, lq, _ = q.shape
    n_pages = page_table.shape[1]
    page = cache_k.shape[2]
    rows = 2 * N_HEADS * lq
    seq_spec = pl.BlockSpec((None, lq, C_W), lambda b, p, pt: (b, 0, 0))
    page_spec = pl.BlockSpec((None, None, page, C_W), lambda b, p, pt: (pt[b, p], layer, 0, 0))
    grid_spec = pltpu.PrefetchScalarGridSpec(
        num_scalar_prefetch=1,
        grid=(bn, n_pages),
        in_specs=[pl.BlockSpec(memory_space=pltpu.SMEM),
                  seq_spec, seq_spec, seq_spec, page_spec, page_spec,
                  pl.BlockSpec((rows, page), lambda b, p, pt: (0, p)),
                  pl.BlockSpec(bias_new.shape, lambda b, p, pt: (0, 0)),
                  pl.BlockSpec(subln_g.shape, lambda b, p, pt: (0, 0))],
        out_specs=seq_spec,
        scratch_shapes=[pltpu.VMEM((rows, C_W), F32), pltpu.VMEM((rows, 1), F32), pltpu.VMEM((rows, 1), F32),
                        pltpu.VMEM((rows, C_W), F32)],
    )
    return pl.pallas_call(
        functools.partial(_attn_sample_kernel, lq=lq, scale=HD ** -0.5, out_scale=out_scale),
        grid_spec=grid_spec,
        out_shape=jax.ShapeDtypeStruct((bn, lq, C_W), F32),
        compiler_params=pltpu.CompilerParams(dimension_semantics=("parallel", "arbitrary")),
        name="attn_sample",
    )(page_table, lam, q, k_new, v_new, cache_k, cache_v, bias_past, bias_new, subln_g)


def _proj_out_kernel(x_ref, oa_ref, ob_ref, oc_ref, wo_ref, g_ref, wq_hi_ref, wq_lo_ref, xo_ref, xnt_ref, q_ref):
    dot = lambda a, w: jnp.dot(a.astype(BF16), w, preferred_element_type=F32)
    x = x_ref[...]
    x = x + dot(oa_ref[...], wo_ref[0:MIX_W, :]) + dot(ob_ref[...], wo_ref[MIX_W:2 * MIX_W, :]) \
        + dot(oc_ref[...], wo_ref[2 * MIX_W:, :])
    xo_ref[...] = x
    xn = x * lax.rsqrt(jnp.mean(x * x, axis=-1, keepdims=True) + NORM_EPS) * g_ref[...]
    xnt_ref[...] = jnp.transpose(xn).astype(BF16)
    hi, lo = _parts(xn, 2)
    q_ref[...] = (jnp.dot(lo, wq_hi_ref[...], preferred_element_type=F32)
                  + jnp.dot(hi, wq_lo_ref[...], preferred_element_type=F32)
                  + jnp.dot(hi, wq_hi_ref[...], preferred_element_type=F32))


def _proj_out(x2d, o_a, o_b, o_c, w_out, g, wq_hi, wq_lo, tm=256):
    t = x2d.shape[0]
    nq = wq_hi.shape[1]
    row = lambda w: pl.BlockSpec((tm, w), lambda i: (i, 0))
    full = lambda a: pl.BlockSpec(a.shape, lambda i: (0,) * a.ndim)
    return pl.pallas_call(
        _proj_out_kernel,
        grid=(t // tm,),
        in_specs=[row(D_MODEL), row(MIX_W), row(MIX_W), row(C_W), full(w_out), full(g), full(wq_hi), full(wq_lo)],
        out_specs=[row(D_MODEL), pl.BlockSpec((D_MODEL, tm), lambda i: (0, i)), row(nq)],
        out_shape=[jax.ShapeDtypeStruct((t, D_MODEL), F32), jax.ShapeDtypeStruct((D_MODEL, t), BF16),
                   jax.ShapeDtypeStruct((t, nq), F32)],
        compiler_params=pltpu.CompilerParams(dimension_semantics=("parallel",),
                                             vmem_limit_bytes=V7X_VMEM_LIMIT),
        name="proj_out",
    )(x2d, o_a, o_b, o_c, w_out, g, wq_hi, wq_lo)


def _top16(s, n_rows):
    tm = s.shape[1]
    rid = _iota((n_rows, tm), 0).astype(F32)
    kid = _iota((PK_TOPK, tm), 0)
    vals = jnp.zeros((PK_TOPK, tm), F32)
    idxs = jnp.zeros((PK_TOPK, tm), F32)
    for kk in range(PK_TOPK):
        m = jnp.max(s, axis=0, keepdims=True)
        ix = jnp.min(jnp.where(s == m, rid, float(n_rows)), axis=0, keepdims=True)
        vals = jnp.where(kid == kk, m, vals)
        idxs = jnp.where(kid == kk, ix, idxs)
        s = jnp.where(rid == ix, -jnp.inf, s)
    return vals, idxs


def _peer_route_kernel(q_ref, keys_ref, e1_ref, f_ref, e2_ref, r_ref):
    tm = q_ref.shape[0]
    q = q_ref[...]
    s1 = _mm(keys_ref[0], q[:, :PK_NKEYS], _NT)
    s2 = _mm(keys_ref[1], q[:, PK_NKEYS:], _NT)
    sv1, si1 = _top16(s1, PK_NKEYS)
    sv2, si2 = _top16(s2, PK_NKEYS)
    cand = jnp.concatenate([sv1[a:a + 1, :] + sv2 for a in range(PK_TOPK)], axis=0)
    tv, ti = _top16(cand, PK_TOPK * PK_TOPK)
    rank_a = jnp.floor(ti * (1.0 / PK_TOPK))
    z = jnp.sum(jnp.exp(tv - tv[0:1, :]), axis=0, keepdims=True)
    aid = _iota((PK_TOPK, tm), 0).astype(F32)
    cnt = jnp.zeros((PK_TOPK, tm), F32)
    for kk in range(PK_TOPK):
        cnt = cnt + (aid == rank_a[kk:kk + 1, :]).astype(F32)
    ex1 = jnp.exp(sv1 - sv1[0:1, :]) / z
    ex2 = jnp.exp(sv2 - sv2[0:1, :])
    kid = _iota((PK_NKEYS, tm), 0).astype(F32)
    e1 = jnp.zeros((PK_NKEYS, tm), F32)
    f = jnp.zeros((PK_NKEYS, tm), F32)
    e2 = jnp.zeros((PK_NKEYS, tm), F32)
    r = jnp.full((PK_NKEYS, tm), float(PK_TOPK), F32)
    for kk in range(PK_TOPK):
        hit1 = kid == si1[kk:kk + 1, :]
        e1 = jnp.where(hit1, ex1[kk:kk + 1, :], e1)
        f = jnp.where(hit1, cnt[kk:kk + 1, :], f)
        hit2 = kid == si2[kk:kk + 1, :]
        e2 = jnp.where(hit2, ex2[kk:kk + 1, :], e2)
        r = jnp.where(hit2, float(kk), r)
    e1_ref[...] = e1
    f_ref[...] = f
    e2_ref[...] = e2
    r_ref[...] = r


def _peer_route(q, keys, tm=256):
    t = q.shape[0]
    tab = jax.ShapeDtypeStruct((PK_HEADS, PK_NKEYS, t), F32)
    tab_spec = pl.BlockSpec((None, PK_NKEYS, tm), lambda i, h: (h, 0, i))
    return pl.pallas_call(
        _peer_route_kernel,
        grid=(t // tm, PK_HEADS),
        in_specs=[pl.BlockSpec((tm, 2 * PK_NKEYS), lambda i, h: (i, h)),
                  pl.BlockSpec((None, 2, PK_NKEYS, PK_NKEYS), lambda i, h: (h, 0, 0, 0))],
        out_specs=[tab_spec] * 4,
        out_shape=[tab] * 4,
        compiler_params=pltpu.CompilerParams(dimension_semantics=("parallel", "parallel")),
        name="peer_route",
    )(q, keys)


def _peer_dense_kernel(x_ref, xnt_ref, u_ref, vt_ref, e1_ref, f_ref, e2_ref, r_ref, o_ref, acc_ref, *, ib):
    ii = pl.program_id(1)

    @pl.when(ii == 0)
    def _():
        acc_ref[...] = jnp.zeros_like(acc_ref)

    xnt = xnt_ref[...]
    for s in range(ib):
        act = jnp.dot(u_ref[s * PK_NKEYS:(s + 1) * PK_NKEYS, :], xnt, preferred_element_type=F32)
        act = 0.5 * act * (1.0 + lax.erf(act * math.sqrt(0.5)))
        gate = jnp.zeros_like(act)
        for hh in range(PK_HEADS):
            e1 = e1_ref[hh, s:s + 1, :]
            f = f_ref[hh, s:s + 1, :]
            gate = gate + e1 * jnp.where(r_ref[hh] < f, e2_ref[hh], 0.0)
        c = (gate * act).astype(BF16)
        acc_ref[...] += jnp.dot(vt_ref[:, s * PK_NKEYS:(s + 1) * PK_NKEYS], c, preferred_element_type=F32)

    @pl.when(ii == pl.num_programs(1) - 1)
    def _():
        o_ref[...] = x_ref[...] + jnp.transpose(acc_ref[...])


def _peer_dense(x2d, xnt, u_bf, vt_bf, e1, f, e2, r, tm=512, ib=8):
    t = x2d.shape[0]
    ni = PK_NKEYS // ib
    tab_i = pl.BlockSpec((PK_HEADS, ib, tm), lambda tt, ii: (0, ii, tt))
    tab_j = pl.BlockSpec((PK_HEADS, PK_NKEYS, tm), lambda tt, ii: (0, 0, tt))
    return pl.pallas_call(
        functools.partial(_peer_dense_kernel, ib=ib),
        grid=(t // tm, ni),
        in_specs=[pl.BlockSpec((tm, D_MODEL), lambda tt, ii: (tt, 0)),
                  pl.BlockSpec((D_MODEL, tm), lambda tt, ii: (0, tt)),
                  pl.BlockSpec((ib * PK_NKEYS, D_MODEL), lambda tt, ii: (ii, 0)),
                  pl.BlockSpec((D_MODEL, ib * PK_NKEYS), lambda tt, ii: (0, ii)),
                  tab_i, tab_i, tab_j, tab_j],
        out_specs=pl.BlockSpec((tm, D_MODEL), lambda tt, ii: (tt, 0)),
        out_shape=jax.ShapeDtypeStruct((t, D_MODEL), F32),
        scratch_shapes=[pltpu.VMEM((D_MODEL, tm), F32)],
        compiler_params=pltpu.CompilerParams(dimension_semantics=("parallel", "arbitrary"),
                                             vmem_limit_bytes=V7X_VMEM_LIMIT),
        name="peer_dense",
    )(x2d, xnt, u_bf, vt_bf, e1, f, e2, r)


def _t5_bucket(rel):
    n = jnp.maximum(-rel, 0)
    max_exact = REL_BUCKETS // 2
    nf = jnp.maximum(n, 1).astype(F32)
    large = max_exact + (jnp.log(nf / max_exact) / math.log(REL_MAX_DIST / max_exact)
                         * (REL_BUCKETS - max_exact)).astype(jnp.int32)
    large = jnp.minimum(large, REL_BUCKETS - 1)
    return jnp.where(n < max_exact, n, large)


def _bias_table(rel_bias, rel):
    return jnp.moveaxis(rel_bias[_t5_bucket(rel)], -1, 0).astype(F32)


def _prep_layer(l, W):
    row = lambda a: a.reshape(1, -1).astype(F32)
    w_in = W["w_in"][l]
    o_b = A_COLS
    o_c = A_COLS + 4 * MIX_W + 2 * N_HEADS
    w_cat = jnp.concatenate([
        w_in[:, :A_COLS],
        w_in[:, o_b:o_b + B_QKV],
        w_in[:, o_b + B_QKV + 2 * N_HEADS:o_c],
        w_in[:, o_b + B_QKV:o_b + B_QKV + 2 * N_HEADS],
        jnp.zeros((D_MODEL, 128 - 2 * N_HEADS), F32),
        w_in[:, o_c:],
    ], axis=1).astype(BF16)
    pad_lane = lambda a, off: jnp.zeros((1, 128), F32).at[0, off:off + a.shape[0]].set(a)
    alog = pad_lane(W["dn_a_log"][l], 0)
    dtb = pad_lane(W["dn_dt_bias"][l], 0)
    f32 = F32
    lam_init = 0.8 - 0.6 * math.exp(-0.3 * l)
    lam = (jnp.exp(jnp.sum(W["df_lq1"][l].astype(f32) * W["df_lk1"][l].astype(f32)))
           - jnp.exp(jnp.sum(W["df_lq2"][l].astype(f32) * W["df_lk2"][l].astype(f32))) + lam_init)
    wq = W["pk_wq"][l]
    wq_hi = wq.astype(BF16)
    wq_lo = (wq - wq_hi.astype(F32)).astype(BF16)
    return dict(
        rms_mix_g=row(W["rms_mix_g"][l]), w_cat=w_cat,
        qg=row(jnp.tile(W["df_qn_g"][l].reshape(-1), N_HEADS)),
        kg=row(jnp.tile(W["df_kn_g"][l].reshape(-1), N_HEADS)),
        rw_mu=row(W["rw_mu"][l]), rw_w0=row(W["rw_w0"][l]), rw_w2=W["rw_w2"][l], rw_a0=row(W["rw_a0"][l]),
        rw_a2=W["rw_a2"][l], rw_g2=W["rw_g2"][l], rw_kk=row(W["rw_kk"][l]), rw_ka=row(W["rw_ka"][l]),
        rw_rk=row(W["rw_rk"][l]), rw_ln_g=row(W["rw_ln_g"][l]), rw_ln_b=row(W["rw_ln_b"][l]),
        dn_conv_w=W["dn_conv_w"][l], dn_alog=alog, dn_dtb=dtb,
        dn_norm_g=row(jnp.tile(W["dn_norm_g"][l], N_HEADS)),
        lam=lam.reshape(1).astype(F32), lam_init=lam_init, subln_g=row(W["df_subln_g"][l]),
        w_out=W["w_out"][l].astype(BF16), rms_ffn_g=row(W["rms_ffn_g"][l]), wq_hi=wq_hi, wq_lo=wq_lo,
        pk_keys=W["pk_keys"][l], pk_u=W["pk_u"][l].astype(BF16), pk_vt=jnp.transpose(W["pk_v"][l]).astype(BF16),
    )


def _layer(x, shift_prev, s_rwkv, conv_prev, s_dn, attn_fn, P, c_len):
    bn, seq, _ = x.shape
    t = bn * seq
    x2d = x.reshape(t, D_MODEL)
    h_a, h_b, q, k, v = _proj_in(x2d, P["rms_mix_g"], P["w_cat"], P["qg"], P["kg"])
    h_a3 = h_a.reshape(bn, seq, A_COLS)
    h_b3 = h_b.reshape(bn, seq, B_COLS_PAD)
    o_a, s_rwkv_new = _rwkv(h_a3, shift_prev, s_rwkv, P, c_len)
    conv8 = jnp.concatenate([jnp.zeros((bn, 5, B_QKV), F32), conv_prev], axis=1)
    o_b, s_dn_new = _gdn(h_b3, conv8, s_dn, P["dn_conv_w"], P["dn_alog"], P["dn_dtb"], P["dn_norm_g"], c_len)
    q3, k3, v3 = (a.reshape(bn, seq, C_W) for a in (q, k, v))
    o_c = attn_fn(q3, k3, v3)
    x_new, xnt, pq = _proj_out(x2d, o_a.reshape(t, MIX_W), o_b.reshape(t, MIX_W), o_c.reshape(t, C_W),
                               P["w_out"], P["rms_ffn_g"], P["wq_hi"], P["wq_lo"])
    e1, f, e2, r = _peer_route(pq, P["pk_keys"])
    y = _peer_dense(x_new, xnt, P["pk_u"], P["pk_vt"], e1, f, e2, r)
    k_rows = k.reshape(bn, seq, N_HEADS, 2 * HD)
    v_rows = v.reshape(bn, seq, N_HEADS, 2 * HD)
    shift_new = h_a3[:, -1]
    conv_new = h_b3[:, -3:, :B_QKV]
    return y.reshape(bn, seq, D_MODEL), k_rows, v_rows, s_rwkv_new, shift_new, s_dn_new, conv_new


def kernel(x_prompt, x_sample, cache_k, cache_v, state_rwkv, state_rwkv_shift, state_dn, state_dn_conv, page_table, rms_mix_g, w_in, w_out, rw_mu, rw_w0, rw_w2, rw_a0, rw_a2, rw_g2, rw_kk, rw_ka, rw_rk, rw_ln_g, rw_ln_b, dn_conv_w, dn_a_log, dn_dt_bias, dn_norm_g, df_qn_g, df_kn_g, df_lq1, df_lk1, df_lq2, df_lk2, df_subln_g, rel_bias, rms_ffn_g, pk_wq, pk_keys, pk_u, pk_v):
    W = dict(rms_mix_g=rms_mix_g, w_in=w_in, w_out=w_out, rw_mu=rw_mu, rw_w0=rw_w0, rw_w2=rw_w2, rw_a0=rw_a0,
             rw_a2=rw_a2, rw_g2=rw_g2, rw_kk=rw_kk, rw_ka=rw_ka, rw_rk=rw_rk, rw_ln_g=rw_ln_g, rw_ln_b=rw_ln_b,
             dn_conv_w=dn_conv_w, dn_a_log=dn_a_log, dn_dt_bias=dn_dt_bias, dn_norm_g=dn_norm_g,
             df_qn_g=df_qn_g, df_kn_g=df_kn_g, df_lq1=df_lq1, df_lk1=df_lk1, df_lq2=df_lq2, df_lk2=df_lk2,
             df_subln_g=df_subln_g, rms_ffn_g=rms_ffn_g, pk_wq=pk_wq, pk_keys=pk_keys, pk_u=pk_u, pk_v=pk_v)
    depth = w_in.shape[0]
    bp, seq_p, _ = x_prompt.shape
    bs, seq_s, _ = x_sample.shape
    n_pages = page_table.shape[1]
    page = cache_k.shape[2]
    past = n_pages * page
    tq = 256
    ck = cache_k.reshape(cache_k.shape[0], depth, page, C_W)
    cv = cache_v.reshape(cache_v.shape[0], depth, page, C_W)

    rel_near = jnp.arange(tq)[None, :] - jnp.arange(tq)[:, None]
    bias_near = jnp.stack([_bias_table(rel_bias, rel_near - d * tq) for d in range(3)], axis=1)
    q_pos = past + jnp.arange(seq_s)
    expand = lambda b: jnp.broadcast_to(b[:, None], (N_HEADS, 2) + b.shape[1:]).reshape(2 * N_HEADS * seq_s, -1)
    bias_past = expand(_bias_table(rel_bias, jnp.arange(past)[None, :] - q_pos[:, None]))
    rel_new = q_pos[None, :] - q_pos[:, None]
    bias_new = expand(jnp.where(rel_new <= 0, _bias_table(rel_bias, rel_new), -jnp.inf))

    xp, xs = x_prompt, x_sample
    outs = [[] for _ in range(12)]
    zeros = lambda *s: jnp.zeros(s, F32)
    for l in range(depth):
        P = _prep_layer(l, W)
        out_scale = 1.0 - P["lam_init"]
        attn_p = lambda q, k, v: _attn_prompt(q, k, v, bias_near, P["lam"], P["subln_g"], out_scale, tq)
        attn_s = lambda q, k, v: _attn_sample(q, k, v, ck, cv, page_table, l, bias_past, bias_new, P["lam"],
                                              P["subln_g"], out_scale)
        xp, kp, vp, sap, shp, sbp, cvp = _layer(
            xp, zeros(bp, A_COLS), zeros(bp, N_HEADS, HD, HD), zeros(bp, 3, B_QKV), zeros(bp, N_HEADS, HD, HD),
            attn_p, P, 64)
        xs, ksn, vsn, sas, shs, sbs, cvs = _layer(
            xs, state_rwkv_shift[l], state_rwkv[l], state_dn_conv[l], state_dn[l], attn_s, P, seq_s)
        for lst, val in zip(outs, (kp, vp, ksn, vsn, sap, sas, shp, shs, sbp, sbs, cvp, cvs)):
            lst.append(val)
    stack = lambda i, ax: jnp.stack(outs[i], axis=ax)
    return (xp, xs, stack(0, 1), stack(1, 1), stack(2, 1), stack(3, 1),
            stack(4, 0), stack(5, 0), stack(6, 0), stack(7, 0), stack(8, 0), stack(9, 0), stack(10, 0), stack(11, 0))
```

```python
import functools
import math

import jax
import jax.numpy as jnp
from jax import lax
from jax.experimental import pallas as pl
from jax.experimental.pallas import tpu as pltpu

F32 = jnp.float32
BF16 = jnp.bfloat16

D_MODEL = 1024
N_HEADS = 4
HD = 64
MIX_W = N_HEADS * HD
A_COLS = 1024
B_QKV = 3 * MIX_W
B_COLS_PAD = 1152
C_W = 512
NORM_EPS = 1e-6
A_LN_EPS = 64e-5
REL_BUCKETS = 32
REL_MAX_DIST = 128
PK_HEADS = 8
PK_NKEYS = 128
PK_TOPK = 16
V7X_VMEM_LIMIT = 56 * 1024 * 1024


def _parts(a, n):
    out = []
    r = a
    for i in range(n):
        p = r.astype(BF16)
        out.append(p)
        if i + 1 < n:
            r = r - p.astype(F32)
    return out


_NN = (((1,), (0,)), ((), ()))
_NT = (((1,), (1,)), ((), ()))
_TN = (((0,), (0,)), ((), ()))


def _mm(a, b, dims=_NN, na=2, nb=2, order=1):
    ap = _parts(a, na)
    bp = _parts(b, nb)
    acc = None
    for s in range(order, -1, -1):
        for i in range(na):
            j = s - i
            if 0 <= j < nb:
                t = lax.dot_general(ap[i], bp[j], dims, preferred_element_type=F32)
                acc = t if acc is None else acc + t
    return acc


def _mm_f32(a, b, dims=_NN):
    return lax.dot_general(a, b, dims, precision=lax.Precision.HIGHEST, preferred_element_type=F32)


def _mm_mask_r(a, mask, dims=_NN):
    return _mm(a, mask, dims, na=3, nb=1, order=2)


def _mm_mask_l(mask, b, dims=_NN):
    return _mm(mask, b, dims, na=1, nb=3, order=2)


def _iota(shape, dim):
    return lax.broadcasted_iota(jnp.int32, shape, dim)


def _seg_ones(n, seg_shift):
    r = jnp.right_shift(_iota((n, n), 0), seg_shift)
    c = jnp.right_shift(_iota((n, n), 1), seg_shift)
    return (r == c).astype(F32)


def _tile4(z):
    return jnp.concatenate([z, z, z, z], axis=0)


def _fold4(y, c):
    return y[0:c] + y[c:2 * c] + y[2 * c:3 * c] + y[3 * c:4 * c]


def _neumann_inverse(n, steps):
    rows = n.shape[0]
    eye = (_iota((rows, rows), 0) == _iota((rows, rows), 1)).astype(F32)
    t = eye + n
    p = n
    for _ in range(steps - 1):
        p = _mm(p, p)
        t = t + _mm(t, p)
    return t


def _softplus(y):
    return jnp.maximum(y, 0.0) + jnp.log1p(jnp.exp(-jnp.abs(y)))


def _chunk_masks(c_len):
    r = 4 * c_len
    shift = int(math.log2(c_len))
    ri = _iota((r, r), 0)
    ci = _iota((r, r), 1)
    same = jnp.right_shift(ri, shift) == jnp.right_shift(ci, shift)
    incl = jnp.logical_and(same, ci <= ri)
    strict = jnp.logical_and(same, ci < ri)
    head = jnp.right_shift(_iota((r, MIX_W), 0), shift) == jnp.right_shift(_iota((r, MIX_W), 1), 6)
    return incl, strict, head


def _proj_in_kernel(x_ref, g_ref, w_ref, qg_ref, kg_ref, ha_ref, hb_ref, q_ref, k_ref, v_ref):
    x = x_ref[...]
    xn = x * lax.rsqrt(jnp.mean(x * x, axis=-1, keepdims=True) + NORM_EPS) * g_ref[...]
    h = jnp.dot(xn.astype(BF16), w_ref[...], preferred_element_type=F32)
    o = A_COLS
    ha_ref[...] = h[:, :o]
    hb_ref[...] = h[:, o:o + B_COLS_PAD]
    o += B_COLS_PAD
    seg = _seg_ones(C_W, 6)

    def qk_norm(t, g):
        ms = _mm_mask_r(t * t, seg) * (1.0 / HD)
        return t * lax.rsqrt(ms + NORM_EPS) * g

    q_ref[...] = qk_norm(h[:, o:o + C_W], qg_ref[...])
    k_ref[...] = qk_norm(h[:, o + C_W:o + 2 * C_W], kg_ref[...])
    v_ref[...] = h[:, o + 2 * C_W:o + 3 * C_W]


def _proj_in(x2d, g, w_cat, qg, kg, tm=256):
    t = x2d.shape[0]
    n = w_cat.shape[1]
    row = lambda w: pl.BlockSpec((tm, w), lambda i: (i, 0))
    full = lambda a: pl.BlockSpec(a.shape, lambda i: (0,) * a.ndim)
    return pl.pallas_call(
        _proj_in_kernel,
        grid=(t // tm,),
        in_specs=[row(D_MODEL), full(g), full(w_cat), full(qg), full(kg)],
        out_specs=[row(A_COLS), row(B_COLS_PAD), row(C_W), row(C_W), row(C_W)],
        out_shape=[jax.ShapeDtypeStruct((t, A_COLS), F32), jax.ShapeDtypeStruct((t, B_COLS_PAD), F32),
                   jax.ShapeDtypeStruct((t, C_W), F32), jax.ShapeDtypeStruct((t, C_W), F32),
                   jax.ShapeDtypeStruct((t, C_W), F32)],
        compiler_params=pltpu.CompilerParams(dimension_semantics=("parallel",),
                                             vmem_limit_bytes=V7X_VMEM_LIMIT),
        name="proj_in",
    )(x2d, g, w_cat, qg, kg)


def _rwkv_kernel(h_ref, shift_ref, s0_ref, mu_ref, w0_ref, w2_ref, a0_ref, a2_ref, g2_ref, kk_ref, ka_ref,
                 rk_ref, lng_ref, lnb_ref, o_ref, sout_ref, sbd_ref, buf_ref, *, c_len):
    c = pl.program_id(1)
    r_rows = 4 * c_len

    @pl.when(c == 0)
    def _():
        sbd_ref[...] = jnp.zeros_like(sbd_ref)
        for hh in range(N_HEADS):
            sbd_ref[hh * HD:(hh + 1) * HD, hh * HD:(hh + 1) * HD] = s0_ref[hh]
        buf_ref[0:8, :] = jnp.broadcast_to(shift_ref[...], (8, A_COLS))

    x = h_ref[...]
    buf_ref[8:8 + c_len, :] = x
    prev = buf_ref[7:7 + c_len, :]
    buf_ref[0:8, :] = x[c_len - 8:c_len, :]

    hs = x + (prev - x) * mu_ref[...]
    r = hs[:, 0:MIX_W]
    k = hs[:, MIX_W:2 * MIX_W]
    v = hs[:, 2 * MIX_W:3 * MIX_W]
    xw = hs[:, 768:832]
    xa = hs[:, 832:896]
    xg = hs[:, 896:1024]
    w = -_softplus(-(w0_ref[...] + _mm(jnp.tanh(xw), w2_ref[...]))) - 0.5
    a = jax.nn.sigmoid(a0_ref[...] + _mm(xa, a2_ref[...]))
    g = _mm(jax.nn.sigmoid(xg), g2_ref[...])
    seg = _seg_ones(MIX_W, 6)
    kkv = k * kk_ref[...]
    kkn = kkv * lax.rsqrt(_mm_mask_r(kkv * kkv, seg) + 1e-6)
    k2 = k * (1.0 + (a - 1.0) * ka_ref[...])
    logd = -jnp.exp(w)
    tri = (_iota((c_len, c_len), 1) <= _iota((c_len, c_len), 0)).astype(F32)
    cum = _mm_mask_l(tri, logd)
    p_in = jnp.exp(cum)
    p_inv = jnp.exp(-cum)
    p_prev = jnp.exp(cum - logd)

    incl, strict, head = _chunk_masks(c_len)
    hm = head.astype(F32)
    a_t = _tile4(-kkn * p_prev) * hm
    b_t = _tile4(kkn * a * p_inv) * hm
    k_t = _tile4(k2 * p_inv) * hm
    r_t = _tile4(r * p_in) * hm
    v_t = _tile4(v) * hm

    s0 = sbd_ref[...]
    ar = jnp.concatenate([a_t, r_t], axis=0)
    bk = jnp.concatenate([b_t, k_t], axis=0)
    gram = _mm(ar, bk, _NT)
    h0 = _mm(ar, s0, _NT)
    l_ab = jnp.where(strict, gram[:r_rows, :r_rows], 0.0)
    l_ak = jnp.where(strict, gram[:r_rows, r_rows:], 0.0)
    m_rb = jnp.where(incl, gram[r_rows:, :r_rows], 0.0)
    m_rk = jnp.where(incl, gram[r_rows:, r_rows:], 0.0)
    t_inv = _neumann_inverse(l_ab, int(math.log2(c_len)))
    u = _mm(t_inv, h0[:r_rows] + _mm(l_ak, v_t))
    uv = jnp.concatenate([u, v_t], axis=0)
    y = h0[r_rows:] + _mm(jnp.concatenate([m_rb, m_rk], axis=1), uv)
    sbd_ref[...] = (s0 + _mm(uv, bk, _TN)) * p_in[c_len - 1:c_len, :]

    y = _fold4(y, c_len)
    mean = _mm_mask_r(y, seg) * (1.0 / HD)
    yc = y - mean
    var = _mm_mask_r(yc * yc, seg) * (1.0 / HD)
    yn = yc * lax.rsqrt(var + A_LN_EPS) * lng_ref[...] + lnb_ref[...]
    bonus = _mm_mask_r(r * k2 * rk_ref[...], seg)
    o_ref[...] = (yn + bonus * v) * g

    @pl.when(c == pl.num_programs(1) - 1)
    def _():
        for hh in range(N_HEADS):
            sout_ref[hh] = sbd_ref[hh * HD:(hh + 1) * HD, hh * HD:(hh + 1) * HD]


def _rwkv(h_a, shift_prev, s0, p, c_len):
    bn, seq, _ = h_a.shape
    nc = seq // c_len
    params = [p[n] for n in ("rw_mu", "rw_w0", "rw_w2", "rw_a0", "rw_a2", "rw_g2", "rw_kk", "rw_ka", "rw_rk",
                             "rw_ln_g", "rw_ln_b")]
    full = lambda a: pl.BlockSpec(a.shape, lambda b, c: (0,) * a.ndim)
    return pl.pallas_call(
        functools.partial(_rwkv_kernel, c_len=c_len),
        grid=(bn, nc),
        in_specs=[pl.BlockSpec((None, c_len, A_COLS), lambda b, c: (b, c, 0)),
                  pl.BlockSpec((None, 1, A_COLS), lambda b, c: (b, 0, 0)),
                  pl.BlockSpec((None, N_HEADS, HD, HD), lambda b, c: (b, 0, 0, 0))] + [full(a) for a in params],
        out_specs=[pl.BlockSpec((None, c_len, MIX_W), lambda b, c: (b, c, 0)),
                   pl.BlockSpec((None, N_HEADS, HD, HD), lambda b, c: (b, 0, 0, 0))],
        out_shape=[jax.ShapeDtypeStruct((bn, seq, MIX_W), F32),
                   jax.ShapeDtypeStruct((bn, N_HEADS, HD, HD), F32)],
        scratch_shapes=[pltpu.VMEM((MIX_W, MIX_W), F32), pltpu.VMEM((8 + c_len, A_COLS), F32)],
        compiler_params=pltpu.CompilerParams(dimension_semantics=("parallel", "arbitrary")),
        name="rwkv",
    )(h_a, shift_prev.reshape(bn, 1, A_COLS), s0, *params)


def _gdn_kernel(h_ref, conv0_ref, s0_ref, cw_ref, alog_ref, dtb_ref, ng_ref, o_ref, sout_ref, sbd_ref, buf_ref,
                *, c_len):
    c = pl.program_id(1)
    r_rows = 4 * c_len

    @pl.when(c == 0)
    def _():
        sbd_ref[...] = jnp.zeros_like(sbd_ref)
        for hh in range(N_HEADS):
            sbd_ref[hh * HD:(hh + 1) * HD, hh * HD:(hh + 1) * HD] = s0_ref[hh]
        buf_ref[0:8, :] = conv0_ref[...]

    hb = h_ref[...]
    qkv = hb[:, :B_QKV]
    buf_ref[8:8 + c_len, :] = qkv
    cw = cw_ref[...]
    conv = cw[3:4, :] * qkv
    for j in range(3):
        conv = conv + cw[j:j + 1, :] * buf_ref[5 + j:5 + j + c_len, :]
    buf_ref[0:8, :] = qkv[c_len - 8:c_len, :]
    conv = conv * jax.nn.sigmoid(conv)
    seg = _seg_ones(MIX_W, 6)

    def l2n(t):
        return t * lax.rsqrt(_mm_mask_r(t * t, seg) + 1e-6)

    q = l2n(conv[:, :MIX_W]) * (HD ** -0.5)
    k = l2n(conv[:, MIX_W:2 * MIX_W])
    v = conv[:, 2 * MIX_W:]
    z = hb[:, B_QKV:B_QKV + MIX_W]
    ab = hb[:, B_QKV + MIX_W:]
    gfull = -jnp.exp(alog_ref[...]) * _softplus(ab + dtb_ref[...])
    bfull = jax.nn.sigmoid(ab)

    incl, strict, head = _chunk_masks(c_len)
    hm = head.astype(F32)
    shift = int(math.log2(c_len))
    hrow = jnp.right_shift(_iota((r_rows, 128), 0), shift)
    lane = _iota((r_rows, 128), 1)
    gcol = jnp.sum(jnp.where(lane == hrow, _tile4(gfull), 0.0), axis=1, keepdims=True)
    bcol = jnp.sum(jnp.where(lane == hrow + N_HEADS, _tile4(bfull), 0.0), axis=1, keepdims=True)

    incl_f = incl.astype(F32)
    strict_f = strict.astype(F32)
    dmat = _mm_mask_l(incl_f, jnp.broadcast_to(gcol, (r_rows, r_rows)) * strict_f)
    g128 = jnp.broadcast_to(gcol, (r_rows, 128))
    gc = _mm_mask_l(incl_f, g128)[:, 0:1]
    ri = _iota((r_rows, r_rows), 0)
    ci = _iota((r_rows, r_rows), 1)
    after_f = jnp.logical_and(jnp.right_shift(ri, shift) == jnp.right_shift(ci, shift), ci > ri).astype(F32)
    tail = _mm_mask_l(after_f, g128)[:, 0:1]
    e_gc = jnp.exp(gc)
    e_tail = jnp.exp(tail)
    first = jnp.bitwise_and(_iota((r_rows, MIX_W), 0), c_len - 1) == 0
    g_last = jnp.sum(jnp.where(first, jnp.exp(gc + tail) * hm, 0.0), axis=0, keepdims=True)

    q_h = _tile4(q) * hm
    k_h = _tile4(k) * hm
    v_h = _tile4(v) * hm
    kb = k_h * bcol
    e_d = jnp.exp(dmat)
    m1 = _mm(jnp.concatenate([kb, q_h], axis=0), k_h, _NT)
    a_mat = jnp.where(strict, m1[:r_rows] * e_d, 0.0)
    att = jnp.where(incl, m1[r_rows:] * e_d, 0.0)
    t_inv = _neumann_inverse(-a_mat, int(math.log2(c_len)))
    uw = _mm(t_inv, jnp.concatenate([v_h * bcol, kb * e_gc], axis=1))
    s0 = sbd_ref[...]
    v_new = uw[:, :MIX_W] - _mm(uw[:, MIX_W:], s0)
    o = _mm(q_h * e_gc, s0) + _mm(att, v_new)
    sbd_ref[...] = s0 * g_last + _mm(k_h * e_tail, v_new, _TN)

    o = _fold4(o, c_len)
    ms = _mm_mask_r(o * o, seg) * (1.0 / HD)
    o = o * lax.rsqrt(ms + NORM_EPS) * ng_ref[...]
    o_ref[...] = o * (z * jax.nn.sigmoid(z))

    @pl.when(c == pl.num_programs(1) - 1)
    def _():
        for hh in range(N_HEADS):
            sout_ref[hh] = sbd_ref[hh * HD:(hh + 1) * HD, hh * HD:(hh + 1) * HD]


def _gdn(h_b, conv_prev8, s0, cw, alog, dtb, ng, c_len):
    bn, seq, _ = h_b.shape
    nc = seq // c_len
    params = [cw, alog, dtb, ng]
    full = lambda a: pl.BlockSpec(a.shape, lambda b, c: (0,) * a.ndim)
    return pl.pallas_call(
        functools.partial(_gdn_kernel, c_len=c_len),
        grid=(bn, nc),
        in_specs=[pl.BlockSpec((None, c_len, B_COLS_PAD), lambda b, c: (b, c, 0)),
                  pl.BlockSpec((None, 8, B_QKV), lambda b, c: (b, 0, 0)),
                  pl.BlockSpec((None, N_HEADS, HD, HD), lambda b, c: (b, 0, 0, 0))] + [full(a) for a in params],
        out_specs=[pl.BlockSpec((None, c_len, MIX_W), lambda b, c: (b, c, 0)),
                   pl.BlockSpec((None, N_HEADS, HD, HD), lambda b, c: (b, 0, 0, 0))],
        out_shape=[jax.ShapeDtypeStruct((bn, seq, MIX_W), F32),
                   jax.ShapeDtypeStruct((bn, N_HEADS, HD, HD), F32)],
        scratch_shapes=[pltpu.VMEM((MIX_W, MIX_W), F32), pltpu.VMEM((8 + c_len, B_QKV), F32)],
        compiler_params=pltpu.CompilerParams(dimension_semantics=("parallel", "arbitrary")),
        name="gdn",
    )(h_b, conv_prev8, s0, *params)


def _attn_prompt_kernel(lam_ref, q_ref, k_ref, v_ref, bias_ref, sg_ref, o_ref, q2_ref, m_ref, l_ref, acc_ref,
                        *, tq, scale, out_scale):
    i = pl.program_id(1)
    j = pl.program_id(2)

    @pl.when(j == 0)
    def _():
        m_ref[...] = jnp.full_like(m_ref, -jnp.inf)
        l_ref[...] = jnp.zeros_like(l_ref)
        acc_ref[...] = jnp.zeros_like(acc_ref)
        lane = _iota((tq, 2 * HD), 1)
        for hh in range(N_HEADS):
            qh = q_ref[:, hh * 2 * HD:(hh + 1) * 2 * HD] * scale
            q2_ref[hh, 0:tq, :] = jnp.where(lane < HD, qh, 0.0).astype(BF16)
            q2_ref[hh, tq:2 * tq, :] = jnp.where(lane >= HD, qh, 0.0).astype(BF16)

    @pl.when(j <= i)
    def _():
        for hh in range(N_HEADS):
            sl = slice(hh * 2 * HD, (hh + 1) * 2 * HD)
            s = lax.dot_general(q2_ref[hh], k_ref[:, sl].astype(BF16), _NT, preferred_element_type=F32)
            s = s + bias_ref[hh, jnp.minimum(i - j, 2)]
            m_old = m_ref[hh]
            m_new = jnp.maximum(m_old, jnp.max(s, axis=-1, keepdims=True))
            alpha = jnp.exp(m_old - m_new)
            p = jnp.exp(s - jnp.concatenate([m_new] * (tq // 128), axis=1))
            l_ref[hh] = alpha * l_ref[hh] + jnp.sum(p, axis=-1, keepdims=True)
            acc_ref[hh] = alpha * acc_ref[hh] + jnp.dot(p.astype(BF16), v_ref[:, sl].astype(BF16),
                                                        preferred_element_type=F32)
            m_ref[hh] = m_new

    @pl.when(j == i)
    def _():
        lam = lam_ref[0]
        for hh in range(N_HEADS):
            o = acc_ref[hh] / l_ref[hh]
            od = o[:tq] - lam * o[tq:]
            od = od * lax.rsqrt(jnp.mean(od * od, axis=-1, keepdims=True) + NORM_EPS) * sg_ref[...] * out_scale
            o_ref[:, hh * 2 * HD:(hh + 1) * 2 * HD] = od


def _attn_prompt(q, k, v, bias_near, lam, subln_g, out_scale, tq=256):
    bn, seq, _ = q.shape
    nq = seq // tq
    kv_spec = pl.BlockSpec((None, tq, C_W), lambda b, i, j: (b, jnp.minimum(i, j), 0))
    return pl.pallas_call(
        functools.partial(_attn_prompt_kernel, tq=tq, scale=HD ** -0.5, out_scale=out_scale),
        grid=(bn, nq, nq),
        in_specs=[pl.BlockSpec(memory_space=pltpu.SMEM),
                  pl.BlockSpec((None, tq, C_W), lambda b, i, j: (b, i, 0)),
                  kv_spec, kv_spec,
                  pl.BlockSpec(bias_near.shape, lambda b, i, j: (0, 0, 0, 0)),
                  pl.BlockSpec(subln_g.shape, lambda b, i, j: (0, 0))],
        out_specs=pl.BlockSpec((None, tq, C_W), lambda b, i, j: (b, i, 0)),
        out_shape=jax.ShapeDtypeStruct((bn, seq, C_W), F32),
        scratch_shapes=[pltpu.VMEM((N_HEADS, 2 * tq, 2 * HD), BF16), pltpu.VMEM((N_HEADS, 2 * tq, 2 * HD), F32),
                        pltpu.VMEM((N_HEADS, 2 * tq, 2 * HD), F32), pltpu.VMEM((N_HEADS, 2 * tq, 2 * HD), F32)],
        compiler_params=pltpu.CompilerParams(dimension_semantics=("parallel", "parallel", "arbitrary"),
                                             vmem_limit_bytes=V7X_VMEM_LIMIT),
        name="attn_prompt",
    )(lam, q, k, v, bias_near, subln_g)


def _attn_sample_kernel(pt_ref, lam_ref, q_ref, kn_ref, vn_ref, *rest, lq, scale, out_scale, ppb):
    del pt_ref
    kp_refs = rest[:ppb]
    vp_refs = rest[ppb:2 * ppb]
    bp_ref, bn_ref, sg_ref, o_ref, qz_ref, m_ref, l_ref, acc_ref = rest[2 * ppb:]
    p_idx = pl.program_id(1)

    @pl.when(p_idx == 0)
    def _():
        lane = _iota((lq, 2 * HD), 1)
        pieces = []
        for hh in range(N_HEADS):
            qh = q_ref[:, hh * 2 * HD:(hh + 1) * 2 * HD]
            pieces += [jnp.where(lane < HD, qh, 0.0), jnp.where(lane >= HD, qh, 0.0)]
        qz = jnp.concatenate(pieces, axis=0)
        qz_ref[...] = qz
        s = _mm_f32(qz, kn_ref[...], _NT) * scale + bn_ref[...]
        m = jnp.max(s, axis=-1, keepdims=True)
        p = jnp.exp(s - m)
        m_ref[...] = m
        l_ref[...] = jnp.sum(p, axis=-1, keepdims=True)
        acc_ref[...] = _mm_f32(p, vn_ref[...])

    qz = qz_ref[...].astype(BF16)
    cols = bp_ref.shape[1] // ppb
    scores = []
    for u in range(ppb):
        s = lax.dot_general(qz, kp_refs[u][...].astype(BF16), _NT, preferred_element_type=F32)
        scores.append(s * scale + bp_ref[:, u * cols:(u + 1) * cols])
    m_old = m_ref[...]
    m_new = m_old
    for s in scores:
        m_new = jnp.maximum(m_new, jnp.max(s, axis=-1, keepdims=True))
    alpha = jnp.exp(m_old - m_new)
    l_new = alpha * l_ref[...]
    acc = alpha * acc_ref[...]
    for u in range(ppb):
        p = jnp.exp(scores[u] - m_new)
        l_new = l_new + jnp.sum(p, axis=-1, keepdims=True)
        acc = acc + jnp.dot(p.astype(BF16), vp_refs[u][...].astype(BF16), preferred_element_type=F32)
    l_ref[...] = l_new
    acc_ref[...] = acc
    m_ref[...] = m_new

    @pl.when(p_idx == pl.num_programs(1) - 1)
    def _():
        lam = lam_ref[0]
        o = acc_ref[...] / l_ref[...]
        for hh in range(N_HEADS):
            o1 = o[(2 * hh) * lq:(2 * hh + 1) * lq]
            o2 = o[(2 * hh + 1) * lq:(2 * hh + 2) * lq]
            od = o1 - lam * o2
            od = od * lax.rsqrt(jnp.mean(od * od, axis=-1, keepdims=True) + NORM_EPS) * sg_ref[...] * out_scale
            o_ref[:, hh * 2 * HD:(hh + 1) * 2 * HD] = od


def _attn_sample(q, k_new, v_new, cache_k, cache_v, page_table, layer, bias_past, bias_new, lam, subln_g,
                 out_scale, ppb=4):
    bn, lq, _ = q.shape
    n_pages = page_table.shape[1]
    prow = cache_k.shape[2]
    rows = 2 * N_HEADS * lq
    seq_spec = pl.BlockSpec((None, lq, C_W), lambda b, p, pt: (b, 0, 0))
    new_spec = pl.BlockSpec((None, lq * N_HEADS, 2 * HD), lambda b, p, pt: (b, 0, 0))

    def page_spec(u):
        return pl.BlockSpec((None, None, prow, 2 * HD), lambda b, p, pt: (pt[b, p * ppb + u], layer, 0, 0))

    grid_spec = pltpu.PrefetchScalarGridSpec(
        num_scalar_prefetch=1,
        grid=(bn, n_pages // ppb),
        in_specs=[pl.BlockSpec(memory_space=pltpu.SMEM), seq_spec, new_spec, new_spec]
                 + [page_spec(u) for u in range(ppb)] * 2
                 + [pl.BlockSpec((rows, ppb * prow), lambda b, p, pt: (0, p)),
                    pl.BlockSpec(bias_new.shape, lambda b, p, pt: (0, 0)),
                    pl.BlockSpec(subln_g.shape, lambda b, p, pt: (0, 0))],
        out_specs=seq_spec,
        scratch_shapes=[pltpu.VMEM((rows, 2 * HD), F32), pltpu.VMEM((rows, 1), F32), pltpu.VMEM((rows, 1), F32),
                        pltpu.VMEM((rows, 2 * HD), F32)],
    )
    return pl.pallas_call(
        functools.partial(_attn_sample_kernel, lq=lq, scale=HD ** -0.5, out_scale=out_scale, ppb=ppb),
        grid_spec=grid_spec,
        out_shape=jax.ShapeDtypeStruct((bn, lq, C_W), F32),
        compiler_params=pltpu.CompilerParams(dimension_semantics=("parallel", "arbitrary")),
        name="attn_sample",
    )(page_table, lam, q, k_new, v_new, *([cache_k] * ppb), *([cache_v] * ppb), bias_past, bias_new, subln_g)


def _proj_out_kernel(x_ref, oa_ref, ob_ref, oc_ref, wo_ref, g_ref, wq_hi_ref, wq_lo_ref, xo_ref, xnt_ref, q_ref):
    dot = lambda a, w: jnp.dot(a.astype(BF16), w, preferred_element_type=F32)
    x = x_ref[...]
    x = x + dot(oa_ref[...], wo_ref[0:MIX_W, :]) + dot(ob_ref[...], wo_ref[MIX_W:2 * MIX_W, :]) \
        + dot(oc_ref[...], wo_ref[2 * MIX_W:, :])
    xo_ref[...] = x
    xn = x * lax.rsqrt(jnp.mean(x * x, axis=-1, keepdims=True) + NORM_EPS) * g_ref[...]
    xnt_ref[...] = jnp.transpose(xn).astype(BF16)
    hi, lo = _parts(xn, 2)
    q_ref[...] = (jnp.dot(lo, wq_hi_ref[...], preferred_element_type=F32)
                  + jnp.dot(hi, wq_lo_ref[...], preferred_element_type=F32)
                  + jnp.dot(hi, wq_hi_ref[...], preferred_element_type=F32))


def _proj_out(x2d, o_a, o_b, o_c, w_out, g, wq_hi, wq_lo, tm=256):
    t = x2d.shape[0]
    nq = wq_hi.shape[1]
    row = lambda w: pl.BlockSpec((tm, w), lambda i: (i, 0))
    full = lambda a: pl.BlockSpec(a.shape, lambda i: (0,) * a.ndim)
    return pl.pallas_call(
        _proj_out_kernel,
        grid=(t // tm,),
        in_specs=[row(D_MODEL), row(MIX_W), row(MIX_W), row(C_W), full(w_out), full(g), full(wq_hi), full(wq_lo)],
        out_specs=[row(D_MODEL), pl.BlockSpec((D_MODEL, tm), lambda i: (0, i)), row(nq)],
        out_shape=[jax.ShapeDtypeStruct((t, D_MODEL), F32), jax.ShapeDtypeStruct((D_MODEL, t), BF16),
                   jax.ShapeDtypeStruct((t, nq), F32)],
        compiler_params=pltpu.CompilerParams(dimension_semantics=("parallel",),
                                             vmem_limit_bytes=V7X_VMEM_LIMIT),
        name="proj_out",
    )(x2d, o_a, o_b, o_c, w_out, g, wq_hi, wq_lo)


def _top16(s, n_rows):
    tm = s.shape[1]
    rid = _iota((n_rows, tm), 0).astype(F32)
    kid = _iota((PK_TOPK, tm), 0)
    vals = jnp.zeros((PK_TOPK, tm), F32)
    idxs = jnp.zeros((PK_TOPK, tm), F32)
    for kk in range(PK_TOPK):
        m = jnp.max(s, axis=0, keepdims=True)
        ix = jnp.min(jnp.where(s == m, rid, float(n_rows)), axis=0, keepdims=True)
        vals = jnp.where(kid == kk, m, vals)
        idxs = jnp.where(kid == kk, ix, idxs)
        s = jnp.where(rid == ix, -jnp.inf, s)
    return vals, idxs


def _peer_route_kernel(q_ref, keys_ref, e1_ref, f_ref, e2_ref, r_ref):
    tm = q_ref.shape[0]
    q = q_ref[...]
    s1 = _mm(keys_ref[0], q[:, :PK_NKEYS], _NT)
    s2 = _mm(keys_ref[1], q[:, PK_NKEYS:], _NT)
    sv1, si1 = _top16(s1, PK_NKEYS)
    sv2, si2 = _top16(s2, PK_NKEYS)
    cand = jnp.concatenate([sv1[a:a + 1, :] + sv2 for a in range(PK_TOPK)], axis=0)
    tv, ti = _top16(cand, PK_TOPK * PK_TOPK)
    rank_a = jnp.floor(ti * (1.0 / PK_TOPK))
    z = jnp.sum(jnp.exp(tv - tv[0:1, :]), axis=0, keepdims=True)
    aid = _iota((PK_TOPK, tm), 0).astype(F32)
    cnt = jnp.zeros((PK_TOPK, tm), F32)
    for kk in range(PK_TOPK):
        cnt = cnt + (aid == rank_a[kk:kk + 1, :]).astype(F32)
    ex1 = jnp.exp(sv1 - sv1[0:1, :]) / z
    ex2 = jnp.exp(sv2 - sv2[0:1, :])
    kid = _iota((PK_NKEYS, tm), 0).astype(F32)
    e1 = jnp.zeros((PK_NKEYS, tm), F32)
    f = jnp.zeros((PK_NKEYS, tm), F32)
    e2 = jnp.zeros((PK_NKEYS, tm), F32)
    r = jnp.full((PK_NKEYS, tm), float(PK_TOPK), F32)
    for kk in range(PK_TOPK):
        hit1 = kid == si1[kk:kk + 1, :]
        e1 = jnp.where(hit1, ex1[kk:kk + 1, :], e1)
        f = jnp.where(hit1, cnt[kk:kk + 1, :], f)
        hit2 = kid == si2[kk:kk + 1, :]
        e2 = jnp.where(hit2, ex2[kk:kk + 1, :], e2)
        r = jnp.where(hit2, float(kk), r)
    e1_ref[...] = e1
    f_ref[...] = f
    e2_ref[...] = e2.astype(BF16)
    r_ref[...] = r.astype(BF16)


def _peer_route(q, keys, tm=256):
    t = q.shape[0]
    tab = jax.ShapeDtypeStruct((PK_HEADS, PK_NKEYS, t), F32)
    tab16 = jax.ShapeDtypeStruct((PK_HEADS, PK_NKEYS, t), BF16)
    tab_spec = pl.BlockSpec((None, PK_NKEYS, tm), lambda i, h: (h, 0, i))
    return pl.pallas_call(
        _peer_route_kernel,
        grid=(t // tm, PK_HEADS),
        in_specs=[pl.BlockSpec((tm, 2 * PK_NKEYS), lambda i, h: (i, h)),
                  pl.BlockSpec((None, 2, PK_NKEYS, PK_NKEYS), lambda i, h: (h, 0, 0, 0))],
        out_specs=[tab_spec] * 4,
        out_shape=[tab, tab, tab16, tab16],
        compiler_params=pltpu.CompilerParams(dimension_semantics=("parallel", "parallel")),
        name="peer_route",
    )(q, keys)


def _peer_dense_kernel(x_ref, xnt_ref, u_ref, vt_ref, e1_ref, f_ref, e2_ref, r_ref, o_ref, acc_ref, act_ref, c_ref,
                       *, ib, tm):
    ii = pl.program_id(1)

    @pl.when(ii == 0)
    def _():
        acc_ref[...] = jnp.zeros_like(acc_ref)

    act_ref[...] = jnp.dot(u_ref[...], xnt_ref[...], preferred_element_type=F32)
    zero = jnp.zeros((PK_NKEYS, 128), BF16)

    for s in range(ib):
        rows = slice(s * PK_NKEYS, (s + 1) * PK_NKEYS)
        for c in range(tm // 128):
            lanes = slice(c * 128, (c + 1) * 128)
            gate = None
            for hh in range(PK_HEADS):
                e1 = e1_ref[hh, s:s + 1, lanes].astype(BF16)
                f = f_ref[hh, s:s + 1, lanes].astype(BF16)
                term = e1 * jnp.where(r_ref[hh, :, lanes] < f, e2_ref[hh, :, lanes], zero)
                gate = term if gate is None else gate + term
            a = act_ref[rows, lanes]
            a = 0.5 * a * (1.0 + lax.erf(a * math.sqrt(0.5)))
            c_ref[rows, lanes] = (gate.astype(F32) * a).astype(BF16)
    acc_ref[...] += jnp.dot(vt_ref[...], c_ref[...], preferred_element_type=F32)

    @pl.when(ii == pl.num_programs(1) - 1)
    def _():
        o_ref[...] = x_ref[...] + jnp.transpose(acc_ref[...])


def _peer_dense(x2d, xnt, u_bf, vt_bf, e1, f, e2, r, tm=512, ib=8):
    t = x2d.shape[0]
    ni = PK_NKEYS // ib
    tab_i = pl.BlockSpec((PK_HEADS, ib, tm), lambda tt, ii: (0, ii, tt))
    tab_j = pl.BlockSpec((PK_HEADS, PK_NKEYS, tm), lambda tt, ii: (0, 0, tt))
    return pl.pallas_call(
        functools.partial(_peer_dense_kernel, ib=ib, tm=tm),
        grid=(t // tm, ni),
        in_specs=[pl.BlockSpec((tm, D_MODEL), lambda tt, ii: (tt, 0)),
                  pl.BlockSpec((D_MODEL, tm), lambda tt, ii: (0, tt)),
                  pl.BlockSpec((ib * PK_NKEYS, D_MODEL), lambda tt, ii: (ii, 0)),
                  pl.BlockSpec((D_MODEL, ib * PK_NKEYS), lambda tt, ii: (0, ii)),
                  tab_i, tab_i, tab_j, tab_j],
        out_specs=pl.BlockSpec((tm, D_MODEL), lambda tt, ii: (tt, 0)),
        out_shape=jax.ShapeDtypeStruct((t, D_MODEL), F32),
        scratch_shapes=[pltpu.VMEM((D_MODEL, tm), F32), pltpu.VMEM((ib * PK_NKEYS, tm), F32),
                        pltpu.VMEM((ib * PK_NKEYS, tm), BF16)],
        compiler_params=pltpu.CompilerParams(dimension_semantics=("parallel", "arbitrary"),
                                             vmem_limit_bytes=V7X_VMEM_LIMIT),
        name="peer_dense",
    )(x2d, xnt, u_bf, vt_bf, e1, f, e2, r)


def _t5_bucket(rel):
    n = jnp.maximum(-rel, 0)
    max_exact = REL_BUCKETS // 2
    nf = jnp.maximum(n, 1).astype(F32)
    large = max_exact + (jnp.log(nf / max_exact) / math.log(REL_MAX_DIST / max_exact)
                         * (REL_BUCKETS - max_exact)).astype(jnp.int32)
    large = jnp.minimum(large, REL_BUCKETS - 1)
    return jnp.where(n < max_exact, n, large)


def _bias_table(rel_bias, rel):
    bucket = _t5_bucket(rel)[None]
    col = lambda b: rel_bias[b].astype(F32).reshape((-1,) + (1,) * rel.ndim)
    out = jnp.broadcast_to(col(0), (rel_bias.shape[1],) + rel.shape)
    for b in range(1, REL_BUCKETS):
        out = jnp.where(bucket == b, col(b), out)
    return out


def _prompt_bias_tiles(rel_bias, tq):
    rel = jnp.arange(tq)[None, :] - jnp.arange(tq)[:, None]
    tiles = []
    for d in range(3):
        b = _bias_table(rel_bias, rel - d * tq)
        if d == 0:
            b = jnp.where(rel <= 0, b, -jnp.inf)
        tiles.append(jnp.concatenate([b, b], axis=1))
    return jnp.stack(tiles, axis=1)


def _prep_layer(l, W):
    row = lambda a: a.reshape(1, -1).astype(F32)
    w_in = W["w_in"][l]
    o_b = A_COLS
    o_c = A_COLS + 4 * MIX_W + 2 * N_HEADS
    w_cat = jnp.concatenate([
        w_in[:, :A_COLS],
        w_in[:, o_b:o_b + B_QKV],
        w_in[:, o_b + B_QKV + 2 * N_HEADS:o_c],
        w_in[:, o_b + B_QKV:o_b + B_QKV + 2 * N_HEADS],
        jnp.zeros((D_MODEL, 128 - 2 * N_HEADS), F32),
        w_in[:, o_c:],
    ], axis=1).astype(BF16)
    pad_lane = lambda a, off: jnp.zeros((1, 128), F32).at[0, off:off + a.shape[0]].set(a)
    alog = pad_lane(W["dn_a_log"][l], 0)
    dtb = pad_lane(W["dn_dt_bias"][l], 0)
    f32 = F32
    lam_init = 0.8 - 0.6 * math.exp(-0.3 * l)
    lam = (jnp.exp(jnp.sum(W["df_lq1"][l].astype(f32) * W["df_lk1"][l].astype(f32)))
           - jnp.exp(jnp.sum(W["df_lq2"][l].astype(f32) * W["df_lk2"][l].astype(f32))) + lam_init)
    wq = W["pk_wq"][l]
    wq_hi = wq.astype(BF16)
    wq_lo = (wq - wq_hi.astype(F32)).astype(BF16)
    return dict(
        rms_mix_g=row(W["rms_mix_g"][l]), w_cat=w_cat,
        qg=row(jnp.tile(W["df_qn_g"][l].reshape(-1), N_HEADS)),
        kg=row(jnp.tile(W["df_kn_g"][l].reshape(-1), N_HEADS)),
        rw_mu=row(W["rw_mu"][l]), rw_w0=row(W["rw_w0"][l]), rw_w2=W["rw_w2"][l], rw_a0=row(W["rw_a0"][l]),
        rw_a2=W["rw_a2"][l], rw_g2=W["rw_g2"][l], rw_kk=row(W["rw_kk"][l]), rw_ka=row(W["rw_ka"][l]),
        rw_rk=row(W["rw_rk"][l]), rw_ln_g=row(W["rw_ln_g"][l]), rw_ln_b=row(W["rw_ln_b"][l]),
        dn_conv_w=W["dn_conv_w"][l], dn_alog=alog, dn_dtb=dtb,
        dn_norm_g=row(jnp.tile(W["dn_norm_g"][l], N_HEADS)),
        lam=lam.reshape(1).astype(F32), lam_init=lam_init, subln_g=row(W["df_subln_g"][l]),
        w_out=W["w_out"][l].astype(BF16), rms_ffn_g=row(W["rms_ffn_g"][l]), wq_hi=wq_hi, wq_lo=wq_lo,
        pk_keys=W["pk_keys"][l], pk_u=W["pk_u"][l].astype(BF16), pk_vt=jnp.transpose(W["pk_v"][l]).astype(BF16),
    )


def _layer(x, shift_prev, s_rwkv, conv_prev, s_dn, attn_fn, P, c_len):
    bn, seq, _ = x.shape
    t = bn * seq
    x2d = x.reshape(t, D_MODEL)
    h_a, h_b, q, k, v = _proj_in(x2d, P["rms_mix_g"], P["w_cat"], P["qg"], P["kg"])
    h_a3 = h_a.reshape(bn, seq, A_COLS)
    h_b3 = h_b.reshape(bn, seq, B_COLS_PAD)
    o_a, s_rwkv_new = _rwkv(h_a3, shift_prev, s_rwkv, P, c_len)
    conv8 = jnp.concatenate([jnp.zeros((bn, 5, B_QKV), F32), conv_prev], axis=1)
    o_b, s_dn_new = _gdn(h_b3, conv8, s_dn, P["dn_conv_w"], P["dn_alog"], P["dn_dtb"], P["dn_norm_g"], c_len)
    q3, k3, v3 = (a.reshape(bn, seq, C_W) for a in (q, k, v))
    o_c = attn_fn(q3, k3, v3)
    x_new, xnt, pq = _proj_out(x2d, o_a.reshape(t, MIX_W), o_b.reshape(t, MIX_W), o_c.reshape(t, C_W),
                               P["w_out"], P["rms_ffn_g"], P["wq_hi"], P["wq_lo"])
    e1, f, e2, r = _peer_route(pq, P["pk_keys"])
    y = _peer_dense(x_new, xnt, P["pk_u"], P["pk_vt"], e1, f, e2, r)
    k_rows = k.reshape(bn, seq, N_HEADS, 2 * HD)
    v_rows = v.reshape(bn, seq, N_HEADS, 2 * HD)
    shift_new = h_a3[:, -1]
    conv_new = h_b3[:, -3:, :B_QKV]
    return y.reshape(bn, seq, D_MODEL), k_rows, v_rows, s_rwkv_new, shift_new, s_dn_new, conv_new


def kernel(x_prompt, x_sample, cache_k, cache_v, state_rwkv, state_rwkv_shift, state_dn, state_dn_conv, page_table, rms_mix_g, w_in, w_out, rw_mu, rw_w0, rw_w2, rw_a0, rw_a2, rw_g2, rw_kk, rw_ka, rw_rk, rw_ln_g, rw_ln_b, dn_conv_w, dn_a_log, dn_dt_bias, dn_norm_g, df_qn_g, df_kn_g, df_lq1, df_lk1, df_lq2, df_lk2, df_subln_g, rel_bias, rms_ffn_g, pk_wq, pk_keys, pk_u, pk_v):
    W = dict(rms_mix_g=rms_mix_g, w_in=w_in, w_out=w_out, rw_mu=rw_mu, rw_w0=rw_w0, rw_w2=rw_w2, rw_a0=rw_a0,
             rw_a2=rw_a2, rw_g2=rw_g2, rw_kk=rw_kk, rw_ka=rw_ka, rw_rk=rw_rk, rw_ln_g=rw_ln_g, rw_ln_b=rw_ln_b,
             dn_conv_w=dn_conv_w, dn_a_log=dn_a_log, dn_dt_bias=dn_dt_bias, dn_norm_g=dn_norm_g,
             df_qn_g=df_qn_g, df_kn_g=df_kn_g, df_lq1=df_lq1, df_lk1=df_lk1, df_lq2=df_lq2, df_lk2=df_lk2,
             df_subln_g=df_subln_g, rms_ffn_g=rms_ffn_g, pk_wq=pk_wq, pk_keys=pk_keys, pk_u=pk_u, pk_v=pk_v)
    depth = w_in.shape[0]
    bp, seq_p, _ = x_prompt.shape
    bs, seq_s, _ = x_sample.shape
    n_pages = page_table.shape[1]
    page = cache_k.shape[2]
    past = n_pages * page
    tq = 256
    ck = cache_k.reshape(cache_k.shape[0], depth, page * N_HEADS, 2 * HD)
    cv = cache_v.reshape(cache_v.shape[0], depth, page * N_HEADS, 2 * HD)

    bias_near = _prompt_bias_tiles(rel_bias, tq)
    q_pos = past + jnp.arange(seq_s)
    same_head = jnp.eye(N_HEADS, dtype=bool)[:, None, None, :]

    def expand(b):
        bx = jnp.where(same_head, b[..., None], -jnp.inf)
        return jnp.broadcast_to(bx[:, None], (N_HEADS, 2) + bx.shape[1:]).reshape(2 * N_HEADS * seq_s, -1)

    bias_past = expand(_bias_table(rel_bias, jnp.arange(past)[None, :] - q_pos[:, None]))
    rel_new = q_pos[None, :] - q_pos[:, None]
    bias_new = expand(jnp.where(rel_new <= 0, _bias_table(rel_bias, rel_new), -jnp.inf))

    xp, xs = x_prompt, x_sample
    outs = [[] for _ in range(12)]
    zeros = lambda *s: jnp.zeros(s, F32)
    for l in range(depth):
        P = _prep_layer(l, W)
        out_scale = 1.0 - P["lam_init"]
        attn_p = lambda q, k, v: _attn_prompt(q, k, v, bias_near, P["lam"], P["subln_g"], out_scale, tq)
        by_head = lambda a: a.reshape(bs, seq_s * N_HEADS, 2 * HD)
        attn_s = lambda q, k, v: _attn_sample(q, by_head(k), by_head(v), ck, cv, page_table, l, bias_past, bias_new,
                                              P["lam"], P["subln_g"], out_scale)
        xp, kp, vp, sap, shp, sbp, cvp = _layer(
            xp, zeros(bp, A_COLS), zeros(bp, N_HEADS, HD, HD), zeros(bp, 3, B_QKV), zeros(bp, N_HEADS, HD, HD),
            attn_p, P, 64)
        xs, ksn, vsn, sas, shs, sbs, cvs = _layer(
            xs, state_rwkv_shift[l], state_rwkv[l], state_dn_conv[l], state_dn[l], attn_s, P, seq_s)
        for lst, val in zip(outs, (kp, vp, ksn, vsn, sap, sas, shp, shs, sbp, sbs, cvp, cvs)):
            lst.append(val)
    stack = lambda i, ax: jnp.stack(outs[i], axis=ax)
    return (xp, xs, stack(0, 1), stack(1, 1), stack(2, 1), stack(3, 1),
            stack(4, 0), stack(5, 0), stack(6, 0), stack(7, 0), stack(8, 0), stack(9, 0), stack(10, 0), stack(11, 0))
```

```python
import functools
import math

import jax
import jax.numpy as jnp
from jax import lax
from jax.experimental import pallas as pl
from jax.experimental.pallas import tpu as pltpu

F32 = jnp.float32
BF16 = jnp.bfloat16

D_MODEL = 1024
N_HEADS = 4
HD = 64
MIX_W = N_HEADS * HD
A_COLS = 1024
B_QKV = 3 * MIX_W
B_COLS_PAD = 1152
C_W = 512
NORM_EPS = 1e-6
A_LN_EPS = 64e-5
REL_BUCKETS = 32
REL_MAX_DIST = 128
PK_HEADS = 8
PK_NKEYS = 128
PK_TOPK = 16
V7X_VMEM_LIMIT = 56 * 1024 * 1024


def _parts(a, n):
    out = []
    r = a
    for i in range(n):
        p = r.astype(BF16)
        out.append(p)
        if i + 1 < n:
            r = r - p.astype(F32)
    return out


_NN = (((1,), (0,)), ((), ()))
_NT = (((1,), (1,)), ((), ()))
_TN = (((0,), (0,)), ((), ()))


def _mm(a, b, dims=_NN, na=2, nb=2, order=1):
    ap = _parts(a, na)
    bp = _parts(b, nb)
    acc = None
    for s in range(order, -1, -1):
        for i in range(na):
            j = s - i
            if 0 <= j < nb:
                t = lax.dot_general(ap[i], bp[j], dims, preferred_element_type=F32)
                acc = t if acc is None else acc + t
    return acc


def _mm1(a, b, dims=_NN):
    return _mm(a, b, dims, na=1, nb=1, order=0)


def _mm_f32(a, b, dims=_NN):
    return lax.dot_general(a, b, dims, precision=lax.Precision.HIGHEST, preferred_element_type=F32)


def _mm_mask_r(a, mask, dims=_NN):
    return _mm(a, mask, dims, na=3, nb=1, order=2)


def _mm_mask_l(mask, b, dims=_NN):
    return _mm(mask, b, dims, na=1, nb=3, order=2)


def _iota(shape, dim):
    return lax.broadcasted_iota(jnp.int32, shape, dim)


def _seg_ones(n, seg_shift):
    r = jnp.right_shift(_iota((n, n), 0), seg_shift)
    c = jnp.right_shift(_iota((n, n), 1), seg_shift)
    return (r == c).astype(F32)


def _tile4(z):
    return jnp.concatenate([z, z, z, z], axis=0)


def _fold4(y, c):
    return y[0:c] + y[c:2 * c] + y[2 * c:3 * c] + y[3 * c:4 * c]


def _neumann_inverse(n, steps):
    rows = n.shape[0]
    eye = (_iota((rows, rows), 0) == _iota((rows, rows), 1)).astype(F32)
    t = eye + n
    p = n
    for _ in range(steps - 1):
        p = _mm1(p, p)
        t = t + _mm1(t, p)
    return t


def _softplus(y):
    return jnp.maximum(y, 0.0) + jnp.log1p(jnp.exp(-jnp.abs(y)))


def _chunk_masks(c_len):
    r = 4 * c_len
    shift = int(math.log2(c_len))
    ri = _iota((r, r), 0)
    ci = _iota((r, r), 1)
    same = jnp.right_shift(ri, shift) == jnp.right_shift(ci, shift)
    incl = jnp.logical_and(same, ci <= ri)
    strict = jnp.logical_and(same, ci < ri)
    head = jnp.right_shift(_iota((r, MIX_W), 0), shift) == jnp.right_shift(_iota((r, MIX_W), 1), 6)
    return incl, strict, head


def _proj_in_kernel(x_ref, g_ref, w_ref, qg_ref, kg_ref, ha_ref, hb_ref, q_ref, k_ref, v_ref):
    x = x_ref[...]
    xn = x * lax.rsqrt(jnp.mean(x * x, axis=-1, keepdims=True) + NORM_EPS) * g_ref[...]
    h = jnp.dot(xn.astype(BF16), w_ref[...], preferred_element_type=F32)
    o = A_COLS
    ha_ref[...] = h[:, :o]
    hb_ref[...] = h[:, o:o + B_COLS_PAD]
    o += B_COLS_PAD
    seg = _seg_ones(C_W, 6)

    def qk_norm(t, g):
        ms = _mm_mask_r(t * t, seg) * (1.0 / HD)
        return t * lax.rsqrt(ms + NORM_EPS) * g

    q_ref[...] = qk_norm(h[:, o:o + C_W], qg_ref[...])
    k_ref[...] = qk_norm(h[:, o + C_W:o + 2 * C_W], kg_ref[...])
    v_ref[...] = h[:, o + 2 * C_W:o + 3 * C_W]


def _proj_in(x2d, g, w_cat, qg, kg, tm=256):
    t = x2d.shape[0]
    n = w_cat.shape[1]
    row = lambda w: pl.BlockSpec((tm, w), lambda i: (i, 0))
    full = lambda a: pl.BlockSpec(a.shape, lambda i: (0,) * a.ndim)
    return pl.pallas_call(
        _proj_in_kernel,
        grid=(t // tm,),
        in_specs=[row(D_MODEL), full(g), full(w_cat), full(qg), full(kg)],
        out_specs=[row(A_COLS), row(B_COLS_PAD), row(C_W), row(C_W), row(C_W)],
        out_shape=[jax.ShapeDtypeStruct((t, A_COLS), F32), jax.ShapeDtypeStruct((t, B_COLS_PAD), F32),
                   jax.ShapeDtypeStruct((t, C_W), F32), jax.ShapeDtypeStruct((t, C_W), F32),
                   jax.ShapeDtypeStruct((t, C_W), F32)],
        compiler_params=pltpu.CompilerParams(dimension_semantics=("parallel",),
                                             vmem_limit_bytes=V7X_VMEM_LIMIT),
        name="proj_in",
    )(x2d, g, w_cat, qg, kg)


def _three_phase(seq_fn, refs, n_shared_from, n_shared_to, bb):
    c = pl.program_id(1)

    def per_seq(bi):
        return [r if n_shared_from <= n < n_shared_to else r.at[bi] for n, r in enumerate(refs)]

    @pl.when(c == 0)
    def _():
        for bi in range(bb):
            seq_fn("init", *per_seq(bi))

    for bi in range(bb):
        seq_fn("chunk", *per_seq(bi))

    @pl.when(c == pl.num_programs(1) - 1)
    def _():
        for bi in range(bb):
            seq_fn("final", *per_seq(bi))


def _rwkv_seq(phase, h_ref, shift_ref, s0_ref, mu_ref, w0_ref, w2_ref, a0_ref, a2_ref, g2_ref, kk_ref, ka_ref,
              rk_ref, lng_ref, lnb_ref, o_ref, sout_ref, sbd_ref, buf_ref, *, c_len):
    r_rows = 4 * c_len

    if phase == "init":
        sbd_ref[...] = jnp.zeros_like(sbd_ref)
        for hh in range(N_HEADS):
            sbd_ref[hh * HD:(hh + 1) * HD, hh * HD:(hh + 1) * HD] = s0_ref[hh]
        buf_ref[0:8, :] = jnp.broadcast_to(shift_ref[...], (8, A_COLS))
        return
    if phase == "final":
        for hh in range(N_HEADS):
            sout_ref[hh] = sbd_ref[hh * HD:(hh + 1) * HD, hh * HD:(hh + 1) * HD]
        return

    x = h_ref[...]
    buf_ref[8:8 + c_len, :] = x
    prev = buf_ref[7:7 + c_len, :]
    buf_ref[0:8, :] = x[c_len - 8:c_len, :]

    hs = x + (prev - x) * mu_ref[...]
    r = hs[:, 0:MIX_W]
    k = hs[:, MIX_W:2 * MIX_W]
    v = hs[:, 2 * MIX_W:3 * MIX_W]
    xw = hs[:, 768:832]
    xa = hs[:, 832:896]
    xg = hs[:, 896:1024]
    w = -_softplus(-(w0_ref[...] + _mm(jnp.tanh(xw), w2_ref[...]))) - 0.5
    a = jax.nn.sigmoid(a0_ref[...] + _mm(xa, a2_ref[...]))
    g = _mm(jax.nn.sigmoid(xg), g2_ref[...])
    seg = _seg_ones(MIX_W, 6)
    kkv = k * kk_ref[...]
    kkn = kkv * lax.rsqrt(_mm_mask_r(kkv * kkv, seg) + 1e-6)
    k2 = k * (1.0 + (a - 1.0) * ka_ref[...])
    logd = -jnp.exp(w)
    tri = (_iota((c_len, c_len), 1) <= _iota((c_len, c_len), 0)).astype(F32)
    cum = _mm_mask_l(tri, logd)
    p_in = jnp.exp(cum)
    p_inv = jnp.exp(-cum)
    p_prev = jnp.exp(cum - logd)

    incl, strict, head = _chunk_masks(c_len)
    hm = head.astype(F32)
    a_t = _tile4(-kkn * p_prev) * hm
    b_t = _tile4(kkn * a * p_inv) * hm
    k_t = _tile4(k2 * p_inv) * hm
    r_t = _tile4(r * p_in) * hm
    v_t = _tile4(v) * hm

    s0 = sbd_ref[...]
    ar = jnp.concatenate([a_t, r_t], axis=0)
    bk = jnp.concatenate([b_t, k_t], axis=0)
    g_b = _mm1(ar, b_t, _NT)
    l_ab = jnp.where(strict, g_b[:r_rows], 0.0)
    l_ak = jnp.where(strict, _mm(a_t, k_t, _NT), 0.0)
    m_rb = jnp.where(incl, g_b[r_rows:], 0.0)
    m_rk = jnp.where(incl, _mm1(r_t, k_t, _NT), 0.0)
    t_inv = _neumann_inverse(l_ab, int(math.log2(c_len)))
    u = _mm(t_inv, _mm(a_t, s0, _NT) + _mm(l_ak, v_t))
    uv = jnp.concatenate([u, v_t], axis=0)
    y = _mm1(r_t, s0, _NT) + _mm1(jnp.concatenate([m_rb, m_rk], axis=1), uv)
    sbd_ref[...] = (s0 + _mm(uv, bk, _TN)) * p_in[c_len - 1:c_len, :]

    y = _fold4(y, c_len)
    mean = _mm_mask_r(y, seg) * (1.0 / HD)
    yc = y - mean
    var = _mm_mask_r(yc * yc, seg) * (1.0 / HD)
    yn = yc * lax.rsqrt(var + A_LN_EPS) * lng_ref[...] + lnb_ref[...]
    bonus = _mm_mask_r(r * k2 * rk_ref[...], seg)
    o_ref[...] = (yn + bonus * v) * g


def _rwkv_kernel(*refs, c_len, bb):
    _three_phase(functools.partial(_rwkv_seq, c_len=c_len), refs, 3, 14, bb)


def _rwkv(h_a, shift_prev, s0, p, c_len, bb):
    bn, seq, _ = h_a.shape
    nc = seq // c_len
    params = [p[n] for n in ("rw_mu", "rw_w0", "rw_w2", "rw_a0", "rw_a2", "rw_g2", "rw_kk", "rw_ka", "rw_rk",
                             "rw_ln_g", "rw_ln_b")]
    full = lambda a: pl.BlockSpec(a.shape, lambda b, c: (0,) * a.ndim)
    return pl.pallas_call(
        functools.partial(_rwkv_kernel, c_len=c_len, bb=bb),
        grid=(bn // bb, nc),
        in_specs=[pl.BlockSpec((bb, c_len, A_COLS), lambda b, c: (b, c, 0)),
                  pl.BlockSpec((bb, 1, A_COLS), lambda b, c: (b, 0, 0)),
                  pl.BlockSpec((bb, N_HEADS, HD, HD), lambda b, c: (b, 0, 0, 0))] + [full(a) for a in params],
        out_specs=[pl.BlockSpec((bb, c_len, MIX_W), lambda b, c: (b, c, 0)),
                   pl.BlockSpec((bb, N_HEADS, HD, HD), lambda b, c: (b, 0, 0, 0))],
        out_shape=[jax.ShapeDtypeStruct((bn, seq, MIX_W), F32),
                   jax.ShapeDtypeStruct((bn, N_HEADS, HD, HD), F32)],
        scratch_shapes=[pltpu.VMEM((bb, MIX_W, MIX_W), F32), pltpu.VMEM((bb, 8 + c_len, A_COLS), F32)],
        compiler_params=pltpu.CompilerParams(dimension_semantics=("parallel", "arbitrary"),
                                             vmem_limit_bytes=V7X_VMEM_LIMIT),
        name="rwkv",
    )(h_a, shift_prev.reshape(bn, 1, A_COLS), s0, *params)


def _gdn_seq(phase, h_ref, conv0_ref, s0_ref, cw_ref, alog_ref, dtb_ref, ng_ref, o_ref, sout_ref, sbd_ref, buf_ref,
             *, c_len):
    r_rows = 4 * c_len

    if phase == "init":
        sbd_ref[...] = jnp.zeros_like(sbd_ref)
        for hh in range(N_HEADS):
            sbd_ref[hh * HD:(hh + 1) * HD, hh * HD:(hh + 1) * HD] = s0_ref[hh]
        buf_ref[0:8, :] = conv0_ref[...]
        return
    if phase == "final":
        for hh in range(N_HEADS):
            sout_ref[hh] = sbd_ref[hh * HD:(hh + 1) * HD, hh * HD:(hh + 1) * HD]
        return

    hb = h_ref[...]
    qkv = hb[:, :B_QKV]
    buf_ref[8:8 + c_len, :] = qkv
    cw = cw_ref[...]
    conv = cw[3:4, :] * qkv
    for j in range(3):
        conv = conv + cw[j:j + 1, :] * buf_ref[5 + j:5 + j + c_len, :]
    buf_ref[0:8, :] = qkv[c_len - 8:c_len, :]
    conv = conv * jax.nn.sigmoid(conv)
    seg = _seg_ones(MIX_W, 6)

    def l2n(t):
        return t * lax.rsqrt(_mm_mask_r(t * t, seg) + 1e-6)

    q = l2n(conv[:, :MIX_W]) * (HD ** -0.5)
    k = l2n(conv[:, MIX_W:2 * MIX_W])
    v = conv[:, 2 * MIX_W:]
    z = hb[:, B_QKV:B_QKV + MIX_W]
    ab = hb[:, B_QKV + MIX_W:]
    gfull = -jnp.exp(alog_ref[...]) * _softplus(ab + dtb_ref[...])
    bfull = jax.nn.sigmoid(ab)

    incl, strict, head = _chunk_masks(c_len)
    hm = head.astype(F32)
    shift = int(math.log2(c_len))
    hrow = jnp.right_shift(_iota((r_rows, 128), 0), shift)
    lane = _iota((r_rows, 128), 1)
    gcol = jnp.sum(jnp.where(lane == hrow, _tile4(gfull), 0.0), axis=1, keepdims=True)
    bcol = jnp.sum(jnp.where(lane == hrow + N_HEADS, _tile4(bfull), 0.0), axis=1, keepdims=True)

    incl_f = incl.astype(F32)
    strict_f = strict.astype(F32)
    dmat = _mm_mask_l(incl_f, jnp.broadcast_to(gcol, (r_rows, r_rows)) * strict_f)
    g128 = jnp.broadcast_to(gcol, (r_rows, 128))
    gc = _mm_mask_l(incl_f, g128)[:, 0:1]
    ri = _iota((r_rows, r_rows), 0)
    ci = _iota((r_rows, r_rows), 1)
    after_f = jnp.logical_and(jnp.right_shift(ri, shift) == jnp.right_shift(ci, shift), ci > ri).astype(F32)
    tail = _mm_mask_l(after_f, g128)[:, 0:1]
    e_gc = jnp.exp(gc)
    e_tail = jnp.exp(tail)
    first = jnp.bitwise_and(_iota((r_rows, MIX_W), 0), c_len - 1) == 0
    g_last = jnp.sum(jnp.where(first, jnp.exp(gc + tail) * hm, 0.0), axis=0, keepdims=True)

    q_h = _tile4(q) * hm
    k_h = _tile4(k) * hm
    v_h = _tile4(v) * hm
    kb = k_h * bcol
    e_d = jnp.exp(dmat)
    m1 = _mm1(jnp.concatenate([kb, q_h], axis=0), k_h, _NT)
    a_mat = jnp.where(strict, m1[:r_rows] * e_d, 0.0)
    att = jnp.where(incl, m1[r_rows:] * e_d, 0.0)
    t_inv = _neumann_inverse(-a_mat, int(math.log2(c_len)))
    uw = _mm(t_inv, jnp.concatenate([v_h * bcol, kb * e_gc], axis=1))
    s0 = sbd_ref[...]
    v_new = uw[:, :MIX_W] - _mm(uw[:, MIX_W:], s0)
    o = _mm1(q_h * e_gc, s0) + _mm1(att, v_new)
    sbd_ref[...] = s0 * g_last + _mm(k_h * e_tail, v_new, _TN)

    o = _fold4(o, c_len)
    ms = _mm_mask_r(o * o, seg) * (1.0 / HD)
    o = o * lax.rsqrt(ms + NORM_EPS) * ng_ref[...]
    o_ref[...] = o * (z * jax.nn.sigmoid(z))


def _gdn_kernel(*refs, c_len, bb):
    _three_phase(functools.partial(_gdn_seq, c_len=c_len), refs, 3, 7, bb)


def _gdn(h_b, conv_prev8, s0, cw, alog, dtb, ng, c_len, bb):
    bn, seq, _ = h_b.shape
    nc = seq // c_len
    params = [cw, alog, dtb, ng]
    full = lambda a: pl.BlockSpec(a.shape, lambda b, c: (0,) * a.ndim)
    return pl.pallas_call(
        functools.partial(_gdn_kernel, c_len=c_len, bb=bb),
        grid=(bn // bb, nc),
        in_specs=[pl.BlockSpec((bb, c_len, B_COLS_PAD), lambda b, c: (b, c, 0)),
                  pl.BlockSpec((bb, 8, B_QKV), lambda b, c: (b, 0, 0)),
                  pl.BlockSpec((bb, N_HEADS, HD, HD), lambda b, c: (b, 0, 0, 0))] + [full(a) for a in params],
        out_specs=[pl.BlockSpec((bb, c_len, MIX_W), lambda b, c: (b, c, 0)),
                   pl.BlockSpec((bb, N_HEADS, HD, HD), lambda b, c: (b, 0, 0, 0))],
        out_shape=[jax.ShapeDtypeStruct((bn, seq, MIX_W), F32),
                   jax.ShapeDtypeStruct((bn, N_HEADS, HD, HD), F32)],
        scratch_shapes=[pltpu.VMEM((bb, MIX_W, MIX_W), F32), pltpu.VMEM((bb, 8 + c_len, B_QKV), F32)],
        compiler_params=pltpu.CompilerParams(dimension_semantics=("parallel", "arbitrary"),
                                             vmem_limit_bytes=V7X_VMEM_LIMIT),
        name="gdn",
    )(h_b, conv_prev8, s0, *params)


def _attn_prompt_kernel(lam_ref, q_ref, k_ref, v_ref, bias_ref, sg_ref, o_ref, q2_ref, m_ref, l_ref, acc_ref,
                        *, tq, scale, out_scale):
    i = pl.program_id(1)
    j = pl.program_id(2)

    @pl.when(j == 0)
    def _():
        m_ref[...] = jnp.full_like(m_ref, -jnp.inf)
        l_ref[...] = jnp.zeros_like(l_ref)
        acc_ref[...] = jnp.zeros_like(acc_ref)
        lane = _iota((tq, 2 * HD), 1)
        for hh in range(N_HEADS):
            qh = q_ref[:, hh * 2 * HD:(hh + 1) * 2 * HD] * scale
            q2_ref[hh, 0:tq, :] = jnp.where(lane < HD, qh, 0.0).astype(BF16)
            q2_ref[hh, tq:2 * tq, :] = jnp.where(lane >= HD, qh, 0.0).astype(BF16)

    @pl.when(j <= i)
    def _():
        for hh in range(N_HEADS):
            sl = slice(hh * 2 * HD, (hh + 1) * 2 * HD)
            s = lax.dot_general(q2_ref[hh], k_ref[:, sl].astype(BF16), _NT, preferred_element_type=F32)
            s = s + bias_ref[hh, jnp.minimum(i - j, 2)]
            m_old = m_ref[hh]
            m_new = jnp.maximum(m_old, jnp.max(s, axis=-1, keepdims=True))
            alpha = jnp.exp(m_old - m_new)
            p = jnp.exp(s - jnp.concatenate([m_new] * (tq // 128), axis=1))
            l_ref[hh] = alpha * l_ref[hh] + jnp.sum(p, axis=-1, keepdims=True)
            acc_ref[hh] = alpha * acc_ref[hh] + jnp.dot(p.astype(BF16), v_ref[:, sl].astype(BF16),
                                                        preferred_element_type=F32)
            m_ref[hh] = m_new

    @pl.when(j == i)
    def _():
        lam = lam_ref[0]
        for hh in range(N_HEADS):
            o = acc_ref[hh] / l_ref[hh]
            od = o[:tq] - lam * o[tq:]
            od = od * lax.rsqrt(jnp.mean(od * od, axis=-1, keepdims=True) + NORM_EPS) * sg_ref[...] * out_scale
            o_ref[:, hh * 2 * HD:(hh + 1) * 2 * HD] = od


def _attn_prompt(q, k, v, bias_near, lam, subln_g, out_scale, tq=256):
    bn, seq, _ = q.shape
    nq = seq // tq
    kv_spec = pl.BlockSpec((None, tq, C_W), lambda b, i, j: (b, jnp.minimum(i, j), 0))
    return pl.pallas_call(
        functools.partial(_attn_prompt_kernel, tq=tq, scale=HD ** -0.5, out_scale=out_scale),
        grid=(bn, nq, nq),
        in_specs=[pl.BlockSpec(memory_space=pltpu.SMEM),
                  pl.BlockSpec((None, tq, C_W), lambda b, i, j: (b, i, 0)),
                  kv_spec, kv_spec,
                  pl.BlockSpec(bias_near.shape, lambda b, i, j: (0, 0, 0, 0)),
                  pl.BlockSpec(subln_g.shape, lambda b, i, j: (0, 0))],
        out_specs=pl.BlockSpec((None, tq, C_W), lambda b, i, j: (b, i, 0)),
        out_shape=jax.ShapeDtypeStruct((bn, seq, C_W), F32),
        scratch_shapes=[pltpu.VMEM((N_HEADS, 2 * tq, 2 * HD), BF16), pltpu.VMEM((N_HEADS, 2 * tq, 2 * HD), F32),
                        pltpu.VMEM((N_HEADS, 2 * tq, 2 * HD), F32), pltpu.VMEM((N_HEADS, 2 * tq, 2 * HD), F32)],
        compiler_params=pltpu.CompilerParams(dimension_semantics=("parallel", "parallel", "arbitrary"),
                                             vmem_limit_bytes=V7X_VMEM_LIMIT),
        name="attn_prompt",
    )(lam, q, k, v, bias_near, subln_g)


def _attn_sample_kernel(pt_ref, lam_ref, q_ref, kn_ref, vn_ref, *rest, lq, scale, out_scale, ppb):
    del pt_ref
    kp_refs = rest[:ppb]
    vp_refs = rest[ppb:2 * ppb]
    bp_ref, bn_ref, sg_ref, o_ref, qz_ref, m_ref, l_ref, acc_ref = rest[2 * ppb:]
    p_idx = pl.program_id(1)

    @pl.when(p_idx == 0)
    def _():
        lane = _iota((lq, 2 * HD), 1)
        pieces = []
        for hh in range(N_HEADS):
            qh = q_ref[:, hh * 2 * HD:(hh + 1) * 2 * HD]
            pieces += [jnp.where(lane < HD, qh, 0.0), jnp.where(lane >= HD, qh, 0.0)]
        qz = jnp.concatenate(pieces, axis=0)
        qz_ref[...] = qz
        s = _mm_f32(qz, kn_ref[...], _NT) * scale + bn_ref[...]
        m = jnp.max(s, axis=-1, keepdims=True)
        p = jnp.exp(s - m)
        m_ref[...] = m
        l_ref[...] = jnp.sum(p, axis=-1, keepdims=True)
        acc_ref[...] = _mm_f32(p, vn_ref[...])

    qz = qz_ref[...].astype(BF16)
    cols = bp_ref.shape[1] // ppb
    scores = []
    for u in range(ppb):
        s = lax.dot_general(qz, kp_refs[u][...].astype(BF16), _NT, preferred_element_type=F32)
        scores.append(s * scale + bp_ref[:, u * cols:(u + 1) * cols])
    m_old = m_ref[...]
    m_new = m_old
    for s in scores:
        m_new = jnp.maximum(m_new, jnp.max(s, axis=-1, keepdims=True))
    alpha = jnp.exp(m_old - m_new)
    l_new = alpha * l_ref[...]
    acc = alpha * acc_ref[...]
    for u in range(ppb):
        p = jnp.exp(scores[u] - m_new)
        l_new = l_new + jnp.sum(p, axis=-1, keepdims=True)
        acc = acc + jnp.dot(p.astype(BF16), vp_refs[u][...].astype(BF16), preferred_element_type=F32)
    l_ref[...] = l_new
    acc_ref[...] = acc
    m_ref[...] = m_new

    @pl.when(p_idx == pl.num_programs(1) - 1)
    def _():
        lam = lam_ref[0]
        o = acc_ref[...] / l_ref[...]
        for hh in range(N_HEADS):
            o1 = o[(2 * hh) * lq:(2 * hh + 1) * lq]
            o2 = o[(2 * hh + 1) * lq:(2 * hh + 2) * lq]
            od = o1 - lam * o2
            od = od * lax.rsqrt(jnp.mean(od * od, axis=-1, keepdims=True) + NORM_EPS) * sg_ref[...] * out_scale
            o_ref[:, hh * 2 * HD:(hh + 1) * 2 * HD] = od


def _attn_sample(q, k_new, v_new, cache_k, cache_v, page_table, layer, bias_past, bias_new, lam, subln_g,
                 out_scale, ppb=8):
    bn, lq, _ = q.shape
    n_pages = page_table.shape[1]
    prow = cache_k.shape[2]
    rows = 2 * N_HEADS * lq
    seq_spec = pl.BlockSpec((None, lq, C_W), lambda b, p, pt: (b, 0, 0))
    new_spec = pl.BlockSpec((None, lq * N_HEADS, 2 * HD), lambda b, p, pt: (b, 0, 0))

    def page_spec(u):
        return pl.BlockSpec((None, None, prow, 2 * HD), lambda b, p, pt: (pt[b, p * ppb + u], layer, 0, 0))

    grid_spec = pltpu.PrefetchScalarGridSpec(
        num_scalar_prefetch=1,
        grid=(bn, n_pages // ppb),
        in_specs=[pl.BlockSpec(memory_space=pltpu.SMEM), seq_spec, new_spec, new_spec]
                 + [page_spec(u) for u in range(ppb)] * 2
                 + [pl.BlockSpec((rows, ppb * prow), lambda b, p, pt: (0, p)),
                    pl.BlockSpec(bias_new.shape, lambda b, p, pt: (0, 0)),
                    pl.BlockSpec(subln_g.shape, lambda b, p, pt: (0, 0))],
        out_specs=seq_spec,
        scratch_shapes=[pltpu.VMEM((rows, 2 * HD), F32), pltpu.VMEM((rows, 1), F32), pltpu.VMEM((rows, 1), F32),
                        pltpu.VMEM((rows, 2 * HD), F32)],
    )
    return pl.pallas_call(
        functools.partial(_attn_sample_kernel, lq=lq, scale=HD ** -0.5, out_scale=out_scale, ppb=ppb),
        grid_spec=grid_spec,
        out_shape=jax.ShapeDtypeStruct((bn, lq, C_W), F32),
        compiler_params=pltpu.CompilerParams(dimension_semantics=("parallel", "arbitrary")),
        name="attn_sample",
    )(page_table, lam, q, k_new, v_new, *([cache_k] * ppb), *([cache_v] * ppb), bias_past, bias_new, subln_g)


def _proj_out_kernel(x_ref, oa_ref, ob_ref, oc_ref, wo_ref, g_ref, wq_hi_ref, wq_lo_ref, xo_ref, xnt_ref, q_ref):
    dot = lambda a, w: jnp.dot(a.astype(BF16), w, preferred_element_type=F32)
    x = x_ref[...]
    x = x + dot(oa_ref[...], wo_ref[0:MIX_W, :]) + dot(ob_ref[...], wo_ref[MIX_W:2 * MIX_W, :]) \
        + dot(oc_ref[...], wo_ref[2 * MIX_W:, :])
    xo_ref[...] = x
    xn = x * lax.rsqrt(jnp.mean(x * x, axis=-1, keepdims=True) + NORM_EPS) * g_ref[...]
    xnt_ref[...] = jnp.transpose(xn).astype(BF16)
    hi, lo = _parts(xn, 2)
    q_ref[...] = (jnp.dot(lo, wq_hi_ref[...], preferred_element_type=F32)
                  + jnp.dot(hi, wq_lo_ref[...], preferred_element_type=F32)
                  + jnp.dot(hi, wq_hi_ref[...], preferred_element_type=F32))


def _proj_out(x2d, o_a, o_b, o_c, w_out, g, wq_hi, wq_lo, tm=256):
    t = x2d.shape[0]
    nq = wq_hi.shape[1]
    row = lambda w: pl.BlockSpec((tm, w), lambda i: (i, 0))
    full = lambda a: pl.BlockSpec(a.shape, lambda i: (0,) * a.ndim)
    return pl.pallas_call(
        _proj_out_kernel,
        grid=(t // tm,),
        in_specs=[row(D_MODEL), row(MIX_W), row(MIX_W), row(C_W), full(w_out), full(g), full(wq_hi), full(wq_lo)],
        out_specs=[row(D_MODEL), pl.BlockSpec((D_MODEL, tm), lambda i: (0, i)), row(nq)],
        out_shape=[jax.ShapeDtypeStruct((t, D_MODEL), F32), jax.ShapeDtypeStruct((D_MODEL, t), BF16),
                   jax.ShapeDtypeStruct((t, nq), F32)],
        compiler_params=pltpu.CompilerParams(dimension_semantics=("parallel",),
                                             vmem_limit_bytes=V7X_VMEM_LIMIT),
        name="proj_out",
    )(x2d, o_a, o_b, o_c, w_out, g, wq_hi, wq_lo)


def _top16(s, rid=None):
    n_rows, tm = s.shape
    if rid is None:
        rid = _iota((n_rows, tm), 0).astype(F32)
    kid = _iota((PK_TOPK, tm), 0)
    vals = jnp.zeros((PK_TOPK, tm), F32)
    idxs = jnp.zeros((PK_TOPK, tm), F32)
    for kk in range(PK_TOPK):
        m = jnp.max(s, axis=0, keepdims=True)
        ix = jnp.min(jnp.where(s == m, rid, float(1 << 20)), axis=0, keepdims=True)
        vals = jnp.where(kid == kk, m, vals)
        idxs = jnp.where(kid == kk, ix, idxs)
        s = jnp.where(rid == ix, -jnp.inf, s)
    return vals, idxs


def _peer_route_kernel(q_ref, keys_ref, e1_ref, f_ref, e2_ref, r_ref):
    tm = q_ref.shape[0]
    q = q_ref[...]
    s1 = _mm(keys_ref[0], q[:, :PK_NKEYS], _NT)
    s2 = _mm(keys_ref[1], q[:, PK_NKEYS:], _NT)
    sv1, si1 = _top16(s1)
    sv2, si2 = _top16(s2)
    lo_a, lo_b = 4, 3
    row16 = _iota((PK_TOPK, tm), 0)
    rows_a = [sv1[a:a + 1, :] + sv2 for a in range(lo_a)]
    rows_b = [jnp.where(row16 >= lo_a, sv1 + sv2[b:b + 1, :], -jnp.inf) for b in range(lo_b)]
    cand = jnp.concatenate(rows_a + rows_b, axis=0)
    n_a = lo_a * PK_TOPK
    rid_a = _iota((n_a, tm), 0)
    row_b = _iota((lo_b * PK_TOPK, tm), 0)
    rid_b = jnp.bitwise_and(row_b, PK_TOPK - 1) * PK_TOPK + jnp.right_shift(row_b, 4)
    tv, ti = _top16(cand, jnp.concatenate([rid_a, rid_b], axis=0).astype(F32))
    rank_a = jnp.floor(ti * (1.0 / PK_TOPK))
    z = jnp.sum(jnp.exp(tv - tv[0:1, :]), axis=0, keepdims=True)
    aid = _iota((PK_TOPK, tm), 0).astype(F32)
    cnt = jnp.zeros((PK_TOPK, tm), F32)
    for kk in range(PK_TOPK):
        cnt = cnt + (aid == rank_a[kk:kk + 1, :]).astype(F32)
    ex1 = jnp.exp(sv1 - sv1[0:1, :]) / z
    ex2 = jnp.exp(sv2 - sv2[0:1, :])
    kid = _iota((PK_NKEYS, tm), 0).astype(F32)
    e1 = jnp.zeros((PK_NKEYS, tm), F32)
    f = jnp.zeros((PK_NKEYS, tm), F32)
    e2 = jnp.zeros((PK_NKEYS, tm), F32)
    r = jnp.full((PK_NKEYS, tm), float(PK_TOPK), F32)
    for kk in range(PK_TOPK):
        hit1 = kid == si1[kk:kk + 1, :]
        e1 = jnp.where(hit1, ex1[kk:kk + 1, :], e1)
        f = jnp.where(hit1, cnt[kk:kk + 1, :], f)
        hit2 = kid == si2[kk:kk + 1, :]
        e2 = jnp.where(hit2, ex2[kk:kk + 1, :], e2)
        r = jnp.where(hit2, float(kk), r)
    e1_ref[...] = e1
    f_ref[...] = f
    e2_ref[...] = e2.astype(BF16)
    r_ref[...] = r.astype(BF16)


def _peer_route(q, keys, tm=256):
    t = q.shape[0]
    tab = jax.ShapeDtypeStruct((PK_HEADS, PK_NKEYS, t), F32)
    tab16 = jax.ShapeDtypeStruct((PK_HEADS, PK_NKEYS, t), BF16)
    tab_spec = pl.BlockSpec((None, PK_NKEYS, tm), lambda i, h: (h, 0, i))
    return pl.pallas_call(
        _peer_route_kernel,
        grid=(t // tm, PK_HEADS),
        in_specs=[pl.BlockSpec((tm, 2 * PK_NKEYS), lambda i, h: (i, h)),
                  pl.BlockSpec((None, 2, PK_NKEYS, PK_NKEYS), lambda i, h: (h, 0, 0, 0))],
        out_specs=[tab_spec] * 4,
        out_shape=[tab, tab, tab16, tab16],
        compiler_params=pltpu.CompilerParams(dimension_semantics=("parallel", "parallel")),
        name="peer_route",
    )(q, keys)


def _peer_dense_kernel(x_ref, xnt_ref, u_ref, vt_ref, e1_ref, f_ref, e2_ref, r_ref, o_ref, acc_ref, act_ref, c_ref,
                       *, ib, tm):
    ii = pl.program_id(1)

    @pl.when(ii == 0)
    def _():
        acc_ref[...] = jnp.zeros_like(acc_ref)

    act = jnp.dot(u_ref[...], xnt_ref[...], preferred_element_type=F32)
    pk = 16
    ng = PK_NKEYS // pk
    act_ref[...] = act.reshape(ib * ng, pk, tm)
    tile = lambda ref, hh, s, lanes: jnp.broadcast_to(ref[hh, s:s + 1, lanes], (pk, 128)).astype(BF16)
    group = 2
    for c in range(tm // 128):
        lanes = slice(c * 128, (c + 1) * 128)
        for s0 in range(0, ib, group):
            gates = [None] * group
            for hh in range(PK_HEADS):
                rank = r_ref[hh, :, :, lanes]
                e2 = e2_ref[hh, :, :, lanes]
                for ds in range(group):
                    e1 = tile(e1_ref, hh, s0 + ds, lanes)
                    f = tile(f_ref, hh, s0 + ds, lanes)
                    term = jnp.maximum(jnp.minimum(f - rank, e1), 0.0) * e2
                    gates[ds] = term if gates[ds] is None else gates[ds] + term
            for ds in range(group):
                rows = slice((s0 + ds) * ng, (s0 + ds + 1) * ng)
                a = act_ref[rows, :, lanes]
                a = 0.5 * a * (1.0 + lax.erf(a * math.sqrt(0.5)))
                c_ref[rows, :, lanes] = (gates[ds].astype(F32) * a).astype(BF16)
    acc_ref[...] += jnp.dot(vt_ref[...], c_ref[...].reshape(ib * PK_NKEYS, tm), preferred_element_type=F32)

    @pl.when(ii == pl.num_programs(1) - 1)
    def _():
        o_ref[...] = x_ref[...] + jnp.transpose(acc_ref[...])


def _peer_dense(x2d, xnt, u_bf, vt_bf, e1, f, e2, r, tm=512, ib=8):
    t = x2d.shape[0]
    ni = PK_NKEYS // ib
    tab_i = pl.BlockSpec((PK_HEADS, ib, tm), lambda tt, ii: (0, ii, tt))
    pk = 16
    ng = PK_NKEYS // pk
    tab_j = pl.BlockSpec((PK_HEADS, ng, pk, tm), lambda tt, ii: (0, 0, 0, tt))
    e2 = e2.reshape(PK_HEADS, ng, pk, t)
    r = r.reshape(PK_HEADS, ng, pk, t)
    return pl.pallas_call(
        functools.partial(_peer_dense_kernel, ib=ib, tm=tm),
        grid=(t // tm, ni),
        in_specs=[pl.BlockSpec((tm, D_MODEL), lambda tt, ii: (tt, 0)),
                  pl.BlockSpec((D_MODEL, tm), lambda tt, ii: (0, tt)),
                  pl.BlockSpec((ib * PK_NKEYS, D_MODEL), lambda tt, ii: (ii, 0)),
                  pl.BlockSpec((D_MODEL, ib * PK_NKEYS), lambda tt, ii: (0, ii)),
                  tab_i, tab_i, tab_j, tab_j],
        out_specs=pl.BlockSpec((tm, D_MODEL), lambda tt, ii: (tt, 0)),
        out_shape=jax.ShapeDtypeStruct((t, D_MODEL), F32),
        scratch_shapes=[pltpu.VMEM((D_MODEL, tm), F32), pltpu.VMEM((ib * ng, pk, tm), F32),
                        pltpu.VMEM((ib * ng, pk, tm), BF16)],
        compiler_params=pltpu.CompilerParams(dimension_semantics=("parallel", "arbitrary"),
                                             vmem_limit_bytes=V7X_VMEM_LIMIT),
        name="peer_dense",
    )(x2d, xnt, u_bf, vt_bf, e1, f, e2, r)


def _t5_bucket(rel):
    n = jnp.maximum(-rel, 0)
    max_exact = REL_BUCKETS // 2
    nf = jnp.maximum(n, 1).astype(F32)
    large = max_exact + (jnp.log(nf / max_exact) / math.log(REL_MAX_DIST / max_exact)
                         * (REL_BUCKETS - max_exact)).astype(jnp.int32)
    large = jnp.minimum(large, REL_BUCKETS - 1)
    return jnp.where(n < max_exact, n, large)


def _bias_table(rel_bias, rel):
    bucket = _t5_bucket(rel)[None]
    col = lambda b: rel_bias[b].astype(F32).reshape((-1,) + (1,) * rel.ndim)
    out = jnp.broadcast_to(col(0), (rel_bias.shape[1],) + rel.shape)
    for b in range(1, REL_BUCKETS):
        out = jnp.where(bucket == b, col(b), out)
    return out


def _prompt_bias_tiles(rel_bias, tq):
    rel = jnp.arange(tq)[None, :] - jnp.arange(tq)[:, None]
    tiles = []
    for d in range(3):
        b = _bias_table(rel_bias, rel - d * tq)
        if d == 0:
            b = jnp.where(rel <= 0, b, -jnp.inf)
        tiles.append(jnp.concatenate([b, b], axis=1))
    return jnp.stack(tiles, axis=1)


def _prep_layer(l, W):
    row = lambda a: a.reshape(1, -1).astype(F32)
    w_in = W["w_in"][l]
    o_b = A_COLS
    o_c = A_COLS + 4 * MIX_W + 2 * N_HEADS
    w_cat = jnp.concatenate([
        w_in[:, :A_COLS],
        w_in[:, o_b:o_b + B_QKV],
        w_in[:, o_b + B_QKV + 2 * N_HEADS:o_c],
        w_in[:, o_b + B_QKV:o_b + B_QKV + 2 * N_HEADS],
        jnp.zeros((D_MODEL, 128 - 2 * N_HEADS), F32),
        w_in[:, o_c:],
    ], axis=1).astype(BF16)
    pad_lane = lambda a, off: jnp.zeros((1, 128), F32).at[0, off:off + a.shape[0]].set(a)
    alog = pad_lane(W["dn_a_log"][l], 0)
    dtb = pad_lane(W["dn_dt_bias"][l], 0)
    f32 = F32
    lam_init = 0.8 - 0.6 * math.exp(-0.3 * l)
    lam = (jnp.exp(jnp.sum(W["df_lq1"][l].astype(f32) * W["df_lk1"][l].astype(f32)))
           - jnp.exp(jnp.sum(W["df_lq2"][l].astype(f32) * W["df_lk2"][l].astype(f32))) + lam_init)
    wq = W["pk_wq"][l]
    wq_hi = wq.astype(BF16)
    wq_lo = (wq - wq_hi.astype(F32)).astype(BF16)
    return dict(
        rms_mix_g=row(W["rms_mix_g"][l]), w_cat=w_cat,
        qg=row(jnp.tile(W["df_qn_g"][l].reshape(-1), N_HEADS)),
        kg=row(jnp.tile(W["df_kn_g"][l].reshape(-1), N_HEADS)),
        rw_mu=row(W["rw_mu"][l]), rw_w0=row(W["rw_w0"][l]), rw_w2=W["rw_w2"][l], rw_a0=row(W["rw_a0"][l]),
        rw_a2=W["rw_a2"][l], rw_g2=W["rw_g2"][l], rw_kk=row(W["rw_kk"][l]), rw_ka=row(W["rw_ka"][l]),
        rw_rk=row(W["rw_rk"][l]), rw_ln_g=row(W["rw_ln_g"][l]), rw_ln_b=row(W["rw_ln_b"][l]),
        dn_conv_w=W["dn_conv_w"][l], dn_alog=alog, dn_dtb=dtb,
        dn_norm_g=row(jnp.tile(W["dn_norm_g"][l], N_HEADS)),
        lam=lam.reshape(1).astype(F32), lam_init=lam_init, subln_g=row(W["df_subln_g"][l]),
        w_out=W["w_out"][l].astype(BF16), rms_ffn_g=row(W["rms_ffn_g"][l]), wq_hi=wq_hi, wq_lo=wq_lo,
        pk_keys=W["pk_keys"][l], pk_u=W["pk_u"][l].astype(BF16), pk_vt=jnp.transpose(W["pk_v"][l]).astype(BF16),
    )


def _layer(x, shift_prev, s_rwkv, conv_prev, s_dn, attn_fn, P, c_len, bb):
    bn, seq, _ = x.shape
    t = bn * seq
    x2d = x.reshape(t, D_MODEL)
    h_a, h_b, q, k, v = _proj_in(x2d, P["rms_mix_g"], P["w_cat"], P["qg"], P["kg"])
    h_a3 = h_a.reshape(bn, seq, A_COLS)
    h_b3 = h_b.reshape(bn, seq, B_COLS_PAD)
    o_a, s_rwkv_new = _rwkv(h_a3, shift_prev, s_rwkv, P, c_len, bb)
    conv8 = jnp.concatenate([jnp.zeros((bn, 5, B_QKV), F32), conv_prev], axis=1)
    o_b, s_dn_new = _gdn(h_b3, conv8, s_dn, P["dn_conv_w"], P["dn_alog"], P["dn_dtb"], P["dn_norm_g"], c_len,
                         bb)
    q3, k3, v3 = (a.reshape(bn, seq, C_W) for a in (q, k, v))
    o_c = attn_fn(q3, k3, v3)
    x_new, xnt, pq = _proj_out(x2d, o_a.reshape(t, MIX_W), o_b.reshape(t, MIX_W), o_c.reshape(t, C_W),
                               P["w_out"], P["rms_ffn_g"], P["wq_hi"], P["wq_lo"])
    e1, f, e2, r = _peer_route(pq, P["pk_keys"])
    y = _peer_dense(x_new, xnt, P["pk_u"], P["pk_vt"], e1, f, e2, r)
    k_rows = k.reshape(bn, seq, N_HEADS, 2 * HD)
    v_rows = v.reshape(bn, seq, N_HEADS, 2 * HD)
    shift_new = h_a3[:, -1]
    conv_new = h_b3[:, -3:, :B_QKV]
    return y.reshape(bn, seq, D_MODEL), k_rows, v_rows, s_rwkv_new, shift_new, s_dn_new, conv_new


def kernel(x_prompt, x_sample, cache_k, cache_v, state_rwkv, state_rwkv_shift, state_dn, state_dn_conv, page_table, rms_mix_g, w_in, w_out, rw_mu, rw_w0, rw_w2, rw_a0, rw_a2, rw_g2, rw_kk, rw_ka, rw_rk, rw_ln_g, rw_ln_b, dn_conv_w, dn_a_log, dn_dt_bias, dn_norm_g, df_qn_g, df_kn_g, df_lq1, df_lk1, df_lq2, df_lk2, df_subln_g, rel_bias, rms_ffn_g, pk_wq, pk_keys, pk_u, pk_v):
    W = dict(rms_mix_g=rms_mix_g, w_in=w_in, w_out=w_out, rw_mu=rw_mu, rw_w0=rw_w0, rw_w2=rw_w2, rw_a0=rw_a0,
             rw_a2=rw_a2, rw_g2=rw_g2, rw_kk=rw_kk, rw_ka=rw_ka, rw_rk=rw_rk, rw_ln_g=rw_ln_g, rw_ln_b=rw_ln_b,
             dn_conv_w=dn_conv_w, dn_a_log=dn_a_log, dn_dt_bias=dn_dt_bias, dn_norm_g=dn_norm_g,
             df_qn_g=df_qn_g, df_kn_g=df_kn_g, df_lq1=df_lq1, df_lk1=df_lk1, df_lq2=df_lq2, df_lk2=df_lk2,
             df_subln_g=df_subln_g, rms_ffn_g=rms_ffn_g, pk_wq=pk_wq, pk_keys=pk_keys, pk_u=pk_u, pk_v=pk_v)
    depth = w_in.shape[0]
    bp, seq_p, _ = x_prompt.shape
    bs, seq_s, _ = x_sample.shape
    n_pages = page_table.shape[1]
    page = cache_k.shape[2]
    past = n_pages * page
    tq = 256
    ck = cache_k.reshape(cache_k.shape[0], depth, page * N_HEADS, 2 * HD)
    cv = cache_v.reshape(cache_v.shape[0], depth, page * N_HEADS, 2 * HD)

    bias_near = _prompt_bias_tiles(rel_bias, tq)
    q_pos = past + jnp.arange(seq_s)
    same_head = jnp.eye(N_HEADS, dtype=bool)[:, None, None, :]

    def expand(b):
        bx = jnp.where(same_head, b[..., None], -jnp.inf)
        return jnp.broadcast_to(bx[:, None], (N_HEADS, 2) + bx.shape[1:]).reshape(2 * N_HEADS * seq_s, -1)

    bias_past = expand(_bias_table(rel_bias, jnp.arange(past)[None, :] - q_pos[:, None]))
    rel_new = q_pos[None, :] - q_pos[:, None]
    bias_new = expand(jnp.where(rel_new <= 0, _bias_table(rel_bias, rel_new), -jnp.inf))

    xp, xs = x_prompt, x_sample
    outs = [[] for _ in range(12)]
    zeros = lambda *s: jnp.zeros(s, F32)
    for l in range(depth):
        P = _prep_layer(l, W)
        out_scale = 1.0 - P["lam_init"]
        attn_p = lambda q, k, v: _attn_prompt(q, k, v, bias_near, P["lam"], P["subln_g"], out_scale, tq)
        by_head = lambda a: a.reshape(bs, seq_s * N_HEADS, 2 * HD)
        attn_s = lambda q, k, v: _attn_sample(q, by_head(k), by_head(v), ck, cv, page_table, l, bias_past, bias_new,
                                              P["lam"], P["subln_g"], out_scale)
        xp, kp, vp, sap, shp, sbp, cvp = _layer(
            xp, zeros(bp, A_COLS), zeros(bp, N_HEADS, HD, HD), zeros(bp, 3, B_QKV), zeros(bp, N_HEADS, HD, HD),
            attn_p, P, 64, 2)
        xs, ksn, vsn, sas, shs, sbs, cvs = _layer(
            xs, state_rwkv_shift[l], state_rwkv[l], state_dn_conv[l], state_dn[l], attn_s, P, seq_s, 8)
        for lst, val in zip(outs, (kp, vp, ksn, vsn, sap, sas, shp, shs, sbp, sbs, cvp, cvs)):
            lst.append(val)
    stack = lambda i, ax: jnp.stack(outs[i], axis=ax)
    return (xp, xs, stack(0, 1), stack(1, 1), stack(2, 1), stack(3, 1),
            stack(4, 0), stack(5, 0), stack(6, 0), stack(7, 0), stack(8, 0), stack(9, 0), stack(10, 0), stack(11, 0))
```

```python
import functools
import math

import jax
import jax.numpy as jnp
from jax import lax
from jax.experimental import pallas as pl
from jax.experimental.pallas import tpu as pltpu

F32 = jnp.float32
BF16 = jnp.bfloat16

D_MODEL = 1024
N_HEADS = 4
HD = 64
MIX_W = N_HEADS * HD
A_COLS = 1024
B_QKV = 3 * MIX_W
B_COLS_PAD = 1152
C_W = 512
NORM_EPS = 1e-6
A_LN_EPS = 64e-5
REL_BUCKETS = 32
REL_MAX_DIST = 128
PK_HEADS = 8
PK_NKEYS = 128
PK_TOPK = 16
V7X_VMEM_LIMIT = 56 * 1024 * 1024


def _parts(a, n):
    out = []
    r = a
    for i in range(n):
        p = r.astype(BF16)
        out.append(p)
        if i + 1 < n:
            r = r - p.astype(F32)
    return out


_NN = (((1,), (0,)), ((), ()))
_NT = (((1,), (1,)), ((), ()))
_TN = (((0,), (0,)), ((), ()))


def _mm(a, b, dims=_NN, na=2, nb=2, order=1):
    ap = _parts(a, na)
    bp = _parts(b, nb)
    acc = None
    for s in range(order, -1, -1):
        for i in range(na):
            j = s - i
            if 0 <= j < nb:
                t = lax.dot_general(ap[i], bp[j], dims, preferred_element_type=F32)
                acc = t if acc is None else acc + t
    return acc


def _mm1(a, b, dims=_NN):
    return _mm(a, b, dims, na=1, nb=1, order=0)


def _mm_f32(a, b, dims=_NN):
    return lax.dot_general(a, b, dims, precision=lax.Precision.HIGHEST, preferred_element_type=F32)


def _mm_mask_r(a, mask, dims=_NN):
    return _mm(a, mask, dims, na=2, nb=1, order=1)


def _mm_mask_l(mask, b, dims=_NN):
    return _mm(mask, b, dims, na=1, nb=3, order=2)


def _iota(shape, dim):
    return lax.broadcasted_iota(jnp.int32, shape, dim)


def _seg_ones(n, seg_shift):
    r = jnp.right_shift(_iota((n, n), 0), seg_shift)
    c = jnp.right_shift(_iota((n, n), 1), seg_shift)
    return (r == c).astype(F32)


def _tile4(z):
    return jnp.concatenate([z, z, z, z], axis=0)


def _fold4(y, c):
    return y[0:c] + y[c:2 * c] + y[2 * c:3 * c] + y[3 * c:4 * c]


def _neumann_inverse(n, steps):
    rows = n.shape[0]
    eye = (_iota((rows, rows), 0) == _iota((rows, rows), 1)).astype(F32)
    t = eye + n
    p = n
    for _ in range(steps - 1):
        p = _mm1(p, p)
        t = t + _mm1(t, p)
    return t


def _softplus(y):
    return jnp.maximum(y, 0.0) + jnp.log1p(jnp.exp(-jnp.abs(y)))


def _chunk_masks(c_len):
    r = 4 * c_len
    shift = int(math.log2(c_len))
    ri = _iota((r, r), 0)
    ci = _iota((r, r), 1)
    same = jnp.right_shift(ri, shift) == jnp.right_shift(ci, shift)
    incl = jnp.logical_and(same, ci <= ri)
    strict = jnp.logical_and(same, ci < ri)
    head = jnp.right_shift(_iota((r, MIX_W), 0), shift) == jnp.right_shift(_iota((r, MIX_W), 1), 6)
    return incl, strict, head


def _proj_in_kernel(x_ref, g_ref, w_ref, qg_ref, kg_ref, ha_ref, hb_ref, q_ref, k_ref, v_ref):
    x = x_ref[...]
    xn = x * lax.rsqrt(jnp.mean(x * x, axis=-1, keepdims=True) + NORM_EPS) * g_ref[...]
    h = jnp.dot(xn.astype(BF16), w_ref[...], preferred_element_type=F32)
    o = A_COLS
    ha_ref[...] = h[:, :o]
    hb_ref[...] = h[:, o:o + B_COLS_PAD]
    o += B_COLS_PAD
    seg = _seg_ones(C_W, 6)

    def qk_norm(t, g):
        ms = _mm_mask_r(t * t, seg) * (1.0 / HD)
        return t * lax.rsqrt(ms + NORM_EPS) * g

    q_ref[...] = qk_norm(h[:, o:o + C_W], qg_ref[...])
    k_ref[...] = qk_norm(h[:, o + C_W:o + 2 * C_W], kg_ref[...])
    v_ref[...] = h[:, o + 2 * C_W:o + 3 * C_W]


def _proj_in(x2d, g, w_cat, qg, kg, tm=256):
    t = x2d.shape[0]
    n = w_cat.shape[1]
    row = lambda w: pl.BlockSpec((tm, w), lambda i: (i, 0))
    full = lambda a: pl.BlockSpec(a.shape, lambda i: (0,) * a.ndim)
    return pl.pallas_call(
        _proj_in_kernel,
        grid=(t // tm,),
        in_specs=[row(D_MODEL), full(g), full(w_cat), full(qg), full(kg)],
        out_specs=[row(A_COLS), row(B_COLS_PAD), row(C_W), row(C_W), row(C_W)],
        out_shape=[jax.ShapeDtypeStruct((t, A_COLS), F32), jax.ShapeDtypeStruct((t, B_COLS_PAD), F32),
                   jax.ShapeDtypeStruct((t, C_W), F32), jax.ShapeDtypeStruct((t, C_W), F32),
                   jax.ShapeDtypeStruct((t, C_W), F32)],
        compiler_params=pltpu.CompilerParams(dimension_semantics=("parallel",),
                                             vmem_limit_bytes=V7X_VMEM_LIMIT),
        name="proj_in",
    )(x2d, g, w_cat, qg, kg)


def _three_phase(seq_fn, refs, n_shared_from, n_shared_to, bb):
    c = pl.program_id(1)

    def per_seq(bi):
        return [r if n_shared_from <= n < n_shared_to else r.at[bi] for n, r in enumerate(refs)]

    @pl.when(c == 0)
    def _():
        for bi in range(bb):
            seq_fn("init", *per_seq(bi))

    for bi in range(bb):
        seq_fn("chunk", *per_seq(bi))

    @pl.when(c == pl.num_programs(1) - 1)
    def _():
        for bi in range(bb):
            seq_fn("final", *per_seq(bi))


def _rwkv_seq(phase, h_ref, shift_ref, s0_ref, mu_ref, w0_ref, w2_ref, a0_ref, a2_ref, g2_ref, kk_ref, ka_ref,
              rk_ref, lng_ref, lnb_ref, o_ref, sout_ref, sbd_ref, buf_ref, *, c_len):
    r_rows = 4 * c_len

    if phase == "init":
        sbd_ref[...] = jnp.zeros_like(sbd_ref)
        for hh in range(N_HEADS):
            sbd_ref[hh * HD:(hh + 1) * HD, hh * HD:(hh + 1) * HD] = s0_ref[hh]
        buf_ref[0:8, :] = jnp.broadcast_to(shift_ref[...], (8, A_COLS))
        return
    if phase == "final":
        for hh in range(N_HEADS):
            sout_ref[hh] = sbd_ref[hh * HD:(hh + 1) * HD, hh * HD:(hh + 1) * HD]
        return

    x = h_ref[...]
    buf_ref[8:8 + c_len, :] = x
    prev = buf_ref[7:7 + c_len, :]
    buf_ref[0:8, :] = x[c_len - 8:c_len, :]

    hs = x + (prev - x) * mu_ref[...]
    r = hs[:, 0:MIX_W]
    k = hs[:, MIX_W:2 * MIX_W]
    v = hs[:, 2 * MIX_W:3 * MIX_W]
    xw = hs[:, 768:832]
    xa = hs[:, 832:896]
    xg = hs[:, 896:1024]
    w = -_softplus(-(w0_ref[...] + _mm(jnp.tanh(xw), w2_ref[...]))) - 0.5
    a = jax.nn.sigmoid(a0_ref[...] + _mm(xa, a2_ref[...]))
    g = _mm(jax.nn.sigmoid(xg), g2_ref[...])
    seg = _seg_ones(MIX_W, 6)
    kkv = k * kk_ref[...]
    kkn = kkv * lax.rsqrt(_mm_mask_r(kkv * kkv, seg) + 1e-6)
    k2 = k * (1.0 + (a - 1.0) * ka_ref[...])
    logd = -jnp.exp(w)
    tri = (_iota((c_len, c_len), 1) <= _iota((c_len, c_len), 0)).astype(F32)
    cum = _mm_mask_l(tri, logd)
    p_in = jnp.exp(cum)
    p_inv = jnp.exp(-cum)
    p_prev = jnp.exp(cum - logd)

    incl, strict, head = _chunk_masks(c_len)
    hm = head.astype(F32)
    a_t = _tile4(-kkn * p_prev) * hm
    b_t = _tile4(kkn * a * p_inv) * hm
    k_t = _tile4(k2 * p_inv) * hm
    r_t = _tile4(r * p_in) * hm
    v_t = _tile4(v) * hm

    s0 = sbd_ref[...]
    ar = jnp.concatenate([a_t, r_t], axis=0)
    bk = jnp.concatenate([b_t, k_t], axis=0)
    gram = _mm1(ar, bk, _NT)
    h0 = _mm1(ar, s0, _NT)
    l_ab = jnp.where(strict, gram[:r_rows, :r_rows], 0.0)
    l_ak = jnp.where(strict, gram[:r_rows, r_rows:], 0.0)
    m_rb = jnp.where(incl, gram[r_rows:, :r_rows], 0.0)
    m_rk = jnp.where(incl, gram[r_rows:, r_rows:], 0.0)
    t_inv = _neumann_inverse(l_ab, int(math.log2(c_len)))
    u = _mm1(t_inv, h0[:r_rows] + _mm1(l_ak, v_t))
    uv = jnp.concatenate([u, v_t], axis=0)
    y = h0[r_rows:] + _mm1(jnp.concatenate([m_rb, m_rk], axis=1), uv)
    sbd_ref[...] = (s0 + _mm1(uv, bk, _TN)) * p_in[c_len - 1:c_len, :]

    y = _fold4(y, c_len)
    mean = _mm_mask_r(y, seg) * (1.0 / HD)
    yc = y - mean
    var = _mm_mask_r(yc * yc, seg) * (1.0 / HD)
    yn = yc * lax.rsqrt(var + A_LN_EPS) * lng_ref[...] + lnb_ref[...]
    bonus = _mm_mask_r(r * k2 * rk_ref[...], seg)
    o_ref[...] = (yn + bonus * v) * g


def _rwkv_kernel(*refs, c_len, bb):
    _three_phase(functools.partial(_rwkv_seq, c_len=c_len), refs, 3, 14, bb)


def _rwkv(h_a, shift_prev, s0, p, c_len, bb):
    bn, seq, _ = h_a.shape
    nc = seq // c_len
    params = [p[n] for n in ("rw_mu", "rw_w0", "rw_w2", "rw_a0", "rw_a2", "rw_g2", "rw_kk", "rw_ka", "rw_rk",
                             "rw_ln_g", "rw_ln_b")]
    full = lambda a: pl.BlockSpec(a.shape, lambda b, c: (0,) * a.ndim)
    return pl.pallas_call(
        functools.partial(_rwkv_kernel, c_len=c_len, bb=bb),
        grid=(bn // bb, nc),
        in_specs=[pl.BlockSpec((bb, c_len, A_COLS), lambda b, c: (b, c, 0)),
                  pl.BlockSpec((bb, 1, A_COLS), lambda b, c: (b, 0, 0)),
                  pl.BlockSpec((bb, N_HEADS, HD, HD), lambda b, c: (b, 0, 0, 0))] + [full(a) for a in params],
        out_specs=[pl.BlockSpec((bb, c_len, MIX_W), lambda b, c: (b, c, 0)),
                   pl.BlockSpec((bb, N_HEADS, HD, HD), lambda b, c: (b, 0, 0, 0))],
        out_shape=[jax.ShapeDtypeStruct((bn, seq, MIX_W), F32),
                   jax.ShapeDtypeStruct((bn, N_HEADS, HD, HD), F32)],
        scratch_shapes=[pltpu.VMEM((bb, MIX_W, MIX_W), F32), pltpu.VMEM((bb, 8 + c_len, A_COLS), F32)],
        compiler_params=pltpu.CompilerParams(dimension_semantics=("parallel", "arbitrary"),
                                             vmem_limit_bytes=V7X_VMEM_LIMIT),
        name="rwkv",
    )(h_a, shift_prev.reshape(bn, 1, A_COLS), s0, *params)


def _gdn_seq(phase, h_ref, conv0_ref, s0_ref, cw_ref, alog_ref, dtb_ref, ng_ref, o_ref, sout_ref, sbd_ref, buf_ref,
             *, c_len):
    r_rows = 4 * c_len

    if phase == "init":
        sbd_ref[...] = jnp.zeros_like(sbd_ref)
        for hh in range(N_HEADS):
            sbd_ref[hh * HD:(hh + 1) * HD, hh * HD:(hh + 1) * HD] = s0_ref[hh]
        buf_ref[0:8, :] = conv0_ref[...]
        return
    if phase == "final":
        for hh in range(N_HEADS):
            sout_ref[hh] = sbd_ref[hh * HD:(hh + 1) * HD, hh * HD:(hh + 1) * HD]
        return

    hb = h_ref[...]
    qkv = hb[:, :B_QKV]
    buf_ref[8:8 + c_len, :] = qkv
    cw = cw_ref[...]
    conv = cw[3:4, :] * qkv
    for j in range(3):
        conv = conv + cw[j:j + 1, :] * buf_ref[5 + j:5 + j + c_len, :]
    buf_ref[0:8, :] = qkv[c_len - 8:c_len, :]
    conv = conv * jax.nn.sigmoid(conv)
    seg = _seg_ones(MIX_W, 6)

    def l2n(t):
        return t * lax.rsqrt(_mm_mask_r(t * t, seg) + 1e-6)

    q = l2n(conv[:, :MIX_W]) * (HD ** -0.5)
    k = l2n(conv[:, MIX_W:2 * MIX_W])
    v = conv[:, 2 * MIX_W:]
    z = hb[:, B_QKV:B_QKV + MIX_W]
    ab = hb[:, B_QKV + MIX_W:]
    gfull = -jnp.exp(alog_ref[...]) * _softplus(ab + dtb_ref[...])
    bfull = jax.nn.sigmoid(ab)

    incl, strict, head = _chunk_masks(c_len)
    hm = head.astype(F32)
    shift = int(math.log2(c_len))
    hrow = jnp.right_shift(_iota((r_rows, 128), 0), shift)
    lane = _iota((r_rows, 128), 1)
    gcol = jnp.sum(jnp.where(lane == hrow, _tile4(gfull), 0.0), axis=1, keepdims=True)
    bcol = jnp.sum(jnp.where(lane == hrow + N_HEADS, _tile4(bfull), 0.0), axis=1, keepdims=True)

    incl_f = incl.astype(F32)
    strict_f = strict.astype(F32)
    dmat = _mm_mask_l(incl_f, jnp.broadcast_to(gcol, (r_rows, r_rows)) * strict_f)
    g128 = jnp.broadcast_to(gcol, (r_rows, 128))
    gc = _mm_mask_l(incl_f, g128)[:, 0:1]
    ri = _iota((r_rows, r_rows), 0)
    ci = _iota((r_rows, r_rows), 1)
    after_f = jnp.logical_and(jnp.right_shift(ri, shift) == jnp.right_shift(ci, shift), ci > ri).astype(F32)
    tail = _mm_mask_l(after_f, g128)[:, 0:1]
    e_gc = jnp.exp(gc)
    e_tail = jnp.exp(tail)
    first = jnp.bitwise_and(_iota((r_rows, MIX_W), 0), c_len - 1) == 0
    g_last = jnp.sum(jnp.where(first, jnp.exp(gc + tail) * hm, 0.0), axis=0, keepdims=True)

    q_h = _tile4(q) * hm
    k_h = _tile4(k) * hm
    v_h = _tile4(v) * hm
    kb = k_h * bcol
    e_d = jnp.exp(dmat)
    m1 = _mm1(jnp.concatenate([kb, q_h], axis=0), k_h, _NT)
    a_mat = jnp.where(strict, m1[:r_rows] * e_d, 0.0)
    att = jnp.where(incl, m1[r_rows:] * e_d, 0.0)
    t_inv = _neumann_inverse(-a_mat, int(math.log2(c_len)))
    uw = _mm1(t_inv, jnp.concatenate([v_h * bcol, kb * e_gc], axis=1))
    s0 = sbd_ref[...]
    v_new = uw[:, :MIX_W] - _mm1(uw[:, MIX_W:], s0)
    o = _mm1(q_h * e_gc, s0) + _mm1(att, v_new)
    sbd_ref[...] = s0 * g_last + _mm1(k_h * e_tail, v_new, _TN)

    o = _fold4(o, c_len)
    ms = _mm_mask_r(o * o, seg) * (1.0 / HD)
    o = o * lax.rsqrt(ms + NORM_EPS) * ng_ref[...]
    o_ref[...] = o * (z * jax.nn.sigmoid(z))


def _gdn_kernel(*refs, c_len, bb):
    _three_phase(functools.partial(_gdn_seq, c_len=c_len), refs, 3, 7, bb)


def _gdn(h_b, conv_prev8, s0, cw, alog, dtb, ng, c_len, bb):
    bn, seq, _ = h_b.shape
    nc = seq // c_len
    params = [cw, alog, dtb, ng]
    full = lambda a: pl.BlockSpec(a.shape, lambda b, c: (0,) * a.ndim)
    return pl.pallas_call(
        functools.partial(_gdn_kernel, c_len=c_len, bb=bb),
        grid=(bn // bb, nc),
        in_specs=[pl.BlockSpec((bb, c_len, B_COLS_PAD), lambda b, c: (b, c, 0)),
                  pl.BlockSpec((bb, 8, B_QKV), lambda b, c: (b, 0, 0)),
                  pl.BlockSpec((bb, N_HEADS, HD, HD), lambda b, c: (b, 0, 0, 0))] + [full(a) for a in params],
        out_specs=[pl.BlockSpec((bb, c_len, MIX_W), lambda b, c: (b, c, 0)),
                   pl.BlockSpec((bb, N_HEADS, HD, HD), lambda b, c: (b, 0, 0, 0))],
        out_shape=[jax.ShapeDtypeStruct((bn, seq, MIX_W), F32),
                   jax.ShapeDtypeStruct((bn, N_HEADS, HD, HD), F32)],
        scratch_shapes=[pltpu.VMEM((bb, MIX_W, MIX_W), F32), pltpu.VMEM((bb, 8 + c_len, B_QKV), F32)],
        compiler_params=pltpu.CompilerParams(dimension_semantics=("parallel", "arbitrary"),
                                             vmem_limit_bytes=V7X_VMEM_LIMIT),
        name="gdn",
    )(h_b, conv_prev8, s0, *params)


def _attn_prompt_kernel(lam_ref, q_ref, k_ref, v_ref, bias_ref, sg_ref, o_ref, q2_ref, m_ref, l_ref, acc_ref,
                        *, tq, scale, out_scale):
    i = pl.program_id(1)
    j = pl.program_id(2)

    @pl.when(j == 0)
    def _():
        m_ref[...] = jnp.full_like(m_ref, -jnp.inf)
        l_ref[...] = jnp.zeros_like(l_ref)
        acc_ref[...] = jnp.zeros_like(acc_ref)
        lane = _iota((tq, 2 * HD), 1)
        for hh in range(N_HEADS):
            qh = q_ref[:, hh * 2 * HD:(hh + 1) * 2 * HD] * scale
            q2_ref[hh, 0:tq, :] = jnp.where(lane < HD, qh, 0.0).astype(BF16)
            q2_ref[hh, tq:2 * tq, :] = jnp.where(lane >= HD, qh, 0.0).astype(BF16)

    @pl.when(j <= i)
    def _():
        for hh in range(N_HEADS):
            sl = slice(hh * 2 * HD, (hh + 1) * 2 * HD)
            s = lax.dot_general(q2_ref[hh], k_ref[:, sl].astype(BF16), _NT, preferred_element_type=F32)
            s = s + bias_ref[hh, jnp.minimum(i - j, 2)]
            m_old = m_ref[hh]
            m_new = jnp.maximum(m_old, jnp.max(s, axis=-1, keepdims=True))
            alpha = jnp.exp(m_old - m_new)
            p = jnp.exp(s - jnp.concatenate([m_new] * (tq // 128), axis=1))
            l_ref[hh] = alpha * l_ref[hh] + jnp.sum(p, axis=-1, keepdims=True)
            acc_ref[hh] = alpha * acc_ref[hh] + jnp.dot(p.astype(BF16), v_ref[:, sl].astype(BF16),
                                                        preferred_element_type=F32)
            m_ref[hh] = m_new

    @pl.when(j == i)
    def _():
        lam = lam_ref[0]
        for hh in range(N_HEADS):
            o = acc_ref[hh] / l_ref[hh]
            od = o[:tq] - lam * o[tq:]
            od = od * lax.rsqrt(jnp.mean(od * od, axis=-1, keepdims=True) + NORM_EPS) * sg_ref[...] * out_scale
            o_ref[:, hh * 2 * HD:(hh + 1) * 2 * HD] = od


def _attn_prompt(q, k, v, bias_near, lam, subln_g, out_scale, tq=256):
    bn, seq, _ = q.shape
    nq = seq // tq
    kv_spec = pl.BlockSpec((None, tq, C_W), lambda b, i, j: (b, jnp.minimum(i, j), 0))
    return pl.pallas_call(
        functools.partial(_attn_prompt_kernel, tq=tq, scale=HD ** -0.5, out_scale=out_scale),
        grid=(bn, nq, nq),
        in_specs=[pl.BlockSpec(memory_space=pltpu.SMEM),
                  pl.BlockSpec((None, tq, C_W), lambda b, i, j: (b, i, 0)),
                  kv_spec, kv_spec,
                  pl.BlockSpec(bias_near.shape, lambda b, i, j: (0, 0, 0, 0)),
                  pl.BlockSpec(subln_g.shape, lambda b, i, j: (0, 0))],
        out_specs=pl.BlockSpec((None, tq, C_W), lambda b, i, j: (b, i, 0)),
        out_shape=jax.ShapeDtypeStruct((bn, seq, C_W), F32),
        scratch_shapes=[pltpu.VMEM((N_HEADS, 2 * tq, 2 * HD), BF16), pltpu.VMEM((N_HEADS, 2 * tq, 2 * HD), F32),
                        pltpu.VMEM((N_HEADS, 2 * tq, 2 * HD), F32), pltpu.VMEM((N_HEADS, 2 * tq, 2 * HD), F32)],
        compiler_params=pltpu.CompilerParams(dimension_semantics=("parallel", "parallel", "arbitrary"),
                                             vmem_limit_bytes=V7X_VMEM_LIMIT),
        name="attn_prompt",
    )(lam, q, k, v, bias_near, subln_g)


def _attn_sample_kernel(pt_ref, lam_ref, q_ref, kn_ref, vn_ref, *rest, lq, scale, out_scale, ppb):
    del pt_ref
    kp_refs = rest[:ppb]
    vp_refs = rest[ppb:2 * ppb]
    bp_ref, bn_ref, sg_ref, o_ref, qz_ref, m_ref, l_ref, acc_ref = rest[2 * ppb:]
    p_idx = pl.program_id(1)

    @pl.when(p_idx == 0)
    def _():
        lane = _iota((lq, 2 * HD), 1)
        pieces = []
        for hh in range(N_HEADS):
            qh = q_ref[:, hh * 2 * HD:(hh + 1) * 2 * HD]
            pieces += [jnp.where(lane < HD, qh, 0.0), jnp.where(lane >= HD, qh, 0.0)]
        qz = jnp.concatenate(pieces, axis=0)
        qz_ref[...] = qz
        s = _mm_f32(qz, kn_ref[...], _NT) * scale + bn_ref[...]
        m = jnp.max(s, axis=-1, keepdims=True)
        p = jnp.exp(s - m)
        m_ref[...] = m
        l_ref[...] = jnp.sum(p, axis=-1, keepdims=True)
        acc_ref[...] = _mm_f32(p, vn_ref[...])

    qz = qz_ref[...].astype(BF16)
    cols = bp_ref.shape[1] // ppb
    scores = []
    for u in range(ppb):
        s = lax.dot_general(qz, kp_refs[u][...].astype(BF16), _NT, preferred_element_type=F32)
        scores.append(s * scale + bp_ref[:, u * cols:(u + 1) * cols])
    m_old = m_ref[...]
    m_new = m_old
    for s in scores:
        m_new = jnp.maximum(m_new, jnp.max(s, axis=-1, keepdims=True))
    alpha = jnp.exp(m_old - m_new)
    l_new = alpha * l_ref[...]
    acc = alpha * acc_ref[...]
    for u in range(ppb):
        p = jnp.exp(scores[u] - m_new)
        l_new = l_new + jnp.sum(p, axis=-1, keepdims=True)
        acc = acc + jnp.dot(p.astype(BF16), vp_refs[u][...].astype(BF16), preferred_element_type=F32)
    l_ref[...] = l_new
    acc_ref[...] = acc
    m_ref[...] = m_new

    @pl.when(p_idx == pl.num_programs(1) - 1)
    def _():
        lam = lam_ref[0]
        o = acc_ref[...] / l_ref[...]
        for hh in range(N_HEADS):
            o1 = o[(2 * hh) * lq:(2 * hh + 1) * lq]
            o2 = o[(2 * hh + 1) * lq:(2 * hh + 2) * lq]
            od = o1 - lam * o2
            od = od * lax.rsqrt(jnp.mean(od * od, axis=-1, keepdims=True) + NORM_EPS) * sg_ref[...] * out_scale
            o_ref[:, hh * 2 * HD:(hh + 1) * 2 * HD] = od


def _attn_sample(q, k_new, v_new, cache_k, cache_v, page_table, layer, bias_past, bias_new, lam, subln_g,
                 out_scale, ppb=8):
    bn, lq, _ = q.shape
    n_pages = page_table.shape[1]
    prow = cache_k.shape[2]
    rows = 2 * N_HEADS * lq
    seq_spec = pl.BlockSpec((None, lq, C_W), lambda b, p, pt: (b, 0, 0))
    new_spec = pl.BlockSpec((None, lq * N_HEADS, 2 * HD), lambda b, p, pt: (b, 0, 0))

    def page_spec(u):
        return pl.BlockSpec((None, None, prow, 2 * HD), lambda b, p, pt: (pt[b, p * ppb + u], layer, 0, 0))

    grid_spec = pltpu.PrefetchScalarGridSpec(
        num_scalar_prefetch=1,
        grid=(bn, n_pages // ppb),
        in_specs=[pl.BlockSpec(memory_space=pltpu.SMEM), seq_spec, new_spec, new_spec]
                 + [page_spec(u) for u in range(ppb)] * 2
                 + [pl.BlockSpec((rows, ppb * prow), lambda b, p, pt: (0, p)),
                    pl.BlockSpec(bias_new.shape, lambda b, p, pt: (0, 0)),
                    pl.BlockSpec(subln_g.shape, lambda b, p, pt: (0, 0))],
        out_specs=seq_spec,
        scratch_shapes=[pltpu.VMEM((rows, 2 * HD), F32), pltpu.VMEM((rows, 1), F32), pltpu.VMEM((rows, 1), F32),
                        pltpu.VMEM((rows, 2 * HD), F32)],
    )
    return pl.pallas_call(
        functools.partial(_attn_sample_kernel, lq=lq, scale=HD ** -0.5, out_scale=out_scale, ppb=ppb),
        grid_spec=grid_spec,
        out_shape=jax.ShapeDtypeStruct((bn, lq, C_W), F32),
        compiler_params=pltpu.CompilerParams(dimension_semantics=("parallel", "arbitrary")),
        name="attn_sample",
    )(page_table, lam, q, k_new, v_new, *([cache_k] * ppb), *([cache_v] * ppb), bias_past, bias_new, subln_g)


def _proj_out_kernel(x_ref, oa_ref, ob_ref, oc_ref, wo_ref, g_ref, wq_ref, xo_ref, xnt_ref, q_ref):
    dot = lambda a, w: jnp.dot(a.astype(BF16), w, preferred_element_type=F32)
    x = x_ref[...]
    x = x + dot(oa_ref[...], wo_ref[0:MIX_W, :]) + dot(ob_ref[...], wo_ref[MIX_W:2 * MIX_W, :]) \
        + dot(oc_ref[...], wo_ref[2 * MIX_W:, :])
    xo_ref[...] = x
    xn = x * lax.rsqrt(jnp.mean(x * x, axis=-1, keepdims=True) + NORM_EPS) * g_ref[...]
    xnt_ref[...] = jnp.transpose(xn).astype(BF16)
    q_ref[...] = jnp.dot(xn.astype(BF16), wq_ref[...], preferred_element_type=F32)


def _proj_out(x2d, o_a, o_b, o_c, w_out, g, wq, tm=256):
    t = x2d.shape[0]
    nq = wq.shape[1]
    row = lambda w: pl.BlockSpec((tm, w), lambda i: (i, 0))
    full = lambda a: pl.BlockSpec(a.shape, lambda i: (0,) * a.ndim)
    return pl.pallas_call(
        _proj_out_kernel,
        grid=(t // tm,),
        in_specs=[row(D_MODEL), row(MIX_W), row(MIX_W), row(C_W), full(w_out), full(g), full(wq)],
        out_specs=[row(D_MODEL), pl.BlockSpec((D_MODEL, tm), lambda i: (0, i)), row(nq)],
        out_shape=[jax.ShapeDtypeStruct((t, D_MODEL), F32), jax.ShapeDtypeStruct((D_MODEL, t), BF16),
                   jax.ShapeDtypeStruct((t, nq), F32)],
        compiler_params=pltpu.CompilerParams(dimension_semantics=("parallel",),
                                             vmem_limit_bytes=V7X_VMEM_LIMIT),
        name="proj_out",
    )(x2d, o_a, o_b, o_c, w_out, g, wq)


def _top16(s, rid=None):
    n_rows, tm = s.shape
    if rid is None:
        rid = _iota((n_rows, tm), 0).astype(F32)
    kid = _iota((PK_TOPK, tm), 0)
    vals = jnp.zeros((PK_TOPK, tm), F32)
    idxs = jnp.zeros((PK_TOPK, tm), F32)
    for kk in range(PK_TOPK):
        m = jnp.max(s, axis=0, keepdims=True)
        ix = jnp.min(jnp.where(s == m, rid, float(1 << 20)), axis=0, keepdims=True)
        vals = jnp.where(kid == kk, m, vals)
        idxs = jnp.where(kid == kk, ix, idxs)
        s = jnp.where(rid == ix, -jnp.inf, s)
    return vals, idxs


def _peer_route_kernel(q_ref, keys_ref, e1_ref, f_ref, e2_ref, r_ref):
    tm = q_ref.shape[0]
    q = q_ref[...]
    s1 = _mm1(keys_ref[0], q[:, :PK_NKEYS], _NT)
    s2 = _mm1(keys_ref[1], q[:, PK_NKEYS:], _NT)
    sv1, si1 = _top16(s1)
    sv2, si2 = _top16(s2)
    lo_a, lo_b = 4, 3
    row16 = _iota((PK_TOPK, tm), 0)
    rows_a = [sv1[a:a + 1, :] + sv2 for a in range(lo_a)]
    rows_b = [jnp.where(row16 >= lo_a, sv1 + sv2[b:b + 1, :], -jnp.inf) for b in range(lo_b)]
    cand = jnp.concatenate(rows_a + rows_b, axis=0)
    n_a = lo_a * PK_TOPK
    rid_a = _iota((n_a, tm), 0)
    row_b = _iota((lo_b * PK_TOPK, tm), 0)
    rid_b = jnp.bitwise_and(row_b, PK_TOPK - 1) * PK_TOPK + jnp.right_shift(row_b, 4)
    tv, ti = _top16(cand, jnp.concatenate([rid_a, rid_b], axis=0).astype(F32))
    rank_a = jnp.floor(ti * (1.0 / PK_TOPK))
    z = jnp.sum(jnp.exp(tv - tv[0:1, :]), axis=0, keepdims=True)
    aid = _iota((PK_TOPK, tm), 0).astype(F32)
    cnt = jnp.zeros((PK_TOPK, tm), F32)
    for kk in range(PK_TOPK):
        cnt = cnt + (aid == rank_a[kk:kk + 1, :]).astype(F32)
    ex1 = jnp.exp(sv1 - sv1[0:1, :]) / z
    ex2 = jnp.exp(sv2 - sv2[0:1, :])
    kid = _iota((PK_NKEYS, tm), 0).astype(F32)
    e1 = jnp.zeros((PK_NKEYS, tm), F32)
    f = jnp.zeros((PK_NKEYS, tm), F32)
    e2 = jnp.zeros((PK_NKEYS, tm), F32)
    r = jnp.full((PK_NKEYS, tm), float(PK_TOPK), F32)
    for kk in range(PK_TOPK):
        hit1 = kid == si1[kk:kk + 1, :]
        e1 = jnp.where(hit1, ex1[kk:kk + 1, :], e1)
        f = jnp.where(hit1, cnt[kk:kk + 1, :], f)
        hit2 = kid == si2[kk:kk + 1, :]
        e2 = jnp.where(hit2, ex2[kk:kk + 1, :], e2)
        r = jnp.where(hit2, float(kk), r)
    e1_ref[...] = e1
    f_ref[...] = f
    e2_ref[...] = e2.astype(BF16)
    r_ref[...] = r.astype(BF16)


def _peer_route(q, keys, tm=256):
    t = q.shape[0]
    tab = jax.ShapeDtypeStruct((PK_HEADS, PK_NKEYS, t), F32)
    tab16 = jax.ShapeDtypeStruct((PK_HEADS, PK_NKEYS, t), BF16)
    tab_spec = pl.BlockSpec((None, PK_NKEYS, tm), lambda i, h: (h, 0, i))
    return pl.pallas_call(
        _peer_route_kernel,
        grid=(t // tm, PK_HEADS),
        in_specs=[pl.BlockSpec((tm, 2 * PK_NKEYS), lambda i, h: (i, h)),
                  pl.BlockSpec((None, 2, PK_NKEYS, PK_NKEYS), lambda i, h: (h, 0, 0, 0))],
        out_specs=[tab_spec] * 4,
        out_shape=[tab, tab, tab16, tab16],
        compiler_params=pltpu.CompilerParams(dimension_semantics=("parallel", "parallel")),
        name="peer_route",
    )(q, keys)


def _peer_dense_kernel(x_ref, xnt_ref, u_ref, vt_ref, e1_ref, f_ref, e2_ref, r_ref, o_ref, acc_ref, act_ref, c_ref,
                       *, ib, tm):
    ii = pl.program_id(1)

    @pl.when(ii == 0)
    def _():
        acc_ref[...] = jnp.zeros_like(acc_ref)

    pk = 16
    ng = PK_NKEYS // pk
    group = 2
    n_parts = ib // group
    n_cols = tm // 128
    half = tm // 2
    tile = lambda ref, hh, s, lanes: jnp.broadcast_to(ref[hh, s:s + 1, lanes], (pk, 128)).astype(BF16)

    def activations(part, n):
        rows = slice(part * group * PK_NKEYS, (part + 1) * group * PK_NKEYS)
        cols = slice(n * half, (n + 1) * half)
        act = jnp.dot(u_ref[rows, :], xnt_ref[:, cols], preferred_element_type=F32)
        act_ref[part * group * ng:(part + 1) * group * ng, :, cols] = act.reshape(group * ng, pk, half)

    def gates_of(part, c):
        lanes = slice(c * 128, (c + 1) * 128)
        s0 = part * group
        gates = [None] * group
        for hh in range(PK_HEADS):
            rank = r_ref[hh, :, :, lanes]
            e2 = e2_ref[hh, :, :, lanes]
            for ds in range(group):
                e1 = tile(e1_ref, hh, s0 + ds, lanes)
                f = tile(f_ref, hh, s0 + ds, lanes)
                term = jnp.maximum(jnp.minimum(f - rank, e1), 0.0) * e2
                gates[ds] = term if gates[ds] is None else gates[ds] + term
        for ds in range(group):
            rows = slice((s0 + ds) * ng, (s0 + ds + 1) * ng)
            a = act_ref[rows, :, lanes]
            a = 0.5 * a * (1.0 + lax.erf(a * math.sqrt(0.5)))
            c_ref[rows, :, lanes] = (gates[ds].astype(F32) * a).astype(BF16)

    activations(0, 0)
    activations(0, 1)
    for part in range(n_parts):
        for c in range(n_cols):
            if part + 1 < n_parts and c < 2:
                activations(part + 1, c)
            gates_of(part, c)
    acc_ref[...] += jnp.dot(vt_ref[...], c_ref[...].reshape(ib * PK_NKEYS, tm), preferred_element_type=F32)

    @pl.when(ii == pl.num_programs(1) - 1)
    def _():
        o_ref[...] = x_ref[...] + jnp.transpose(acc_ref[...])


def _peer_dense(x2d, xnt, u_bf, vt_bf, e1, f, e2, r, tm=512, ib=8):
    t = x2d.shape[0]
    ni = PK_NKEYS // ib
    tab_i = pl.BlockSpec((PK_HEADS, ib, tm), lambda tt, ii: (0, ii, tt))
    pk = 16
    ng = PK_NKEYS // pk
    tab_j = pl.BlockSpec((PK_HEADS, ng, pk, tm), lambda tt, ii: (0, 0, 0, tt))
    e2 = e2.reshape(PK_HEADS, ng, pk, t)
    r = r.reshape(PK_HEADS, ng, pk, t)
    return pl.pallas_call(
        functools.partial(_peer_dense_kernel, ib=ib, tm=tm),
        grid=(t // tm, ni),
        in_specs=[pl.BlockSpec((tm, D_MODEL), lambda tt, ii: (tt, 0)),
                  pl.BlockSpec((D_MODEL, tm), lambda tt, ii: (0, tt)),
                  pl.BlockSpec((ib * PK_NKEYS, D_MODEL), lambda tt, ii: (ii, 0)),
                  pl.BlockSpec((D_MODEL, ib * PK_NKEYS), lambda tt, ii: (0, ii)),
                  tab_i, tab_i, tab_j, tab_j],
        out_specs=pl.BlockSpec((tm, D_MODEL), lambda tt, ii: (tt, 0)),
        out_shape=jax.ShapeDtypeStruct((t, D_MODEL), F32),
        scratch_shapes=[pltpu.VMEM((D_MODEL, tm), F32), pltpu.VMEM((ib * ng, pk, tm), F32),
                        pltpu.VMEM((ib * ng, pk, tm), BF16)],
        compiler_params=pltpu.CompilerParams(dimension_semantics=("parallel", "arbitrary"),
                                             vmem_limit_bytes=V7X_VMEM_LIMIT),
        name="peer_dense",
    )(x2d, xnt, u_bf, vt_bf, e1, f, e2, r)


def _t5_bucket(rel):
    n = jnp.maximum(-rel, 0)
    max_exact = REL_BUCKETS // 2
    nf = jnp.maximum(n, 1).astype(F32)
    large = max_exact + (jnp.log(nf / max_exact) / math.log(REL_MAX_DIST / max_exact)
                         * (REL_BUCKETS - max_exact)).astype(jnp.int32)
    large = jnp.minimum(large, REL_BUCKETS - 1)
    return jnp.where(n < max_exact, n, large)


def _bias_table(rel_bias, rel):
    bucket = _t5_bucket(rel)[None]
    col = lambda b: rel_bias[b].astype(F32).reshape((-1,) + (1,) * rel.ndim)
    out = jnp.broadcast_to(col(0), (rel_bias.shape[1],) + rel.shape)
    for b in range(1, REL_BUCKETS):
        out = jnp.where(bucket == b, col(b), out)
    return out


def _prompt_bias_tiles(rel_bias, tq):
    rel = jnp.arange(tq)[None, :] - jnp.arange(tq)[:, None]
    tiles = []
    for d in range(3):
        b = _bias_table(rel_bias, rel - d * tq)
        if d == 0:
            b = jnp.where(rel <= 0, b, -jnp.inf)
        tiles.append(jnp.concatenate([b, b], axis=1))
    return jnp.stack(tiles, axis=1)


def _prep_layer(l, W):
    row = lambda a: a.reshape(1, -1).astype(F32)
    w_in = W["w_in"][l]
    o_b = A_COLS
    o_c = A_COLS + 4 * MIX_W + 2 * N_HEADS
    w_cat = jnp.concatenate([
        w_in[:, :A_COLS],
        w_in[:, o_b:o_b + B_QKV],
        w_in[:, o_b + B_QKV + 2 * N_HEADS:o_c],
        w_in[:, o_b + B_QKV:o_b + B_QKV + 2 * N_HEADS],
        jnp.zeros((D_MODEL, 128 - 2 * N_HEADS), F32),
        w_in[:, o_c:],
    ], axis=1).astype(BF16)
    pad_lane = lambda a, off: jnp.zeros((1, 128), F32).at[0, off:off + a.shape[0]].set(a)
    alog = pad_lane(W["dn_a_log"][l], 0)
    dtb = pad_lane(W["dn_dt_bias"][l], 0)
    f32 = F32
    lam_init = 0.8 - 0.6 * math.exp(-0.3 * l)
    lam = (jnp.exp(jnp.sum(W["df_lq1"][l].astype(f32) * W["df_lk1"][l].astype(f32)))
           - jnp.exp(jnp.sum(W["df_lq2"][l].astype(f32) * W["df_lk2"][l].astype(f32))) + lam_init)
    return dict(
        rms_mix_g=row(W["rms_mix_g"][l]), w_cat=w_cat,
        qg=row(jnp.tile(W["df_qn_g"][l].reshape(-1), N_HEADS)),
        kg=row(jnp.tile(W["df_kn_g"][l].reshape(-1), N_HEADS)),
        rw_mu=row(W["rw_mu"][l]), rw_w0=row(W["rw_w0"][l]), rw_w2=W["rw_w2"][l], rw_a0=row(W["rw_a0"][l]),
        rw_a2=W["rw_a2"][l], rw_g2=W["rw_g2"][l], rw_kk=row(W["rw_kk"][l]), rw_ka=row(W["rw_ka"][l]),
        rw_rk=row(W["rw_rk"][l]), rw_ln_g=row(W["rw_ln_g"][l]), rw_ln_b=row(W["rw_ln_b"][l]),
        dn_conv_w=W["dn_conv_w"][l], dn_alog=alog, dn_dtb=dtb,
        dn_norm_g=row(jnp.tile(W["dn_norm_g"][l], N_HEADS)),
        lam=lam.reshape(1).astype(F32), lam_init=lam_init, subln_g=row(W["df_subln_g"][l]),
        w_out=W["w_out"][l].astype(BF16), rms_ffn_g=row(W["rms_ffn_g"][l]),
        pk_wq=W["pk_wq"][l].astype(BF16),
        pk_keys=W["pk_keys"][l], pk_u=W["pk_u"][l].astype(BF16), pk_vt=jnp.transpose(W["pk_v"][l]).astype(BF16),
    )


def _layer(x, shift_prev, s_rwkv, conv_prev, s_dn, attn_fn, P, c_len, bb):
    bn, seq, _ = x.shape
    t = bn * seq
    x2d = x.reshape(t, D_MODEL)
    h_a, h_b, q, k, v = _proj_in(x2d, P["rms_mix_g"], P["w_cat"], P["qg"], P["kg"])
    h_a3 = h_a.reshape(bn, seq, A_COLS)
    h_b3 = h_b.reshape(bn, seq, B_COLS_PAD)
    o_a, s_rwkv_new = _rwkv(h_a3, shift_prev, s_rwkv, P, c_len, bb)
    conv8 = jnp.concatenate([jnp.zeros((bn, 5, B_QKV), F32), conv_prev], axis=1)
    o_b, s_dn_new = _gdn(h_b3, conv8, s_dn, P["dn_conv_w"], P["dn_alog"], P["dn_dtb"], P["dn_norm_g"], c_len,
                         bb)
    q3, k3, v3 = (a.reshape(bn, seq, C_W) for a in (q, k, v))
    o_c = attn_fn(q3, k3, v3)
    x_new, xnt, pq = _proj_out(x2d, o_a.reshape(t, MIX_W), o_b.reshape(t, MIX_W), o_c.reshape(t, C_W),
                               P["w_out"], P["rms_ffn_g"], P["pk_wq"])
    e1, f, e2, r = _peer_route(pq, P["pk_keys"])
    y = _peer_dense(x_new, xnt, P["pk_u"], P["pk_vt"], e1, f, e2, r)
    k_rows = k.reshape(bn, seq, N_HEADS, 2 * HD)
    v_rows = v.reshape(bn, seq, N_HEADS, 2 * HD)
    shift_new = h_a3[:, -1]
    conv_new = h_b3[:, -3:, :B_QKV]
    return y.reshape(bn, seq, D_MODEL), k_rows, v_rows, s_rwkv_new, shift_new, s_dn_new, conv_new


def kernel(x_prompt, x_sample, cache_k, cache_v, state_rwkv, state_rwkv_shift, state_dn, state_dn_conv, page_table, rms_mix_g, w_in, w_out, rw_mu, rw_w0, rw_w2, rw_a0, rw_a2, rw_g2, rw_kk, rw_ka, rw_rk, rw_ln_g, rw_ln_b, dn_conv_w, dn_a_log, dn_dt_bias, dn_norm_g, df_qn_g, df_kn_g, df_lq1, df_lk1, df_lq2, df_lk2, df_subln_g, rel_bias, rms_ffn_g, pk_wq, pk_keys, pk_u, pk_v):
    W = dict(rms_mix_g=rms_mix_g, w_in=w_in, w_out=w_out, rw_mu=rw_mu, rw_w0=rw_w0, rw_w2=rw_w2, rw_a0=rw_a0,
             rw_a2=rw_a2, rw_g2=rw_g2, rw_kk=rw_kk, rw_ka=rw_ka, rw_rk=rw_rk, rw_ln_g=rw_ln_g, rw_ln_b=rw_ln_b,
             dn_conv_w=dn_conv_w, dn_a_log=dn_a_log, dn_dt_bias=dn_dt_bias, dn_norm_g=dn_norm_g,
             df_qn_g=df_qn_g, df_kn_g=df_kn_g, df_lq1=df_lq1, df_lk1=df_lk1, df_lq2=df_lq2, df_lk2=df_lk2,
             df_subln_g=df_subln_g, rms_ffn_g=rms_ffn_g, pk_wq=pk_wq, pk_keys=pk_keys, pk_u=pk_u, pk_v=pk_v)
    depth = w_in.shape[0]
    bp, seq_p, _ = x_prompt.shape
    bs, seq_s, _ = x_sample.shape
    n_pages = page_table.shape[1]
    page = cache_k.shape[2]
    past = n_pages * page
    tq = 256
    ck = cache_k.reshape(cache_k.shape[0], depth, page * N_HEADS, 2 * HD)
    cv = cache_v.reshape(cache_v.shape[0], depth, page * N_HEADS, 2 * HD)

    bias_near = _prompt_bias_tiles(rel_bias, tq)
    q_pos = past + jnp.arange(seq_s)
    same_head = jnp.eye(N_HEADS, dtype=bool)[:, None, None, :]

    def expand(b):
        bx = jnp.where(same_head, b[..., None], -jnp.inf)
        return jnp.broadcast_to(bx[:, None], (N_HEADS, 2) + bx.shape[1:]).reshape(2 * N_HEADS * seq_s, -1)

    bias_past = expand(_bias_table(rel_bias, jnp.arange(past)[None, :] - q_pos[:, None]))
    rel_new = q_pos[None, :] - q_pos[:, None]
    bias_new = expand(jnp.where(rel_new <= 0, _bias_table(rel_bias, rel_new), -jnp.inf))

    xp, xs = x_prompt, x_sample
    outs = [[] for _ in range(12)]
    zeros = lambda *s: jnp.zeros(s, F32)
    for l in range(depth):
        P = _prep_layer(l, W)
        out_scale = 1.0 - P["lam_init"]
        attn_p = lambda q, k, v: _attn_prompt(q, k, v, bias_near, P["lam"], P["subln_g"], out_scale, tq)
        by_head = lambda a: a.reshape(bs, seq_s * N_HEADS, 2 * HD)
        attn_s = lambda q, k, v: _attn_sample(q, by_head(k), by_head(v), ck, cv, page_table, l, bias_past, bias_new,
                                              P["lam"], P["subln_g"], out_scale)
        xp, kp, vp, sap, shp, sbp, cvp = _layer(
            xp, zeros(bp, A_COLS), zeros(bp, N_HEADS, HD, HD), zeros(bp, 3, B_QKV), zeros(bp, N_HEADS, HD, HD),
            attn_p, P, 64, 2)
        xs, ksn, vsn, sas, shs, sbs, cvs = _layer(
            xs, state_rwkv_shift[l], state_rwkv[l], state_dn_conv[l], state_dn[l], attn_s, P, seq_s, 8)
        for lst, val in zip(outs, (kp, vp, ksn, vsn, sap, sas, shp, shs, sbp, sbs, cvp, cvs)):
            lst.append(val)
    stack = lambda i, ax: jnp.stack(outs[i], axis=ax)
    return (xp, xs, stack(0, 1), stack(1, 1), stack(2, 1), stack(3, 1),
            stack(4, 0), stack(5, 0), stack(6, 0), stack(7, 0), stack(8, 0), stack(9, 0), stack(10, 0), stack(11, 0))
```

```python
import functools
import math

import jax
import jax.numpy as jnp
from jax import lax
from jax.experimental import pallas as pl
from jax.experimental.pallas import tpu as pltpu

F32 = jnp.float32
BF16 = jnp.bfloat16

D_MODEL = 1024
N_HEADS = 4
HD = 64
MIX_W = N_HEADS * HD
A_COLS = 1024
B_QKV = 3 * MIX_W
B_COLS_PAD = 1152
C_W = 512
NORM_EPS = 1e-6
A_LN_EPS = 64e-5
REL_BUCKETS = 32
REL_MAX_DIST = 128
PK_HEADS = 8
PK_NKEYS = 128
PK_TOPK = 16
V7X_VMEM_LIMIT = 56 * 1024 * 1024


def _parts(a, n):
    out = []
    r = a
    for i in range(n):
        p = r.astype(BF16)
        out.append(p)
        if i + 1 < n:
            r = r - p.astype(F32)
    return out


_NN = (((1,), (0,)), ((), ()))
_NT = (((1,), (1,)), ((), ()))
_TN = (((0,), (0,)), ((), ()))


def _mm(a, b, dims=_NN, na=2, nb=2, order=1):
    ap = _parts(a, na)
    bp = _parts(b, nb)
    acc = None
    for s in range(order, -1, -1):
        for i in range(na):
            j = s - i
            if 0 <= j < nb:
                t = lax.dot_general(ap[i], bp[j], dims, preferred_element_type=F32)
                acc = t if acc is None else acc + t
    return acc


def _mm1(a, b, dims=_NN):
    return _mm(a, b, dims, na=1, nb=1, order=0)


def _mm_f32(a, b, dims=_NN):
    return lax.dot_general(a, b, dims, precision=lax.Precision.HIGHEST, preferred_element_type=F32)


def _mm_mask_r(a, mask, dims=_NN):
    return _mm(a, mask, dims, na=2, nb=1, order=1)


def _mm_mask_l(mask, b, dims=_NN):
    return _mm(mask, b, dims, na=1, nb=3, order=2)


def _iota(shape, dim):
    return lax.broadcasted_iota(jnp.int32, shape, dim)


def _seg_ones(n, seg_shift):
    r = jnp.right_shift(_iota((n, n), 0), seg_shift)
    c = jnp.right_shift(_iota((n, n), 1), seg_shift)
    return (r == c).astype(F32)


def _tile4(z):
    return jnp.concatenate([z, z, z, z], axis=0)


def _fold4(y, c):
    return y[0:c] + y[c:2 * c] + y[2 * c:3 * c] + y[3 * c:4 * c]


def _neumann_inverse(n, steps):
    rows = n.shape[0]
    eye = (_iota((rows, rows), 0) == _iota((rows, rows), 1)).astype(F32)
    t = eye + n
    p = n
    for _ in range(steps - 1):
        p = _mm1(p, p)
        t = t + _mm1(t, p)
    return t


def _softplus(y):
    return jnp.maximum(y, 0.0) + jnp.log1p(jnp.exp(-jnp.abs(y)))


def _chunk_masks(c_len):
    r = 4 * c_len
    shift = int(math.log2(c_len))
    ri = _iota((r, r), 0)
    ci = _iota((r, r), 1)
    same = jnp.right_shift(ri, shift) == jnp.right_shift(ci, shift)
    incl = jnp.logical_and(same, ci <= ri)
    strict = jnp.logical_and(same, ci < ri)
    head = jnp.right_shift(_iota((r, MIX_W), 0), shift) == jnp.right_shift(_iota((r, MIX_W), 1), 6)
    return incl, strict, head


def _proj_in_kernel(x_ref, g_ref, w_ref, qg_ref, kg_ref, ha_ref, hb_ref, q_ref, k_ref, v_ref):
    x = x_ref[...]
    xn = x * lax.rsqrt(jnp.mean(x * x, axis=-1, keepdims=True) + NORM_EPS) * g_ref[...]
    h = jnp.dot(xn.astype(BF16), w_ref[...], preferred_element_type=F32)
    o = A_COLS
    ha_ref[...] = h[:, :o]
    hb_ref[...] = h[:, o:o + B_COLS_PAD]
    o += B_COLS_PAD
    seg = _seg_ones(C_W, 6)

    def qk_norm(t, g):
        ms = _mm_mask_r(t * t, seg) * (1.0 / HD)
        return t * lax.rsqrt(ms + NORM_EPS) * g

    q_ref[...] = qk_norm(h[:, o:o + C_W], qg_ref[...])
    k_ref[...] = qk_norm(h[:, o + C_W:o + 2 * C_W], kg_ref[...])
    v_ref[...] = h[:, o + 2 * C_W:o + 3 * C_W]


def _proj_in(x2d, g, w_cat, qg, kg, tm=256):
    t = x2d.shape[0]
    n = w_cat.shape[1]
    row = lambda w: pl.BlockSpec((tm, w), lambda i: (i, 0))
    full = lambda a: pl.BlockSpec(a.shape, lambda i: (0,) * a.ndim)
    return pl.pallas_call(
        _proj_in_kernel,
        grid=(t // tm,),
        in_specs=[row(D_MODEL), full(g), full(w_cat), full(qg), full(kg)],
        out_specs=[row(A_COLS), row(B_COLS_PAD), row(C_W), row(C_W), row(C_W)],
        out_shape=[jax.ShapeDtypeStruct((t, A_COLS), F32), jax.ShapeDtypeStruct((t, B_COLS_PAD), F32),
                   jax.ShapeDtypeStruct((t, C_W), F32), jax.ShapeDtypeStruct((t, C_W), F32),
                   jax.ShapeDtypeStruct((t, C_W), F32)],
        compiler_params=pltpu.CompilerParams(dimension_semantics=("parallel",),
                                             vmem_limit_bytes=V7X_VMEM_LIMIT),
        name="proj_in",
    )(x2d, g, w_cat, qg, kg)


def _three_phase(seq_fn, refs, n_shared_from, n_shared_to, bb):
    c = pl.program_id(1)

    def per_seq(bi):
        return [r if n_shared_from <= n < n_shared_to else r.at[bi] for n, r in enumerate(refs)]

    @pl.when(c == 0)
    def _():
        for bi in range(bb):
            seq_fn("init", *per_seq(bi))

    for bi in range(bb):
        seq_fn("chunk", *per_seq(bi))

    @pl.when(c == pl.num_programs(1) - 1)
    def _():
        for bi in range(bb):
            seq_fn("final", *per_seq(bi))


def _rwkv_seq(phase, h_ref, shift_ref, s0_ref, mu_ref, w0_ref, w2_ref, a0_ref, a2_ref, g2_ref, kk_ref, ka_ref,
              rk_ref, lng_ref, lnb_ref, o_ref, sout_ref, sbd_ref, buf_ref, *, c_len):
    r_rows = 4 * c_len

    if phase == "init":
        sbd_ref[...] = jnp.zeros_like(sbd_ref)
        for hh in range(N_HEADS):
            sbd_ref[hh * HD:(hh + 1) * HD, hh * HD:(hh + 1) * HD] = s0_ref[hh]
        buf_ref[0:8, :] = jnp.broadcast_to(shift_ref[...], (8, A_COLS))
        return
    if phase == "final":
        for hh in range(N_HEADS):
            sout_ref[hh] = sbd_ref[hh * HD:(hh + 1) * HD, hh * HD:(hh + 1) * HD]
        return

    x = h_ref[...]
    buf_ref[8:8 + c_len, :] = x
    prev = buf_ref[7:7 + c_len, :]
    buf_ref[0:8, :] = x[c_len - 8:c_len, :]

    hs = x + (prev - x) * mu_ref[...]
    r = hs[:, 0:MIX_W]
    k = hs[:, MIX_W:2 * MIX_W]
    v = hs[:, 2 * MIX_W:3 * MIX_W]
    xw = hs[:, 768:832]
    xa = hs[:, 832:896]
    xg = hs[:, 896:1024]
    w = -_softplus(-(w0_ref[...] + _mm(jnp.tanh(xw), w2_ref[...]))) - 0.5
    a = jax.nn.sigmoid(a0_ref[...] + _mm(xa, a2_ref[...]))
    g = _mm(jax.nn.sigmoid(xg), g2_ref[...])
    seg = _seg_ones(MIX_W, 6)
    kkv = k * kk_ref[...]
    kkn = kkv * lax.rsqrt(_mm_mask_r(kkv * kkv, seg) + 1e-6)
    k2 = k * (1.0 + (a - 1.0) * ka_ref[...])
    logd = -jnp.exp(w)
    tri = (_iota((c_len, c_len), 1) <= _iota((c_len, c_len), 0)).astype(F32)
    cum = _mm_mask_l(tri, logd)
    p_in = jnp.exp(cum)
    p_inv = jnp.exp(-cum)
    p_prev = jnp.exp(cum - logd)

    incl, strict, head = _chunk_masks(c_len)
    hm = head.astype(F32)
    a_t = _tile4(-kkn * p_prev) * hm
    b_t = _tile4(kkn * a * p_inv) * hm
    k_t = _tile4(k2 * p_inv) * hm
    r_t = _tile4(r * p_in) * hm
    v_t = _tile4(v) * hm

    s0 = sbd_ref[...]
    ar = jnp.concatenate([a_t, r_t], axis=0)
    bk = jnp.concatenate([b_t, k_t], axis=0)
    gram = _mm1(ar, bk, _NT)
    h0 = _mm1(ar, s0, _NT)
    l_ab = jnp.where(strict, gram[:r_rows, :r_rows], 0.0)
    l_ak = jnp.where(strict, gram[:r_rows, r_rows:], 0.0)
    m_rb = jnp.where(incl, gram[r_rows:, :r_rows], 0.0)
    m_rk = jnp.where(incl, gram[r_rows:, r_rows:], 0.0)
    t_inv = _neumann_inverse(l_ab, int(math.log2(c_len)))
    u = _mm1(t_inv, h0[:r_rows] + _mm1(l_ak, v_t))
    uv = jnp.concatenate([u, v_t], axis=0)
    y = h0[r_rows:] + _mm1(jnp.concatenate([m_rb, m_rk], axis=1), uv)
    sbd_ref[...] = (s0 + _mm1(uv, bk, _TN)) * p_in[c_len - 1:c_len, :]

    y = _fold4(y, c_len)
    mean = _mm_mask_r(y, seg) * (1.0 / HD)
    yc = y - mean
    var = _mm_mask_r(yc * yc, seg) * (1.0 / HD)
    yn = yc * lax.rsqrt(var + A_LN_EPS) * lng_ref[...] + lnb_ref[...]
    bonus = _mm_mask_r(r * k2 * rk_ref[...], seg)
    o_ref[...] = (yn + bonus * v) * g


def _rwkv_kernel(*refs, c_len, bb):
    _three_phase(functools.partial(_rwkv_seq, c_len=c_len), refs, 3, 14, bb)


def _rwkv(h_a, shift_prev, s0, p, c_len, bb):
    bn, seq, _ = h_a.shape
    nc = seq // c_len
    params = [p[n] for n in ("rw_mu", "rw_w0", "rw_w2", "rw_a0", "rw_a2", "rw_g2", "rw_kk", "rw_ka", "rw_rk",
                             "rw_ln_g", "rw_ln_b")]
    full = lambda a: pl.BlockSpec(a.shape, lambda b, c: (0,) * a.ndim)
    return pl.pallas_call(
        functools.partial(_rwkv_kernel, c_len=c_len, bb=bb),
        grid=(bn // bb, nc),
        in_specs=[pl.BlockSpec((bb, c_len, A_COLS), lambda b, c: (b, c, 0)),
                  pl.BlockSpec((bb, 1, A_COLS), lambda b, c: (b, 0, 0)),
                  pl.BlockSpec((bb, N_HEADS, HD, HD), lambda b, c: (b, 0, 0, 0))] + [full(a) for a in params],
        out_specs=[pl.BlockSpec((bb, c_len, MIX_W), lambda b, c: (b, c, 0)),
                   pl.BlockSpec((bb, N_HEADS, HD, HD), lambda b, c: (b, 0, 0, 0))],
        out_shape=[jax.ShapeDtypeStruct((bn, seq, MIX_W), F32),
                   jax.ShapeDtypeStruct((bn, N_HEADS, HD, HD), F32)],
        scratch_shapes=[pltpu.VMEM((bb, MIX_W, MIX_W), F32), pltpu.VMEM((bb, 8 + c_len, A_COLS), F32)],
        compiler_params=pltpu.CompilerParams(dimension_semantics=("parallel", "arbitrary"),
                                             vmem_limit_bytes=V7X_VMEM_LIMIT),
        name="rwkv",
    )(h_a, shift_prev.reshape(bn, 1, A_COLS), s0, *params)


def _gdn_seq(phase, h_ref, conv0_ref, s0_ref, cw_ref, alog_ref, dtb_ref, ng_ref, o_ref, sout_ref, sbd_ref, buf_ref,
             *, c_len):
    r_rows = 4 * c_len

    if phase == "init":
        sbd_ref[...] = jnp.zeros_like(sbd_ref)
        for hh in range(N_HEADS):
            sbd_ref[hh * HD:(hh + 1) * HD, hh * HD:(hh + 1) * HD] = s0_ref[hh]
        buf_ref[0:8, :] = conv0_ref[...]
        return
    if phase == "final":
        for hh in range(N_HEADS):
            sout_ref[hh] = sbd_ref[hh * HD:(hh + 1) * HD, hh * HD:(hh + 1) * HD]
        return

    hb = h_ref[...]
    qkv = hb[:, :B_QKV]
    buf_ref[8:8 + c_len, :] = qkv
    cw = cw_ref[...]
    conv = cw[3:4, :] * qkv
    for j in range(3):
        conv = conv + cw[j:j + 1, :] * buf_ref[5 + j:5 + j + c_len, :]
    buf_ref[0:8, :] = qkv[c_len - 8:c_len, :]
    conv = conv * jax.nn.sigmoid(conv)
    seg = _seg_ones(MIX_W, 6)

    def l2n(t):
        return t * lax.rsqrt(_mm_mask_r(t * t, seg) + 1e-6)

    q = l2n(conv[:, :MIX_W]) * (HD ** -0.5)
    k = l2n(conv[:, MIX_W:2 * MIX_W])
    v = conv[:, 2 * MIX_W:]
    z = hb[:, B_QKV:B_QKV + MIX_W]
    ab = hb[:, B_QKV + MIX_W:]
    gfull = -jnp.exp(alog_ref[...]) * _softplus(ab + dtb_ref[...])
    bfull = jax.nn.sigmoid(ab)

    incl, strict, head = _chunk_masks(c_len)
    hm = head.astype(F32)
    shift = int(math.log2(c_len))
    hrow = jnp.right_shift(_iota((r_rows, 128), 0), shift)
    lane = _iota((r_rows, 128), 1)
    gcol = jnp.sum(jnp.where(lane == hrow, _tile4(gfull), 0.0), axis=1, keepdims=True)
    bcol = jnp.sum(jnp.where(lane == hrow + N_HEADS, _tile4(bfull), 0.0), axis=1, keepdims=True)

    incl_f = incl.astype(F32)
    strict_f = strict.astype(F32)
    dmat = _mm_mask_l(incl_f, jnp.broadcast_to(gcol, (r_rows, r_rows)) * strict_f)
    g128 = jnp.broadcast_to(gcol, (r_rows, 128))
    gc = _mm_mask_l(incl_f, g128)[:, 0:1]
    ri = _iota((r_rows, r_rows), 0)
    ci = _iota((r_rows, r_rows), 1)
    after_f = jnp.logical_and(jnp.right_shift(ri, shift) == jnp.right_shift(ci, shift), ci > ri).astype(F32)
    tail = _mm_mask_l(after_f, g128)[:, 0:1]
    e_gc = jnp.exp(gc)
    e_tail = jnp.exp(tail)
    first = jnp.bitwise_and(_iota((r_rows, MIX_W), 0), c_len - 1) == 0
    g_last = jnp.sum(jnp.where(first, jnp.exp(gc + tail) * hm, 0.0), axis=0, keepdims=True)

    q_h = _tile4(q) * hm
    k_h = _tile4(k) * hm
    v_h = _tile4(v) * hm
    kb = k_h * bcol
    e_d = jnp.exp(dmat)
    m1 = _mm1(jnp.concatenate([kb, q_h], axis=0), k_h, _NT)
    a_mat = jnp.where(strict, m1[:r_rows] * e_d, 0.0)
    att = jnp.where(incl, m1[r_rows:] * e_d, 0.0)
    t_inv = _neumann_inverse(-a_mat, int(math.log2(c_len)))
    uw = _mm1(t_inv, jnp.concatenate([v_h * bcol, kb * e_gc], axis=1))
    s0 = sbd_ref[...]
    v_new = uw[:, :MIX_W] - _mm1(uw[:, MIX_W:], s0)
    o = _mm1(q_h * e_gc, s0) + _mm1(att, v_new)
    sbd_ref[...] = s0 * g_last + _mm1(k_h * e_tail, v_new, _TN)

    o = _fold4(o, c_len)
    ms = _mm_mask_r(o * o, seg) * (1.0 / HD)
    o = o * lax.rsqrt(ms + NORM_EPS) * ng_ref[...]
    o_ref[...] = o * (z * jax.nn.sigmoid(z))


def _gdn_kernel(*refs, c_len, bb):
    _three_phase(functools.partial(_gdn_seq, c_len=c_len), refs, 3, 7, bb)


def _gdn(h_b, conv_prev8, s0, cw, alog, dtb, ng, c_len, bb):
    bn, seq, _ = h_b.shape
    nc = seq // c_len
    params = [cw, alog, dtb, ng]
    full = lambda a: pl.BlockSpec(a.shape, lambda b, c: (0,) * a.ndim)
    return pl.pallas_call(
        functools.partial(_gdn_kernel, c_len=c_len, bb=bb),
        grid=(bn // bb, nc),
        in_specs=[pl.BlockSpec((bb, c_len, B_COLS_PAD), lambda b, c: (b, c, 0)),
                  pl.BlockSpec((bb, 8, B_QKV), lambda b, c: (b, 0, 0)),
                  pl.BlockSpec((bb, N_HEADS, HD, HD), lambda b, c: (b, 0, 0, 0))] + [full(a) for a in params],
        out_specs=[pl.BlockSpec((bb, c_len, MIX_W), lambda b, c: (b, c, 0)),
                   pl.BlockSpec((bb, N_HEADS, HD, HD), lambda b, c: (b, 0, 0, 0))],
        out_shape=[jax.ShapeDtypeStruct((bn, seq, MIX_W), F32),
                   jax.ShapeDtypeStruct((bn, N_HEADS, HD, HD), F32)],
        scratch_shapes=[pltpu.VMEM((bb, MIX_W, MIX_W), F32), pltpu.VMEM((bb, 8 + c_len, B_QKV), F32)],
        compiler_params=pltpu.CompilerParams(dimension_semantics=("parallel", "arbitrary"),
                                             vmem_limit_bytes=V7X_VMEM_LIMIT),
        name="gdn",
    )(h_b, conv_prev8, s0, *params)


def _attn_prompt_kernel(lam_ref, q_ref, k_ref, v_ref, bias_ref, sg_ref, o_ref, q2_ref, m_ref, l_ref, acc_ref,
                        *, tq, scale, out_scale):
    i = pl.program_id(1)
    j = pl.program_id(2)

    @pl.when(j == 0)
    def _():
        m_ref[...] = jnp.full_like(m_ref, -jnp.inf)
        l_ref[...] = jnp.zeros_like(l_ref)
        acc_ref[...] = jnp.zeros_like(acc_ref)
        lane = _iota((tq, 2 * HD), 1)
        for hh in range(N_HEADS):
            qh = q_ref[:, hh * 2 * HD:(hh + 1) * 2 * HD] * scale
            q2_ref[hh, 0:tq, :] = jnp.where(lane < HD, qh, 0.0).astype(BF16)
            q2_ref[hh, tq:2 * tq, :] = jnp.where(lane >= HD, qh, 0.0).astype(BF16)

    @pl.when(j <= i)
    def _():
        for hh in range(N_HEADS):
            sl = slice(hh * 2 * HD, (hh + 1) * 2 * HD)
            s = lax.dot_general(q2_ref[hh], k_ref[:, sl].astype(BF16), _NT, preferred_element_type=F32)
            s = s + bias_ref[hh, jnp.minimum(i - j, 2)]
            m_old = m_ref[hh]
            m_new = jnp.maximum(m_old, jnp.max(s, axis=-1, keepdims=True))
            alpha = jnp.exp(m_old - m_new)
            p = jnp.exp(s - jnp.concatenate([m_new] * (tq // 128), axis=1))
            l_ref[hh] = alpha * l_ref[hh] + jnp.sum(p, axis=-1, keepdims=True)
            acc_ref[hh] = alpha * acc_ref[hh] + jnp.dot(p.astype(BF16), v_ref[:, sl].astype(BF16),
                                                        preferred_element_type=F32)
            m_ref[hh] = m_new

    @pl.when(j == i)
    def _():
        lam = lam_ref[0]
        for hh in range(N_HEADS):
            o = acc_ref[hh] / l_ref[hh]
            od = o[:tq] - lam * o[tq:]
            od = od * lax.rsqrt(jnp.mean(od * od, axis=-1, keepdims=True) + NORM_EPS) * sg_ref[...] * out_scale
            o_ref[:, hh * 2 * HD:(hh + 1) * 2 * HD] = od


def _attn_prompt(q, k, v, bias_near, lam, subln_g, out_scale, tq=256):
    bn, seq, _ = q.shape
    nq = seq // tq
    kv_spec = pl.BlockSpec((None, tq, C_W), lambda b, i, j: (b, jnp.minimum(i, j), 0))
    return pl.pallas_call(
        functools.partial(_attn_prompt_kernel, tq=tq, scale=HD ** -0.5, out_scale=out_scale),
        grid=(bn, nq, nq),
        in_specs=[pl.BlockSpec(memory_space=pltpu.SMEM),
                  pl.BlockSpec((None, tq, C_W), lambda b, i, j: (b, i, 0)),
                  kv_spec, kv_spec,
                  pl.BlockSpec(bias_near.shape, lambda b, i, j: (0, 0, 0, 0)),
                  pl.BlockSpec(subln_g.shape, lambda b, i, j: (0, 0))],
        out_specs=pl.BlockSpec((None, tq, C_W), lambda b, i, j: (b, i, 0)),
        out_shape=jax.ShapeDtypeStruct((bn, seq, C_W), F32),
        scratch_shapes=[pltpu.VMEM((N_HEADS, 2 * tq, 2 * HD), BF16), pltpu.VMEM((N_HEADS, 2 * tq, 2 * HD), F32),
                        pltpu.VMEM((N_HEADS, 2 * tq, 2 * HD), F32), pltpu.VMEM((N_HEADS, 2 * tq, 2 * HD), F32)],
        compiler_params=pltpu.CompilerParams(dimension_semantics=("parallel", "parallel", "arbitrary"),
                                             vmem_limit_bytes=V7X_VMEM_LIMIT),
        name="attn_prompt",
    )(lam, q, k, v, bias_near, subln_g)


def _attn_sample_kernel(pt_ref, lam_ref, q_ref, kn_ref, vn_ref, *rest, lq, scale, out_scale, ppb):
    del pt_ref
    kp_refs = rest[:ppb]
    vp_refs = rest[ppb:2 * ppb]
    bp_ref, bn_ref, sg_ref, o_ref, qz_ref, m_ref, l_ref, acc_ref = rest[2 * ppb:]
    p_idx = pl.program_id(1)

    @pl.when(p_idx == 0)
    def _():
        lane = _iota((lq, 2 * HD), 1)
        pieces = []
        for hh in range(N_HEADS):
            qh = q_ref[:, hh * 2 * HD:(hh + 1) * 2 * HD]
            pieces += [jnp.where(lane < HD, qh, 0.0), jnp.where(lane >= HD, qh, 0.0)]
        qz = jnp.concatenate(pieces, axis=0)
        qz_ref[...] = qz
        s = _mm_f32(qz, kn_ref[...], _NT) * scale + bn_ref[...]
        m = jnp.max(s, axis=-1, keepdims=True)
        p = jnp.exp(s - m)
        m_ref[...] = m
        l_ref[...] = jnp.sum(p, axis=-1, keepdims=True)
        acc_ref[...] = _mm_f32(p, vn_ref[...])

    qz = qz_ref[...].astype(BF16)
    cols = bp_ref.shape[1] // ppb
    scores = []
    for u in range(ppb):
        s = lax.dot_general(qz, kp_refs[u][...].astype(BF16), _NT, preferred_element_type=F32)
        scores.append(s * scale + bp_ref[:, u * cols:(u + 1) * cols])
    m_old = m_ref[...]
    m_new = m_old
    for s in scores:
        m_new = jnp.maximum(m_new, jnp.max(s, axis=-1, keepdims=True))
    alpha = jnp.exp(m_old - m_new)
    l_new = alpha * l_ref[...]
    acc = alpha * acc_ref[...]
    for u in range(ppb):
        p = jnp.exp(scores[u] - m_new)
        l_new = l_new + jnp.sum(p, axis=-1, keepdims=True)
        acc = acc + jnp.dot(p.astype(BF16), vp_refs[u][...].astype(BF16), preferred_element_type=F32)
    l_ref[...] = l_new
    acc_ref[...] = acc
    m_ref[...] = m_new

    @pl.when(p_idx == pl.num_programs(1) - 1)
    def _():
        lam = lam_ref[0]
        o = acc_ref[...] / l_ref[...]
        for hh in range(N_HEADS):
            o1 = o[(2 * hh) * lq:(2 * hh + 1) * lq]
            o2 = o[(2 * hh + 1) * lq:(2 * hh + 2) * lq]
            od = o1 - lam * o2
            od = od * lax.rsqrt(jnp.mean(od * od, axis=-1, keepdims=True) + NORM_EPS) * sg_ref[...] * out_scale
            o_ref[:, hh * 2 * HD:(hh + 1) * 2 * HD] = od


def _attn_sample(q, k_new, v_new, cache_k, cache_v, page_table, layer, bias_past, bias_new, lam, subln_g,
                 out_scale, ppb=8):
    bn, lq, _ = q.shape
    n_pages = page_table.shape[1]
    prow = cache_k.shape[2]
    rows = 2 * N_HEADS * lq
    seq_spec = pl.BlockSpec((None, lq, C_W), lambda b, p, pt: (b, 0, 0))
    new_spec = pl.BlockSpec((None, lq * N_HEADS, 2 * HD), lambda b, p, pt: (b, 0, 0))

    def page_spec(u):
        return pl.BlockSpec((None, None, prow, 2 * HD), lambda b, p, pt: (pt[b, p * ppb + u], layer, 0, 0))

    grid_spec = pltpu.PrefetchScalarGridSpec(
        num_scalar_prefetch=1,
        grid=(bn, n_pages // ppb),
        in_specs=[pl.BlockSpec(memory_space=pltpu.SMEM), seq_spec, new_spec, new_spec]
                 + [page_spec(u) for u in range(ppb)] * 2
                 + [pl.BlockSpec((rows, ppb * prow), lambda b, p, pt: (0, p)),
                    pl.BlockSpec(bias_new.shape, lambda b, p, pt: (0, 0)),
                    pl.BlockSpec(subln_g.shape, lambda b, p, pt: (0, 0))],
        out_specs=seq_spec,
        scratch_shapes=[pltpu.VMEM((rows, 2 * HD), F32), pltpu.VMEM((rows, 1), F32), pltpu.VMEM((rows, 1), F32),
                        pltpu.VMEM((rows, 2 * HD), F32)],
    )
    return pl.pallas_call(
        functools.partial(_attn_sample_kernel, lq=lq, scale=HD ** -0.5, out_scale=out_scale, ppb=ppb),
        grid_spec=grid_spec,
        out_shape=jax.ShapeDtypeStruct((bn, lq, C_W), F32),
        compiler_params=pltpu.CompilerParams(dimension_semantics=("parallel", "arbitrary")),
        name="attn_sample",
    )(page_table, lam, q, k_new, v_new, *([cache_k] * ppb), *([cache_v] * ppb), bias_past, bias_new, subln_g)


def _proj_out_kernel(x_ref, oa_ref, ob_ref, oc_ref, wo_ref, g_ref, wq_ref, xo_ref, xnt_ref, q_ref):
    dot = lambda a, w: jnp.dot(a.astype(BF16), w, preferred_element_type=F32)
    x = x_ref[...]
    x = x + dot(oa_ref[...], wo_ref[0:MIX_W, :]) + dot(ob_ref[...], wo_ref[MIX_W:2 * MIX_W, :]) \
        + dot(oc_ref[...], wo_ref[2 * MIX_W:, :])
    xo_ref[...] = x
    xn = x * lax.rsqrt(jnp.mean(x * x, axis=-1, keepdims=True) + NORM_EPS) * g_ref[...]
    xnt_ref[...] = jnp.transpose(xn).astype(BF16)
    q_ref[...] = jnp.dot(xn.astype(BF16), wq_ref[...], preferred_element_type=F32)


def _proj_out(x2d, o_a, o_b, o_c, w_out, g, wq, tm=256):
    t = x2d.shape[0]
    nq = wq.shape[1]
    row = lambda w: pl.BlockSpec((tm, w), lambda i: (i, 0))
    full = lambda a: pl.BlockSpec(a.shape, lambda i: (0,) * a.ndim)
    return pl.pallas_call(
        _proj_out_kernel,
        grid=(t // tm,),
        in_specs=[row(D_MODEL), row(MIX_W), row(MIX_W), row(C_W), full(w_out), full(g), full(wq)],
        out_specs=[row(D_MODEL), pl.BlockSpec((D_MODEL, tm), lambda i: (0, i)), row(nq)],
        out_shape=[jax.ShapeDtypeStruct((t, D_MODEL), F32), jax.ShapeDtypeStruct((D_MODEL, t), BF16),
                   jax.ShapeDtypeStruct((t, nq), F32)],
        compiler_params=pltpu.CompilerParams(dimension_semantics=("parallel",),
                                             vmem_limit_bytes=V7X_VMEM_LIMIT),
        name="proj_out",
    )(x2d, o_a, o_b, o_c, w_out, g, wq)


def _top16(s, rid=None):
    n_rows, tm = s.shape
    if rid is None:
        rid = _iota((n_rows, tm), 0).astype(F32)
    kid = _iota((PK_TOPK, tm), 0)
    vals = jnp.zeros((PK_TOPK, tm), F32)
    idxs = jnp.zeros((PK_TOPK, tm), F32)
    for kk in range(PK_TOPK):
        m = jnp.max(s, axis=0, keepdims=True)
        ix = jnp.min(jnp.where(s == m, rid, float(1 << 20)), axis=0, keepdims=True)
        vals = jnp.where(kid == kk, m, vals)
        idxs = jnp.where(kid == kk, ix, idxs)
        s = jnp.where(rid == ix, -jnp.inf, s)
    return vals, idxs


_CAND_A, _CAND_B = 4, 3


def _candidates(sv1, sv2):
    tm = sv1.shape[1]
    row16 = _iota((PK_TOPK, tm), 0)
    rows_a = [sv1[a:a + 1, :] + sv2 for a in range(_CAND_A)]
    rows_b = [jnp.where(row16 >= _CAND_A, sv1 + sv2[b:b + 1, :], -jnp.inf) for b in range(_CAND_B)]
    cand = jnp.concatenate(rows_a + rows_b, axis=0)
    rid_a = _iota((_CAND_A * PK_TOPK, tm), 0)
    row_b = _iota((_CAND_B * PK_TOPK, tm), 0)
    rid_b = jnp.bitwise_and(row_b, PK_TOPK - 1) * PK_TOPK + jnp.right_shift(row_b, 4)
    return cand, jnp.concatenate([rid_a, rid_b], axis=0).astype(F32)


def _top16_unique(s):
    n_rows, tm = s.shape
    kid = _iota((PK_TOPK, tm), 0)
    vals = jnp.zeros((PK_TOPK, tm), F32)
    rank = jnp.full((n_rows, tm), float(PK_TOPK), F32)
    for kk in range(PK_TOPK):
        m = jnp.max(s, axis=0, keepdims=True)
        hit = s == m
        vals = jnp.where(kid == kk, m, vals)
        rank = jnp.where(hit, float(kk), rank)
        s = jnp.where(hit, -jnp.inf, s)
    chosen = jnp.sum((rank < PK_TOPK).astype(F32), axis=0, keepdims=True)
    return vals, rank, chosen == PK_TOPK


def _route_tables_unique(s1, s2):
    tm = s1.shape[1]
    sv1, r1, ok1 = _top16_unique(s1)
    sv2, r2, ok2 = _top16_unique(s2)
    cand, _ = _candidates(sv1, sv2)
    tv, r3, ok3 = _top16_unique(cand)
    z = jnp.sum(jnp.exp(tv - tv[0:1, :]), axis=0, keepdims=True)
    chosen = (r3 < PK_TOPK).astype(F32)
    n_a = _CAND_A * PK_TOPK
    cnt = chosen[n_a:n_a + PK_TOPK]
    for b in range(1, _CAND_B):
        cnt = cnt + chosen[n_a + b * PK_TOPK:n_a + (b + 1) * PK_TOPK]
    row16 = _iota((PK_TOPK, tm), 0)
    for a in range(_CAND_A):
        cnt_a = jnp.sum(chosen[a * PK_TOPK:(a + 1) * PK_TOPK], axis=0, keepdims=True)
        cnt = jnp.where(row16 == a, cnt_a, cnt)
    f = jnp.zeros((PK_NKEYS, tm), F32)
    for a in range(PK_TOPK):
        f = jnp.where(r1 == float(a), cnt[a:a + 1, :], f)
    e1 = jnp.where(r1 < PK_TOPK, jnp.exp(s1 - sv1[0:1, :]), 0.0) / z
    e2 = jnp.where(r2 < PK_TOPK, jnp.exp(s2 - sv2[0:1, :]), 0.0)
    ok = jnp.logical_and(jnp.logical_and(ok1, ok2), ok3)
    return e1, f, e2, r2, ok


def _route_tables_exact(s1, s2):
    tm = s1.shape[1]
    sv1, si1 = _top16(s1)
    sv2, si2 = _top16(s2)
    cand, cand_ids = _candidates(sv1, sv2)
    tv, ti = _top16(cand, cand_ids)
    rank_a = jnp.floor(ti * (1.0 / PK_TOPK))
    z = jnp.sum(jnp.exp(tv - tv[0:1, :]), axis=0, keepdims=True)
    aid = _iota((PK_TOPK, tm), 0).astype(F32)
    cnt = jnp.zeros((PK_TOPK, tm), F32)
    for kk in range(PK_TOPK):
        cnt = cnt + (aid == rank_a[kk:kk + 1, :]).astype(F32)
    ex1 = jnp.exp(sv1 - sv1[0:1, :]) / z
    ex2 = jnp.exp(sv2 - sv2[0:1, :])
    kid = _iota((PK_NKEYS, tm), 0).astype(F32)
    e1 = jnp.zeros((PK_NKEYS, tm), F32)
    f = jnp.zeros((PK_NKEYS, tm), F32)
    e2 = jnp.zeros((PK_NKEYS, tm), F32)
    r = jnp.full((PK_NKEYS, tm), float(PK_TOPK), F32)
    for kk in range(PK_TOPK):
        hit1 = kid == si1[kk:kk + 1, :]
        e1 = jnp.where(hit1, ex1[kk:kk + 1, :], e1)
        f = jnp.where(hit1, cnt[kk:kk + 1, :], f)
        hit2 = kid == si2[kk:kk + 1, :]
        e2 = jnp.where(hit2, ex2[kk:kk + 1, :], e2)
        r = jnp.where(hit2, float(kk), r)
    return e1, f, e2, r


def _peer_route_kernel(q_ref, keys_ref, e1_ref, f_ref, e2_ref, r_ref):
    q = q_ref[...]
    s1 = _mm1(keys_ref[0], q[:, :PK_NKEYS], _NT)
    s2 = _mm1(keys_ref[1], q[:, PK_NKEYS:], _NT)

    def write(e1, f, e2, r):
        e1_ref[...] = e1
        f_ref[...] = f
        e2_ref[...] = e2.astype(BF16)
        r_ref[...] = r.astype(BF16)

    e1, f, e2, r, ok = _route_tables_unique(s1, s2)
    all_ok = jnp.min(ok.astype(F32)) > 0.5
    write(e1, f, e2, r)

    @pl.when(jnp.logical_not(all_ok))
    def _():
        write(*_route_tables_exact(s1, s2))


def _peer_route(q, keys, tm=512):
    t = q.shape[0]
    tab = jax.ShapeDtypeStruct((PK_HEADS, PK_NKEYS, t), F32)
    tab16 = jax.ShapeDtypeStruct((PK_HEADS, PK_NKEYS, t), BF16)
    tab_spec = pl.BlockSpec((None, PK_NKEYS, tm), lambda i, h: (h, 0, i))
    return pl.pallas_call(
        _peer_route_kernel,
        grid=(t // tm, PK_HEADS),
        in_specs=[pl.BlockSpec((tm, 2 * PK_NKEYS), lambda i, h: (i, h)),
                  pl.BlockSpec((None, 2, PK_NKEYS, PK_NKEYS), lambda i, h: (h, 0, 0, 0))],
        out_specs=[tab_spec] * 4,
        out_shape=[tab, tab, tab16, tab16],
        compiler_params=pltpu.CompilerParams(dimension_semantics=("parallel", "parallel")),
        name="peer_route",
    )(q, keys)


def _peer_dense_kernel(x_ref, xnt_ref, u_ref, vt_ref, e1_ref, f_ref, e2_ref, r_ref, o_ref, acc_ref, act_ref, c_ref,
                       *, ib, tm):
    ii = pl.program_id(1)

    @pl.when(ii == 0)
    def _():
        acc_ref[...] = jnp.zeros_like(acc_ref)

    pk = 16
    ng = PK_NKEYS // pk
    group = 2
    n_parts = ib // group
    n_cols = tm // 128
    half = tm // 2
    tile = lambda ref, hh, s, lanes: jnp.broadcast_to(ref[hh, s:s + 1, lanes], (pk, 128)).astype(BF16)

    def activations(part, n):
        rows = slice(part * group * PK_NKEYS, (part + 1) * group * PK_NKEYS)
        cols = slice(n * half, (n + 1) * half)
        act = jnp.dot(u_ref[rows, :], xnt_ref[:, cols], preferred_element_type=F32)
        act_ref[part * group * ng:(part + 1) * group * ng, :, cols] = act.reshape(group * ng, pk, half)

    def gates_of(part, c):
        lanes = slice(c * 128, (c + 1) * 128)
        s0 = part * group
        gates = [None] * group
        for hh in range(PK_HEADS):
            rank = r_ref[hh, :, :, lanes]
            e2 = e2_ref[hh, :, :, lanes]
            for ds in range(group):
                e1 = tile(e1_ref, hh, s0 + ds, lanes)
                f = tile(f_ref, hh, s0 + ds, lanes)
                term = jnp.maximum(jnp.minimum(f - rank, e1), 0.0) * e2
                gates[ds] = term if gates[ds] is None else gates[ds] + term
        for ds in range(group):
            rows = slice((s0 + ds) * ng, (s0 + ds + 1) * ng)
            a = act_ref[rows, :, lanes]
            a = 0.5 * a * (1.0 + lax.erf(a * math.sqrt(0.5)))
            c_ref[rows, :, lanes] = (gates[ds].astype(F32) * a).astype(BF16)

    activations(0, 0)
    activations(0, 1)
    for part in range(n_parts):
        for c in range(n_cols):
            if part + 1 < n_parts and c < 2:
                activations(part + 1, c)
            gates_of(part, c)
    acc_ref[...] += jnp.dot(vt_ref[...], c_ref[...].reshape(ib * PK_NKEYS, tm), preferred_element_type=F32)

    @pl.when(ii == pl.num_programs(1) - 1)
    def _():
        o_ref[...] = x_ref[...] + jnp.transpose(acc_ref[...])


def _peer_dense(x2d, xnt, u_bf, vt_bf, e1, f, e2, r, tm=512, ib=8):
    t = x2d.shape[0]
    ni = PK_NKEYS // ib
    tab_i = pl.BlockSpec((PK_HEADS, ib, tm), lambda tt, ii: (0, ii, tt))
    pk = 16
    ng = PK_NKEYS // pk
    tab_j = pl.BlockSpec((PK_HEADS, ng, pk, tm), lambda tt, ii: (0, 0, 0, tt))
    e2 = e2.reshape(PK_HEADS, ng, pk, t)
    r = r.reshape(PK_HEADS, ng, pk, t)
    return pl.pallas_call(
        functools.partial(_peer_dense_kernel, ib=ib, tm=tm),
        grid=(t // tm, ni),
        in_specs=[pl.BlockSpec((tm, D_MODEL), lambda tt, ii: (tt, 0)),
                  pl.BlockSpec((D_MODEL, tm), lambda tt, ii: (0, tt)),
                  pl.BlockSpec((ib * PK_NKEYS, D_MODEL), lambda tt, ii: (ii, 0)),
                  pl.BlockSpec((D_MODEL, ib * PK_NKEYS), lambda tt, ii: (0, ii)),
                  tab_i, tab_i, tab_j, tab_j],
        out_specs=pl.BlockSpec((tm, D_MODEL), lambda tt, ii: (tt, 0)),
        out_shape=jax.ShapeDtypeStruct((t, D_MODEL), F32),
        scratch_shapes=[pltpu.VMEM((D_MODEL, tm), F32), pltpu.VMEM((ib * ng, pk, tm), F32),
                        pltpu.VMEM((ib * ng, pk, tm), BF16)],
        compiler_params=pltpu.CompilerParams(dimension_semantics=("parallel", "arbitrary"),
                                             vmem_limit_bytes=V7X_VMEM_LIMIT),
        name="peer_dense",
    )(x2d, xnt, u_bf, vt_bf, e1, f, e2, r)


def _t5_bucket(rel):
    n = jnp.maximum(-rel, 0)
    max_exact = REL_BUCKETS // 2
    nf = jnp.maximum(n, 1).astype(F32)
    large = max_exact + (jnp.log(nf / max_exact) / math.log(REL_MAX_DIST / max_exact)
                         * (REL_BUCKETS - max_exact)).astype(jnp.int32)
    large = jnp.minimum(large, REL_BUCKETS - 1)
    return jnp.where(n < max_exact, n, large)


def _bias_table(rel_bias, rel):
    bucket = _t5_bucket(rel)[None]
    col = lambda b: rel_bias[b].astype(F32).reshape((-1,) + (1,) * rel.ndim)
    out = jnp.broadcast_to(col(0), (rel_bias.shape[1],) + rel.shape)
    for b in range(1, REL_BUCKETS):
        out = jnp.where(bucket == b, col(b), out)
    return out


def _prompt_bias_tiles(rel_bias, tq):
    rel = jnp.arange(tq)[None, :] - jnp.arange(tq)[:, None]
    tiles = []
    for d in range(3):
        b = _bias_table(rel_bias, rel - d * tq)
        if d == 0:
            b = jnp.where(rel <= 0, b, -jnp.inf)
        tiles.append(jnp.concatenate([b, b], axis=1))
    return jnp.stack(tiles, axis=1)


def _prep_layer(l, W):
    row = lambda a: a.reshape(1, -1).astype(F32)
    w_in = W["w_in"][l]
    o_b = A_COLS
    o_c = A_COLS + 4 * MIX_W + 2 * N_HEADS
    w_cat = jnp.concatenate([
        w_in[:, :A_COLS],
        w_in[:, o_b:o_b + B_QKV],
        w_in[:, o_b + B_QKV + 2 * N_HEADS:o_c],
        w_in[:, o_b + B_QKV:o_b + B_QKV + 2 * N_HEADS],
        jnp.zeros((D_MODEL, 128 - 2 * N_HEADS), F32),
        w_in[:, o_c:],
    ], axis=1).astype(BF16)
    pad_lane = lambda a, off: jnp.zeros((1, 128), F32).at[0, off:off + a.shape[0]].set(a)
    alog = pad_lane(W["dn_a_log"][l], 0)
    dtb = pad_lane(W["dn_dt_bias"][l], 0)
    f32 = F32
    lam_init = 0.8 - 0.6 * math.exp(-0.3 * l)
    lam = (jnp.exp(jnp.sum(W["df_lq1"][l].astype(f32) * W["df_lk1"][l].astype(f32)))
           - jnp.exp(jnp.sum(W["df_lq2"][l].astype(f32) * W["df_lk2"][l].astype(f32))) + lam_init)
    return dict(
        rms_mix_g=row(W["rms_mix_g"][l]), w_cat=w_cat,
        qg=row(jnp.tile(W["df_qn_g"][l].reshape(-1), N_HEADS)),
        kg=row(jnp.tile(W["df_kn_g"][l].reshape(-1), N_HEADS)),
        rw_mu=row(W["rw_mu"][l]), rw_w0=row(W["rw_w0"][l]), rw_w2=W["rw_w2"][l], rw_a0=row(W["rw_a0"][l]),
        rw_a2=W["rw_a2"][l], rw_g2=W["rw_g2"][l], rw_kk=row(W["rw_kk"][l]), rw_ka=row(W["rw_ka"][l]),
        rw_rk=row(W["rw_rk"][l]), rw_ln_g=row(W["rw_ln_g"][l]), rw_ln_b=row(W["rw_ln_b"][l]),
        dn_conv_w=W["dn_conv_w"][l], dn_alog=alog, dn_dtb=dtb,
        dn_norm_g=row(jnp.tile(W["dn_norm_g"][l], N_HEADS)),
        lam=lam.reshape(1).astype(F32), lam_init=lam_init, subln_g=row(W["df_subln_g"][l]),
        w_out=W["w_out"][l].astype(BF16), rms_ffn_g=row(W["rms_ffn_g"][l]),
        pk_wq=W["pk_wq"][l].astype(BF16),
        pk_keys=W["pk_keys"][l], pk_u=W["pk_u"][l].astype(BF16), pk_vt=jnp.transpose(W["pk_v"][l]).astype(BF16),
    )


def _layer(x, shift_prev, s_rwkv, conv_prev, s_dn, attn_fn, P, c_len, bb):
    bn, seq, _ = x.shape
    t = bn * seq
    x2d = x.reshape(t, D_MODEL)
    h_a, h_b, q, k, v = _proj_in(x2d, P["rms_mix_g"], P["w_cat"], P["qg"], P["kg"])
    h_a3 = h_a.reshape(bn, seq, A_COLS)
    h_b3 = h_b.reshape(bn, seq, B_COLS_PAD)
    o_a, s_rwkv_new = _rwkv(h_a3, shift_prev, s_rwkv, P, c_len, bb)
    conv8 = jnp.concatenate([jnp.zeros((bn, 5, B_QKV), F32), conv_prev], axis=1)
    o_b, s_dn_new = _gdn(h_b3, conv8, s_dn, P["dn_conv_w"], P["dn_alog"], P["dn_dtb"], P["dn_norm_g"], c_len,
                         bb)
    q3, k3, v3 = (a.reshape(bn, seq, C_W) for a in (q, k, v))
    o_c = attn_fn(q3, k3, v3)
    x_new, xnt, pq = _proj_out(x2d, o_a.reshape(t, MIX_W), o_b.reshape(t, MIX_W), o_c.reshape(t, C_W),
                               P["w_out"], P["rms_ffn_g"], P["pk_wq"])
    e1, f, e2, r = _peer_route(pq, P["pk_keys"])
    y = _peer_dense(x_new, xnt, P["pk_u"], P["pk_vt"], e1, f, e2, r)
    k_rows = k.reshape(bn, seq, N_HEADS, 2 * HD)
    v_rows = v.reshape(bn, seq, N_HEADS, 2 * HD)
    shift_new = h_a3[:, -1]
    conv_new = h_b3[:, -3:, :B_QKV]
    return y.reshape(bn, seq, D_MODEL), k_rows, v_rows, s_rwkv_new, shift_new, s_dn_new, conv_new


def kernel(x_prompt, x_sample, cache_k, cache_v, state_rwkv, state_rwkv_shift, state_dn, state_dn_conv, page_table, rms_mix_g, w_in, w_out, rw_mu, rw_w0, rw_w2, rw_a0, rw_a2, rw_g2, rw_kk, rw_ka, rw_rk, rw_ln_g, rw_ln_b, dn_conv_w, dn_a_log, dn_dt_bias, dn_norm_g, df_qn_g, df_kn_g, df_lq1, df_lk1, df_lq2, df_lk2, df_subln_g, rel_bias, rms_ffn_g, pk_wq, pk_keys, pk_u, pk_v):
    W = dict(rms_mix_g=rms_mix_g, w_in=w_in, w_out=w_out, rw_mu=rw_mu, rw_w0=rw_w0, rw_w2=rw_w2, rw_a0=rw_a0,
             rw_a2=rw_a2, rw_g2=rw_g2, rw_kk=rw_kk, rw_ka=rw_ka, rw_rk=rw_rk, rw_ln_g=rw_ln_g, rw_ln_b=rw_ln_b,
             dn_conv_w=dn_conv_w, dn_a_log=dn_a_log, dn_dt_bias=dn_dt_bias, dn_norm_g=dn_norm_g,
             df_qn_g=df_qn_g, df_kn_g=df_kn_g, df_lq1=df_lq1, df_lk1=df_lk1, df_lq2=df_lq2, df_lk2=df_lk2,
             df_subln_g=df_subln_g, rms_ffn_g=rms_ffn_g, pk_wq=pk_wq, pk_keys=pk_keys, pk_u=pk_u, pk_v=pk_v)
    depth = w_in.shape[0]
    bp, seq_p, _ = x_prompt.shape
    bs, seq_s, _ = x_sample.shape
    n_pages = page_table.shape[1]
    page = cache_k.shape[2]
    past = n_pages * page
    tq = 256
    ck = cache_k.reshape(cache_k.shape[0], depth, page * N_HEADS, 2 * HD)
    cv = cache_v.reshape(cache_v.shape[0], depth, page * N_HEADS, 2 * HD)

    bias_near = _prompt_bias_tiles(rel_bias, tq)
    q_pos = past + jnp.arange(seq_s)
    same_head = jnp.eye(N_HEADS, dtype=bool)[:, None, None, :]

    def expand(b):
        bx = jnp.where(same_head, b[..., None], -jnp.inf)
        return jnp.broadcast_to(bx[:, None], (N_HEADS, 2) + bx.shape[1:]).reshape(2 * N_HEADS * seq_s, -1)

    bias_past = expand(_bias_table(rel_bias, jnp.arange(past)[None, :] - q_pos[:, None]))
    rel_new = q_pos[None, :] - q_pos[:, None]
    bias_new = expand(jnp.where(rel_new <= 0, _bias_table(rel_bias, rel_new), -jnp.inf))

    xp, xs = x_prompt, x_sample
    outs = [[] for _ in range(12)]
    zeros = lambda *s: jnp.zeros(s, F32)
    for l in range(depth):
        P = _prep_layer(l, W)
        out_scale = 1.0 - P["lam_init"]
        attn_p = lambda q, k, v: _attn_prompt(q, k, v, bias_near, P["lam"], P["subln_g"], out_scale, tq)
        by_head = lambda a: a.reshape(bs, seq_s * N_HEADS, 2 * HD)
        attn_s = lambda q, k, v: _attn_sample(q, by_head(k), by_head(v), ck, cv, page_table, l, bias_past, bias_new,
                                              P["lam"], P["subln_g"], out_scale)
        xp, kp, vp, sap, shp, sbp, cvp = _layer(
            xp, zeros(bp, A_COLS), zeros(bp, N_HEADS, HD, HD), zeros(bp, 3, B_QKV), zeros(bp, N_HEADS, HD, HD),
            attn_p, P, 64, 2)
        xs, ksn, vsn, sas, shs, sbs, cvs = _layer(
            xs, state_rwkv_shift[l], state_rwkv[l], state_dn_conv[l], state_dn[l], attn_s, P, seq_s, 8)
        for lst, val in zip(outs, (kp, vp, ksn, vsn, sap, sas, shp, shs, sbp, sbs, cvp, cvs)):
            lst.append(val)
    stack = lambda i, ax: jnp.stack(outs[i], axis=ax)
    return (xp, xs, stack(0, 1), stack(1, 1), stack(2, 1), stack(3, 1),
            stack(4, 0), stack(5, 0), stack(6, 0), stack(7, 0), stack(8, 0), stack(9, 0), stack(10, 0), stack(11, 0))
```

```python
import functools
import math

import jax
import jax.numpy as jnp
from jax import lax
from jax.experimental import pallas as pl
from jax.experimental.pallas import tpu as pltpu

F32 = jnp.float32
BF16 = jnp.bfloat16

D_MODEL = 1024
N_HEADS = 4
HD = 64
MIX_W = N_HEADS * HD
A_COLS = 1024
B_QKV = 3 * MIX_W
B_COLS_PAD = 1152
C_W = 512
NORM_EPS = 1e-6
A_LN_EPS = 64e-5
REL_BUCKETS = 32
REL_MAX_DIST = 128
PK_HEADS = 8
PK_NKEYS = 128
PK_TOPK = 16
V7X_VMEM_LIMIT = 56 * 1024 * 1024


def _parts(a, n):
    out = []
    r = a
    for i in range(n):
        p = r.astype(BF16)
        out.append(p)
        if i + 1 < n:
            r = r - p.astype(F32)
    return out


_NN = (((1,), (0,)), ((), ()))
_NT = (((1,), (1,)), ((), ()))
_TN = (((0,), (0,)), ((), ()))


def _mm(a, b, dims=_NN, na=2, nb=2, order=1):
    ap = _parts(a, na)
    bp = _parts(b, nb)
    acc = None
    for s in range(order, -1, -1):
        for i in range(na):
            j = s - i
            if 0 <= j < nb:
                t = lax.dot_general(ap[i], bp[j], dims, preferred_element_type=F32)
                acc = t if acc is None else acc + t
    return acc


def _mm1(a, b, dims=_NN):
    return _mm(a, b, dims, na=1, nb=1, order=0)


def _mm_f32(a, b, dims=_NN):
    return lax.dot_general(a, b, dims, precision=lax.Precision.HIGHEST, preferred_element_type=F32)


def _mm_mask_r(a, mask, dims=_NN):
    return _mm(a, mask, dims, na=2, nb=1, order=1)


def _mm_mask_l(mask, b, dims=_NN):
    return _mm(mask, b, dims, na=1, nb=3, order=2)


def _iota(shape, dim):
    return lax.broadcasted_iota(jnp.int32, shape, dim)


def _seg_ones(n, seg_shift):
    r = jnp.right_shift(_iota((n, n), 0), seg_shift)
    c = jnp.right_shift(_iota((n, n), 1), seg_shift)
    return (r == c).astype(F32)


def _tile4(z):
    return jnp.concatenate([z, z, z, z], axis=0)


def _fold4(y, c):
    return y[0:c] + y[c:2 * c] + y[2 * c:3 * c] + y[3 * c:4 * c]


def _neumann_inverse(n, steps):
    rows = n.shape[0]
    eye = (_iota((rows, rows), 0) == _iota((rows, rows), 1)).astype(F32)
    t = eye + n
    p = n
    for _ in range(steps - 1):
        p = _mm1(p, p)
        t = t + _mm1(t, p)
    return t


def _softplus(y):
    return jnp.maximum(y, 0.0) + jnp.log1p(jnp.exp(-jnp.abs(y)))


def _chunk_masks(c_len):
    r = 4 * c_len
    shift = int(math.log2(c_len))
    ri = _iota((r, r), 0)
    ci = _iota((r, r), 1)
    same = jnp.right_shift(ri, shift) == jnp.right_shift(ci, shift)
    incl = jnp.logical_and(same, ci <= ri)
    strict = jnp.logical_and(same, ci < ri)
    head = jnp.right_shift(_iota((r, MIX_W), 0), shift) == jnp.right_shift(_iota((r, MIX_W), 1), 6)
    return incl, strict, head


def _proj_in_kernel(x_ref, g_ref, w_ref, qg_ref, kg_ref, ha_ref, hb_ref, q_ref, k_ref, v_ref, kr_ref, vr_ref):
    x = x_ref[...]
    xn = x * lax.rsqrt(jnp.mean(x * x, axis=-1, keepdims=True) + NORM_EPS) * g_ref[...]
    h = jnp.dot(xn.astype(BF16), w_ref[...], preferred_element_type=F32)
    o = A_COLS
    ha_ref[...] = h[:, :o]
    hb_ref[...] = h[:, o:o + B_COLS_PAD]
    o += B_COLS_PAD
    seg = _seg_ones(C_W, 6)

    def qk_norm(t, g):
        ms = _mm_mask_r(t * t, seg) * (1.0 / HD)
        return t * lax.rsqrt(ms + NORM_EPS) * g

    q_ref[...] = qk_norm(h[:, o:o + C_W], qg_ref[...])
    k = qk_norm(h[:, o + C_W:o + 2 * C_W], kg_ref[...])
    v = h[:, o + 2 * C_W:o + 3 * C_W]
    k_ref[...] = k
    v_ref[...] = v
    tm = k.shape[0]
    for hh in range(N_HEADS):
        kr_ref[pl.ds(hh, tm, stride=N_HEADS), :] = k[:, hh * 2 * HD:(hh + 1) * 2 * HD]
        vr_ref[pl.ds(hh, tm, stride=N_HEADS), :] = v[:, hh * 2 * HD:(hh + 1) * 2 * HD]


def _proj_in(x2d, g, w_cat, qg, kg, tm=512):
    t = x2d.shape[0]
    assert t % tm == 0, (t, tm)
    n = w_cat.shape[1]
    row = lambda w: pl.BlockSpec((tm, w), lambda i: (i, 0))
    full = lambda a: pl.BlockSpec(a.shape, lambda i: (0,) * a.ndim)
    by_head = pl.BlockSpec((tm * N_HEADS, 2 * HD), lambda i: (i, 0))
    rows_shape = jax.ShapeDtypeStruct((t * N_HEADS, 2 * HD), F32)
    return pl.pallas_call(
        _proj_in_kernel,
        grid=(t // tm,),
        in_specs=[row(D_MODEL), full(g), full(w_cat), full(qg), full(kg)],
        out_specs=[row(A_COLS), row(B_COLS_PAD), row(C_W), row(C_W), row(C_W), by_head, by_head],
        out_shape=[jax.ShapeDtypeStruct((t, A_COLS), F32), jax.ShapeDtypeStruct((t, B_COLS_PAD), F32),
                   jax.ShapeDtypeStruct((t, C_W), F32), jax.ShapeDtypeStruct((t, C_W), F32),
                   jax.ShapeDtypeStruct((t, C_W), F32), rows_shape, rows_shape],
        compiler_params=pltpu.CompilerParams(dimension_semantics=("parallel",),
                                             vmem_limit_bytes=V7X_VMEM_LIMIT),
        name="proj_in",
    )(x2d, g, w_cat, qg, kg)


def _three_phase(seq_fn, refs, n_shared_from, n_shared_to, bb):
    c = pl.program_id(1)

    def per_seq(bi):
        return [r if n_shared_from <= n < n_shared_to else r.at[bi] for n, r in enumerate(refs)]

    @pl.when(c == 0)
    def _():
        for bi in range(bb):
            seq_fn("init", *per_seq(bi))

    for bi in range(bb):
        seq_fn("chunk", *per_seq(bi))

    @pl.when(c == pl.num_programs(1) - 1)
    def _():
        for bi in range(bb):
            seq_fn("final", *per_seq(bi))


def _rwkv_seq(phase, h_ref, shift_ref, s0_ref, mu_ref, w0_ref, w2_ref, a0_ref, a2_ref, g2_ref, kk_ref, ka_ref,
              rk_ref, lng_ref, lnb_ref, o_ref, sout_ref, sbd_ref, buf_ref, *, c_len):
    r_rows = 4 * c_len

    if phase == "init":
        sbd_ref[...] = jnp.zeros_like(sbd_ref)
        for hh in range(N_HEADS):
            sbd_ref[hh * HD:(hh + 1) * HD, hh * HD:(hh + 1) * HD] = s0_ref[hh]
        buf_ref[0:8, :] = jnp.broadcast_to(shift_ref[...], (8, A_COLS))
        return
    if phase == "final":
        for hh in range(N_HEADS):
            sout_ref[hh] = sbd_ref[hh * HD:(hh + 1) * HD, hh * HD:(hh + 1) * HD]
        return

    x = h_ref[...]
    buf_ref[8:8 + c_len, :] = x
    prev = buf_ref[7:7 + c_len, :]
    buf_ref[0:8, :] = x[c_len - 8:c_len, :]

    hs = x + (prev - x) * mu_ref[...]
    r = hs[:, 0:MIX_W]
    k = hs[:, MIX_W:2 * MIX_W]
    v = hs[:, 2 * MIX_W:3 * MIX_W]
    xw = hs[:, 768:832]
    xa = hs[:, 832:896]
    xg = hs[:, 896:1024]
    w = -_softplus(-(w0_ref[...] + _mm(jnp.tanh(xw), w2_ref[...]))) - 0.5
    a = jax.nn.sigmoid(a0_ref[...] + _mm(xa, a2_ref[...]))
    g = _mm(jax.nn.sigmoid(xg), g2_ref[...])
    seg = _seg_ones(MIX_W, 6)
    kkv = k * kk_ref[...]
    kkn = kkv * lax.rsqrt(_mm_mask_r(kkv * kkv, seg) + 1e-6)
    k2 = k * (1.0 + (a - 1.0) * ka_ref[...])
    logd = -jnp.exp(w)
    tri = (_iota((c_len, c_len), 1) <= _iota((c_len, c_len), 0)).astype(F32)
    cum = _mm_mask_l(tri, logd)
    p_in = jnp.exp(cum)
    p_inv = jnp.exp(-cum)
    p_prev = jnp.exp(cum - logd)

    incl, strict, head = _chunk_masks(c_len)
    hm = head.astype(F32)
    a_t = _tile4(-kkn * p_prev) * hm
    b_t = _tile4(kkn * a * p_inv) * hm
    k_t = _tile4(k2 * p_inv) * hm
    r_t = _tile4(r * p_in) * hm
    v_t = _tile4(v) * hm

    s0 = sbd_ref[...]
    ar = jnp.concatenate([a_t, r_t], axis=0)
    bk = jnp.concatenate([b_t, k_t], axis=0)
    gram = _mm1(ar, bk, _NT)
    h0 = _mm1(ar, s0, _NT)
    l_ab = jnp.where(strict, gram[:r_rows, :r_rows], 0.0)
    l_ak = jnp.where(strict, gram[:r_rows, r_rows:], 0.0)
    m_rb = jnp.where(incl, gram[r_rows:, :r_rows], 0.0)
    m_rk = jnp.where(incl, gram[r_rows:, r_rows:], 0.0)
    t_inv = _neumann_inverse(l_ab, int(math.log2(c_len)))
    u = _mm1(t_inv, h0[:r_rows] + _mm1(l_ak, v_t))
    uv = jnp.concatenate([u, v_t], axis=0)
    y = h0[r_rows:] + _mm1(jnp.concatenate([m_rb, m_rk], axis=1), uv)
    sbd_ref[...] = (s0 + _mm1(uv, bk, _TN)) * p_in[c_len - 1:c_len, :]

    y = _fold4(y, c_len)
    mean = _mm_mask_r(y, seg) * (1.0 / HD)
    yc = y - mean
    var = _mm_mask_r(yc * yc, seg) * (1.0 / HD)
    yn = yc * lax.rsqrt(var + A_LN_EPS) * lng_ref[...] + lnb_ref[...]
    bonus = _mm_mask_r(r * k2 * rk_ref[...], seg)
    o_ref[...] = (yn + bonus * v) * g


def _rwkv_kernel(*refs, c_len, bb):
    _three_phase(functools.partial(_rwkv_seq, c_len=c_len), refs, 3, 14, bb)


def _rwkv(h_a, shift_prev, s0, p, c_len, bb):
    bn, seq, _ = h_a.shape
    nc = seq // c_len
    params = [p[n] for n in ("rw_mu", "rw_w0", "rw_w2", "rw_a0", "rw_a2", "rw_g2", "rw_kk", "rw_ka", "rw_rk",
                             "rw_ln_g", "rw_ln_b")]
    full = lambda a: pl.BlockSpec(a.shape, lambda b, c: (0,) * a.ndim)
    return pl.pallas_call(
        functools.partial(_rwkv_kernel, c_len=c_len, bb=bb),
        grid=(bn // bb, nc),
        in_specs=[pl.BlockSpec((bb, c_len, A_COLS), lambda b, c: (b, c, 0)),
                  pl.BlockSpec((bb, 1, A_COLS), lambda b, c: (b, 0, 0)),
                  pl.BlockSpec((bb, N_HEADS, HD, HD), lambda b, c: (b, 0, 0, 0))] + [full(a) for a in params],
        out_specs=[pl.BlockSpec((bb, c_len, MIX_W), lambda b, c: (b, c, 0)),
                   pl.BlockSpec((bb, N_HEADS, HD, HD), lambda b, c: (b, 0, 0, 0))],
        out_shape=[jax.ShapeDtypeStruct((bn, seq, MIX_W), F32),
                   jax.ShapeDtypeStruct((bn, N_HEADS, HD, HD), F32)],
        scratch_shapes=[pltpu.VMEM((bb, MIX_W, MIX_W), F32), pltpu.VMEM((bb, 8 + c_len, A_COLS), F32)],
        compiler_params=pltpu.CompilerParams(dimension_semantics=("parallel", "arbitrary"),
                                             vmem_limit_bytes=V7X_VMEM_LIMIT),
        name="rwkv",
    )(h_a, shift_prev.reshape(bn, 1, A_COLS), s0, *params)


def _gdn_seq(phase, h_ref, conv0_ref, s0_ref, cw_ref, alog_ref, dtb_ref, ng_ref, o_ref, sout_ref, sbd_ref, buf_ref,
             *, c_len):
    r_rows = 4 * c_len

    if phase == "init":
        sbd_ref[...] = jnp.zeros_like(sbd_ref)
        for hh in range(N_HEADS):
            sbd_ref[hh * HD:(hh + 1) * HD, hh * HD:(hh + 1) * HD] = s0_ref[hh]
        buf_ref[0:8, :] = conv0_ref[...]
        return
    if phase == "final":
        for hh in range(N_HEADS):
            sout_ref[hh] = sbd_ref[hh * HD:(hh + 1) * HD, hh * HD:(hh + 1) * HD]
        return

    hb = h_ref[...]
    qkv = hb[:, :B_QKV]
    buf_ref[8:8 + c_len, :] = qkv
    cw = cw_ref[...]
    conv = cw[3:4, :] * qkv
    for j in range(3):
        conv = conv + cw[j:j + 1, :] * buf_ref[5 + j:5 + j + c_len, :]
    buf_ref[0:8, :] = qkv[c_len - 8:c_len, :]
    conv = conv * jax.nn.sigmoid(conv)
    seg = _seg_ones(MIX_W, 6)

    def l2n(t):
        return t * lax.rsqrt(_mm_mask_r(t * t, seg) + 1e-6)

    q = l2n(conv[:, :MIX_W]) * (HD ** -0.5)
    k = l2n(conv[:, MIX_W:2 * MIX_W])
    v = conv[:, 2 * MIX_W:]
    z = hb[:, B_QKV:B_QKV + MIX_W]
    ab = hb[:, B_QKV + MIX_W:]
    gfull = -jnp.exp(alog_ref[...]) * _softplus(ab + dtb_ref[...])
    bfull = jax.nn.sigmoid(ab)

    incl, strict, head = _chunk_masks(c_len)
    hm = head.astype(F32)
    shift = int(math.log2(c_len))
    hrow = jnp.right_shift(_iota((r_rows, 128), 0), shift)
    lane = _iota((r_rows, 128), 1)
    gcol = jnp.sum(jnp.where(lane == hrow, _tile4(gfull), 0.0), axis=1, keepdims=True)
    bcol = jnp.sum(jnp.where(lane == hrow + N_HEADS, _tile4(bfull), 0.0), axis=1, keepdims=True)

    incl_f = incl.astype(F32)
    strict_f = strict.astype(F32)
    dmat = _mm_mask_l(incl_f, jnp.broadcast_to(gcol, (r_rows, r_rows)) * strict_f)
    g128 = jnp.broadcast_to(gcol, (r_rows, 128))
    gc = _mm_mask_l(incl_f, g128)[:, 0:1]
    ri = _iota((r_rows, r_rows), 0)
    ci = _iota((r_rows, r_rows), 1)
    after_f = jnp.logical_and(jnp.right_shift(ri, shift) == jnp.right_shift(ci, shift), ci > ri).astype(F32)
    tail = _mm_mask_l(after_f, g128)[:, 0:1]
    e_gc = jnp.exp(gc)
    e_tail = jnp.exp(tail)
    first = jnp.bitwise_and(_iota((r_rows, MIX_W), 0), c_len - 1) == 0
    g_last = jnp.sum(jnp.where(first, jnp.exp(gc + tail) * hm, 0.0), axis=0, keepdims=True)

    q_h = _tile4(q) * hm
    k_h = _tile4(k) * hm
    v_h = _tile4(v) * hm
    kb = k_h * bcol
    e_d = jnp.exp(dmat)
    m1 = _mm1(jnp.concatenate([kb, q_h], axis=0), k_h, _NT)
    a_mat = jnp.where(strict, m1[:r_rows] * e_d, 0.0)
    att = jnp.where(incl, m1[r_rows:] * e_d, 0.0)
    t_inv = _neumann_inverse(-a_mat, int(math.log2(c_len)))
    uw = _mm1(t_inv, jnp.concatenate([v_h * bcol, kb * e_gc], axis=1))
    s0 = sbd_ref[...]
    v_new = uw[:, :MIX_W] - _mm1(uw[:, MIX_W:], s0)
    o = _mm1(q_h * e_gc, s0) + _mm1(att, v_new)
    sbd_ref[...] = s0 * g_last + _mm1(k_h * e_tail, v_new, _TN)

    o = _fold4(o, c_len)
    ms = _mm_mask_r(o * o, seg) * (1.0 / HD)
    o = o * lax.rsqrt(ms + NORM_EPS) * ng_ref[...]
    o_ref[...] = o * (z * jax.nn.sigmoid(z))


def _gdn_kernel(*refs, c_len, bb):
    _three_phase(functools.partial(_gdn_seq, c_len=c_len), refs, 3, 7, bb)


def _gdn(h_b, conv_prev8, s0, cw, alog, dtb, ng, c_len, bb):
    bn, seq, _ = h_b.shape
    nc = seq // c_len
    params = [cw, alog, dtb, ng]
    full = lambda a: pl.BlockSpec(a.shape, lambda b, c: (0,) * a.ndim)
    return pl.pallas_call(
        functools.partial(_gdn_kernel, c_len=c_len, bb=bb),
        grid=(bn // bb, nc),
        in_specs=[pl.BlockSpec((bb, c_len, B_COLS_PAD), lambda b, c: (b, c, 0)),
                  pl.BlockSpec((bb, 8, B_QKV), lambda b, c: (b, 0, 0)),
                  pl.BlockSpec((bb, N_HEADS, HD, HD), lambda b, c: (b, 0, 0, 0))] + [full(a) for a in params],
        out_specs=[pl.BlockSpec((bb, c_len, MIX_W), lambda b, c: (b, c, 0)),
                   pl.BlockSpec((bb, N_HEADS, HD, HD), lambda b, c: (b, 0, 0, 0))],
        out_shape=[jax.ShapeDtypeStruct((bn, seq, MIX_W), F32),
                   jax.ShapeDtypeStruct((bn, N_HEADS, HD, HD), F32)],
        scratch_shapes=[pltpu.VMEM((bb, MIX_W, MIX_W), F32), pltpu.VMEM((bb, 8 + c_len, B_QKV), F32)],
        compiler_params=pltpu.CompilerParams(dimension_semantics=("parallel", "arbitrary"),
                                             vmem_limit_bytes=V7X_VMEM_LIMIT),
        name="gdn",
    )(h_b, conv_prev8, s0, *params)


def _attn_prompt_kernel(lam_ref, q_ref, k_ref, v_ref, bias_ref, sg_ref, o_ref, q2_ref, m_ref, l_ref, acc_ref,
                        *, tq, scale, out_scale):
    i = pl.program_id(1)
    j = pl.program_id(2)

    @pl.when(j == 0)
    def _():
        m_ref[...] = jnp.full_like(m_ref, -jnp.inf)
        l_ref[...] = jnp.zeros_like(l_ref)
        acc_ref[...] = jnp.zeros_like(acc_ref)
        lane = _iota((tq, 2 * HD), 1)
        for hh in range(N_HEADS):
            qh = q_ref[:, hh * 2 * HD:(hh + 1) * 2 * HD] * scale
            q2_ref[hh, 0:tq, :] = jnp.where(lane < HD, qh, 0.0).astype(BF16)
            q2_ref[hh, tq:2 * tq, :] = jnp.where(lane >= HD, qh, 0.0).astype(BF16)

    @pl.when(j <= i)
    def _():
        for hh in range(N_HEADS):
            sl = slice(hh * 2 * HD, (hh + 1) * 2 * HD)
            s = lax.dot_general(q2_ref[hh], k_ref[:, sl].astype(BF16), _NT, preferred_element_type=F32)
            s = s + bias_ref[hh, jnp.minimum(i - j, 2)]
            m_old = m_ref[hh]
            m_new = jnp.maximum(m_old, jnp.max(s, axis=-1, keepdims=True))
            alpha = jnp.exp(m_old - m_new)
            p = jnp.exp(s - jnp.concatenate([m_new] * (tq // 128), axis=1))
            l_ref[hh] = alpha * l_ref[hh] + jnp.sum(p, axis=-1, keepdims=True)
            acc_ref[hh] = alpha * acc_ref[hh] + jnp.dot(p.astype(BF16), v_ref[:, sl].astype(BF16),
                                                        preferred_element_type=F32)
            m_ref[hh] = m_new

    @pl.when(j == i)
    def _():
        lam = lam_ref[0]
        for hh in range(N_HEADS):
            o = acc_ref[hh] / l_ref[hh]
            od = o[:tq] - lam * o[tq:]
            od = od * lax.rsqrt(jnp.mean(od * od, axis=-1, keepdims=True) + NORM_EPS) * sg_ref[...] * out_scale
            o_ref[:, hh * 2 * HD:(hh + 1) * 2 * HD] = od


def _attn_prompt(q, k, v, bias_near, lam, subln_g, out_scale, tq=256):
    bn, seq, _ = q.shape
    nq = seq // tq
    kv_spec = pl.BlockSpec((None, tq, C_W), lambda b, i, j: (b, jnp.minimum(i, j), 0))
    return pl.pallas_call(
        functools.partial(_attn_prompt_kernel, tq=tq, scale=HD ** -0.5, out_scale=out_scale),
        grid=(bn, nq, nq),
        in_specs=[pl.BlockSpec(memory_space=pltpu.SMEM),
                  pl.BlockSpec((None, tq, C_W), lambda b, i, j: (b, i, 0)),
                  kv_spec, kv_spec,
                  pl.BlockSpec(bias_near.shape, lambda b, i, j: (0, 0, 0, 0)),
                  pl.BlockSpec(subln_g.shape, lambda b, i, j: (0, 0))],
        out_specs=pl.BlockSpec((None, tq, C_W), lambda b, i, j: (b, i, 0)),
        out_shape=jax.ShapeDtypeStruct((bn, seq, C_W), F32),
        scratch_shapes=[pltpu.VMEM((N_HEADS, 2 * tq, 2 * HD), BF16), pltpu.VMEM((N_HEADS, 2 * tq, 2 * HD), F32),
                        pltpu.VMEM((N_HEADS, 2 * tq, 2 * HD), F32), pltpu.VMEM((N_HEADS, 2 * tq, 2 * HD), F32)],
        compiler_params=pltpu.CompilerParams(dimension_semantics=("parallel", "parallel", "arbitrary"),
                                             vmem_limit_bytes=V7X_VMEM_LIMIT),
        name="attn_prompt",
    )(lam, q, k, v, bias_near, subln_g)


def _attn_sample_kernel(pt_ref, lam_ref, q_ref, kn_ref, vn_ref, *rest, lq, scale, out_scale, ppb):
    del pt_ref
    kp_refs = rest[:ppb]
    vp_refs = rest[ppb:2 * ppb]
    bp_ref, bn_ref, sg_ref, o_ref, qz_ref, m_ref, l_ref, acc_ref = rest[2 * ppb:]
    p_idx = pl.program_id(1)

    @pl.when(p_idx == 0)
    def _():
        lane = _iota((lq, 2 * HD), 1)
        pieces = []
        for hh in range(N_HEADS):
            qh = q_ref[:, hh * 2 * HD:(hh + 1) * 2 * HD]
            pieces += [jnp.where(lane < HD, qh, 0.0), jnp.where(lane >= HD, qh, 0.0)]
        qz = jnp.concatenate(pieces, axis=0)
        qz_ref[...] = qz
        s = _mm_f32(qz, kn_ref[...], _NT) * scale + bn_ref[...]
        m = jnp.max(s, axis=-1, keepdims=True)
        p = jnp.exp(s - m)
        m_ref[...] = m
        l_ref[...] = jnp.sum(p, axis=-1, keepdims=True)
        acc_ref[...] = _mm_f32(p, vn_ref[...])

    qz = qz_ref[...].astype(BF16)
    cols = bp_ref.shape[1] // ppb
    scores = []
    for u in range(ppb):
        s = lax.dot_general(qz, kp_refs[u][...].astype(BF16), _NT, preferred_element_type=F32)
        scores.append(s * scale + bp_ref[:, u * cols:(u + 1) * cols])
    m_old = m_ref[...]
    m_new = m_old
    for s in scores:
        m_new = jnp.maximum(m_new, jnp.max(s, axis=-1, keepdims=True))
    alpha = jnp.exp(m_old - m_new)
    l_new = alpha * l_ref[...]
    acc = alpha * acc_ref[...]
    for u in range(ppb):
        p = jnp.exp(scores[u] - m_new)
        l_new = l_new + jnp.sum(p, axis=-1, keepdims=True)
        acc = acc + jnp.dot(p.astype(BF16), vp_refs[u][...].astype(BF16), preferred_element_type=F32)
    l_ref[...] = l_new
    acc_ref[...] = acc
    m_ref[...] = m_new

    @pl.when(p_idx == pl.num_programs(1) - 1)
    def _():
        lam = lam_ref[0]
        o = acc_ref[...] / l_ref[...]
        for hh in range(N_HEADS):
            o1 = o[(2 * hh) * lq:(2 * hh + 1) * lq]
            o2 = o[(2 * hh + 1) * lq:(2 * hh + 2) * lq]
            od = o1 - lam * o2
            od = od * lax.rsqrt(jnp.mean(od * od, axis=-1, keepdims=True) + NORM_EPS) * sg_ref[...] * out_scale
            o_ref[:, hh * 2 * HD:(hh + 1) * 2 * HD] = od


def _attn_sample(q, k_new, v_new, cache_k, cache_v, page_table, layer, bias_past, bias_new, lam, subln_g,
                 out_scale, ppb=16):
    bn, lq, _ = q.shape
    n_pages = page_table.shape[1]
    prow = cache_k.shape[2]
    rows = 2 * N_HEADS * lq
    seq_spec = pl.BlockSpec((None, lq, C_W), lambda b, p, pt: (b, 0, 0))
    new_spec = pl.BlockSpec((None, lq * N_HEADS, 2 * HD), lambda b, p, pt: (b, 0, 0))

    def page_spec(u):
        return pl.BlockSpec((None, None, prow, 2 * HD), lambda b, p, pt: (pt[b, p * ppb + u], layer, 0, 0))

    grid_spec = pltpu.PrefetchScalarGridSpec(
        num_scalar_prefetch=1,
        grid=(bn, n_pages // ppb),
        in_specs=[pl.BlockSpec(memory_space=pltpu.SMEM), seq_spec, new_spec, new_spec]
                 + [page_spec(u) for u in range(ppb)] * 2
                 + [pl.BlockSpec((rows, ppb * prow), lambda b, p, pt: (0, p)),
                    pl.BlockSpec(bias_new.shape, lambda b, p, pt: (0, 0)),
                    pl.BlockSpec(subln_g.shape, lambda b, p, pt: (0, 0))],
        out_specs=seq_spec,
        scratch_shapes=[pltpu.VMEM((rows, 2 * HD), F32), pltpu.VMEM((rows, 1), F32), pltpu.VMEM((rows, 1), F32),
                        pltpu.VMEM((rows, 2 * HD), F32)],
    )
    return pl.pallas_call(
        functools.partial(_attn_sample_kernel, lq=lq, scale=HD ** -0.5, out_scale=out_scale, ppb=ppb),
        grid_spec=grid_spec,
        out_shape=jax.ShapeDtypeStruct((bn, lq, C_W), F32),
        compiler_params=pltpu.CompilerParams(dimension_semantics=("parallel", "arbitrary")),
        name="attn_sample",
    )(page_table, lam, q, k_new, v_new, *([cache_k] * ppb), *([cache_v] * ppb), bias_past, bias_new, subln_g)


def _proj_out_kernel(x_ref, oa_ref, ob_ref, oc_ref, wo_ref, g_ref, wq_ref, xo_ref, xnt_ref, q_ref):
    dot = lambda a, w: jnp.dot(a.astype(BF16), w, preferred_element_type=F32)
    x = x_ref[...]
    x = x + dot(oa_ref[...], wo_ref[0:MIX_W, :]) + dot(ob_ref[...], wo_ref[MIX_W:2 * MIX_W, :]) \
        + dot(oc_ref[...], wo_ref[2 * MIX_W:, :])
    xo_ref[...] = x
    xn = x * lax.rsqrt(jnp.mean(x * x, axis=-1, keepdims=True) + NORM_EPS) * g_ref[...]
    xnt_ref[...] = jnp.transpose(xn).astype(BF16)
    q_ref[...] = jnp.dot(xn.astype(BF16), wq_ref[...], preferred_element_type=F32)


def _proj_out(x2d, o_a, o_b, o_c, w_out, g, wq, tm=512):
    t = x2d.shape[0]
    assert t % tm == 0, (t, tm)
    nq = wq.shape[1]
    row = lambda w: pl.BlockSpec((tm, w), lambda i: (i, 0))
    full = lambda a: pl.BlockSpec(a.shape, lambda i: (0,) * a.ndim)
    return pl.pallas_call(
        _proj_out_kernel,
        grid=(t // tm,),
        in_specs=[row(D_MODEL), row(MIX_W), row(MIX_W), row(C_W), full(w_out), full(g), full(wq)],
        out_specs=[row(D_MODEL), pl.BlockSpec((D_MODEL, tm), lambda i: (0, i)), row(nq)],
        out_shape=[jax.ShapeDtypeStruct((t, D_MODEL), F32), jax.ShapeDtypeStruct((D_MODEL, t), BF16),
                   jax.ShapeDtypeStruct((t, nq), F32)],
        compiler_params=pltpu.CompilerParams(dimension_semantics=("parallel",),
                                             vmem_limit_bytes=V7X_VMEM_LIMIT),
        name="proj_out",
    )(x2d, o_a, o_b, o_c, w_out, g, wq)


def _top16(s, rid=None):
    n_rows, tm = s.shape
    if rid is None:
        rid = _iota((n_rows, tm), 0).astype(F32)
    kid = _iota((PK_TOPK, tm), 0)
    vals = jnp.zeros((PK_TOPK, tm), F32)
    idxs = jnp.zeros((PK_TOPK, tm), F32)
    for kk in range(PK_TOPK):
        m = jnp.max(s, axis=0, keepdims=True)
        ix = jnp.min(jnp.where(s == m, rid, float(1 << 20)), axis=0, keepdims=True)
        vals = jnp.where(kid == kk, m, vals)
        idxs = jnp.where(kid == kk, ix, idxs)
        s = jnp.where(rid == ix, -jnp.inf, s)
    return vals, idxs


_CAND_A, _CAND_B = 4, 3


def _candidates(sv1, sv2):
    tm = sv1.shape[1]
    row16 = _iota((PK_TOPK, tm), 0)
    rows_a = [sv1[a:a + 1, :] + sv2 for a in range(_CAND_A)]
    rows_b = [jnp.where(row16 >= _CAND_A, sv1 + sv2[b:b + 1, :], -jnp.inf) for b in range(_CAND_B)]
    cand = jnp.concatenate(rows_a + rows_b, axis=0)
    rid_a = _iota((_CAND_A * PK_TOPK, tm), 0)
    row_b = _iota((_CAND_B * PK_TOPK, tm), 0)
    rid_b = jnp.bitwise_and(row_b, PK_TOPK - 1) * PK_TOPK + jnp.right_shift(row_b, 4)
    return cand, jnp.concatenate([rid_a, rid_b], axis=0).astype(F32)


def _top16_unique(s):
    n_rows, tm = s.shape
    kid = _iota((PK_TOPK, tm), 0)
    vals = jnp.zeros((PK_TOPK, tm), F32)
    rank = jnp.full((n_rows, tm), float(PK_TOPK), F32)
    for kk in range(PK_TOPK):
        m = jnp.max(s, axis=0, keepdims=True)
        hit = s == m
        vals = jnp.where(kid == kk, m, vals)
        rank = jnp.where(hit, float(kk), rank)
        s = jnp.where(hit, -jnp.inf, s)
    chosen = jnp.sum((rank < PK_TOPK).astype(F32), axis=0, keepdims=True)
    return vals, rank, chosen == PK_TOPK


def _route_tables_unique(s1, s2):
    tm = s1.shape[1]
    sv1, r1, ok1 = _top16_unique(s1)
    sv2, r2, ok2 = _top16_unique(s2)
    cand, _ = _candidates(sv1, sv2)
    tv, r3, ok3 = _top16_unique(cand)
    z = jnp.sum(jnp.exp(tv - tv[0:1, :]), axis=0, keepdims=True)
    chosen = (r3 < PK_TOPK).astype(F32)
    n_a = _CAND_A * PK_TOPK
    cnt = chosen[n_a:n_a + PK_TOPK]
    for b in range(1, _CAND_B):
        cnt = cnt + chosen[n_a + b * PK_TOPK:n_a + (b + 1) * PK_TOPK]
    row16 = _iota((PK_TOPK, tm), 0)
    for a in range(_CAND_A):
        cnt_a = jnp.sum(chosen[a * PK_TOPK:(a + 1) * PK_TOPK], axis=0, keepdims=True)
        cnt = jnp.where(row16 == a, cnt_a, cnt)
    f = jnp.zeros((PK_NKEYS, tm), F32)
    for a in range(PK_TOPK):
        f = jnp.where(r1 == float(a), cnt[a:a + 1, :], f)
    e1 = jnp.where(r1 < PK_TOPK, jnp.exp(s1 - sv1[0:1, :]), 0.0) / z
    e2 = jnp.where(r2 < PK_TOPK, jnp.exp(s2 - sv2[0:1, :]), 0.0)
    ok = jnp.logical_and(jnp.logical_and(ok1, ok2), ok3)
    return e1, f, e2, r2, ok


def _route_tables_exact(s1, s2):
    tm = s1.shape[1]
    sv1, si1 = _top16(s1)
    sv2, si2 = _top16(s2)
    cand, cand_ids = _candidates(sv1, sv2)
    tv, ti = _top16(cand, cand_ids)
    rank_a = jnp.floor(ti * (1.0 / PK_TOPK))
    z = jnp.sum(jnp.exp(tv - tv[0:1, :]), axis=0, keepdims=True)
    aid = _iota((PK_TOPK, tm), 0).astype(F32)
    cnt = jnp.zeros((PK_TOPK, tm), F32)
    for kk in range(PK_TOPK):
        cnt = cnt + (aid == rank_a[kk:kk + 1, :]).astype(F32)
    ex1 = jnp.exp(sv1 - sv1[0:1, :]) / z
    ex2 = jnp.exp(sv2 - sv2[0:1, :])
    kid = _iota((PK_NKEYS, tm), 0).astype(F32)
    e1 = jnp.zeros((PK_NKEYS, tm), F32)
    f = jnp.zeros((PK_NKEYS, tm), F32)
    e2 = jnp.zeros((PK_NKEYS, tm), F32)
    r = jnp.full((PK_NKEYS, tm), float(PK_TOPK), F32)
    for kk in range(PK_TOPK):
        hit1 = kid == si1[kk:kk + 1, :]
        e1 = jnp.where(hit1, ex1[kk:kk + 1, :], e1)
        f = jnp.where(hit1, cnt[kk:kk + 1, :], f)
        hit2 = kid == si2[kk:kk + 1, :]
        e2 = jnp.where(hit2, ex2[kk:kk + 1, :], e2)
        r = jnp.where(hit2, float(kk), r)
    return e1, f, e2, r


def _peer_route_kernel(q_ref, keys_ref, e1_ref, f_ref, e2_ref, r_ref):
    q = q_ref[...]
    s1 = _mm1(keys_ref[0], q[:, :PK_NKEYS], _NT)
    s2 = _mm1(keys_ref[1], q[:, PK_NKEYS:], _NT)

    def write(e1, f, e2, r):
        e1_ref[...] = 0.5 * e1
        f_ref[...] = f
        e2_ref[...] = e2.astype(BF16)
        r_ref[...] = r.astype(BF16)

    e1, f, e2, r, ok = _route_tables_unique(s1, s2)
    all_ok = jnp.min(ok.astype(F32)) > 0.5
    write(e1, f, e2, r)

    @pl.when(jnp.logical_not(all_ok))
    def _():
        write(*_route_tables_exact(s1, s2))


def _peer_route(q, keys, tm=512):
    t = q.shape[0]
    assert t % tm == 0, (t, tm)
    tab = jax.ShapeDtypeStruct((PK_HEADS, PK_NKEYS, t), F32)
    tab16 = jax.ShapeDtypeStruct((PK_HEADS, PK_NKEYS, t), BF16)
    tab_spec = pl.BlockSpec((None, PK_NKEYS, tm), lambda i, h: (h, 0, i))
    return pl.pallas_call(
        _peer_route_kernel,
        grid=(t // tm, PK_HEADS),
        in_specs=[pl.BlockSpec((tm, 2 * PK_NKEYS), lambda i, h: (i, h)),
                  pl.BlockSpec((None, 2, PK_NKEYS, PK_NKEYS), lambda i, h: (h, 0, 0, 0))],
        out_specs=[tab_spec] * 4,
        out_shape=[tab, tab, tab16, tab16],
        compiler_params=pltpu.CompilerParams(dimension_semantics=("parallel", "parallel")),
        name="peer_route",
    )(q, keys)


def _peer_dense_kernel(x_ref, xnt_ref, u_ref, vt_ref, e1_ref, f_ref, e2_ref, r_ref, o_ref, acc_ref, act_ref, c_ref,
                       *, ib, tm):
    ii = pl.program_id(1)

    @pl.when(ii == 0)
    def _():
        acc_ref[...] = jnp.zeros_like(acc_ref)

    pk = 16
    ng = PK_NKEYS // pk
    group = 2
    n_parts = ib // group
    n_cols = tm // 128
    half = tm // 2
    tile = lambda ref, hh, s, lanes: jnp.broadcast_to(ref[hh, s:s + 1, lanes], (pk, 128)).astype(BF16)

    def activations(part, n):
        rows = slice(part * group * PK_NKEYS, (part + 1) * group * PK_NKEYS)
        cols = slice(n * half, (n + 1) * half)
        act = jnp.dot(u_ref[rows, :], xnt_ref[:, cols], preferred_element_type=F32)
        act_ref[part * group * ng:(part + 1) * group * ng, :, cols] = act.reshape(group * ng, pk, half)

    def gates_of(part, c):
        lanes = slice(c * 128, (c + 1) * 128)
        s0 = part * group
        gates = [None] * group
        for hh in range(PK_HEADS):
            rank = r_ref[hh, :, :, lanes]
            e2 = e2_ref[hh, :, :, lanes]
            for ds in range(group):
                e1 = tile(e1_ref, hh, s0 + ds, lanes)
                f = tile(f_ref, hh, s0 + ds, lanes)
                term = jnp.maximum(jnp.minimum(f - rank, e1), 0.0) * e2
                gates[ds] = term if gates[ds] is None else gates[ds] + term
        for ds in range(group):
            rows = slice((s0 + ds) * ng, (s0 + ds + 1) * ng)
            a = act_ref[rows, :, lanes]
            a = a * (1.0 + lax.erf(a * math.sqrt(0.5)))
            c_ref[rows, :, lanes] = (gates[ds].astype(F32) * a).astype(BF16)

    activations(0, 0)
    activations(0, 1)
    for part in range(n_parts):
        for c in range(n_cols):
            if part + 1 < n_parts and c < 2:
                activations(part + 1, c)
            gates_of(part, c)
    acc_ref[...] += jnp.dot(vt_ref[...], c_ref[...].reshape(ib * PK_NKEYS, tm), preferred_element_type=F32)

    @pl.when(ii == pl.num_programs(1) - 1)
    def _():
        o_ref[...] = x_ref[...] + jnp.transpose(acc_ref[...])


def _peer_dense(x2d, xnt, u_bf, vt_bf, e1, f, e2, r, tm=512, ib=8):
    t = x2d.shape[0]
    assert t % tm == 0 and PK_NKEYS % ib == 0, (t, tm, ib)
    ni = PK_NKEYS // ib
    tab_i = pl.BlockSpec((PK_HEADS, ib, tm), lambda tt, ii: (0, ii, tt))
    pk = 16
    ng = PK_NKEYS // pk
    tab_j = pl.BlockSpec((PK_HEADS, ng, pk, tm), lambda tt, ii: (0, 0, 0, tt))
    e2 = e2.reshape(PK_HEADS, ng, pk, t)
    r = r.reshape(PK_HEADS, ng, pk, t)
    return pl.pallas_call(
        functools.partial(_peer_dense_kernel, ib=ib, tm=tm),
        grid=(t // tm, ni),
        in_specs=[pl.BlockSpec((tm, D_MODEL), lambda tt, ii: (tt, 0)),
                  pl.BlockSpec((D_MODEL, tm), lambda tt, ii: (0, tt)),
                  pl.BlockSpec((ib * PK_NKEYS, D_MODEL), lambda tt, ii: (ii, 0)),
                  pl.BlockSpec((D_MODEL, ib * PK_NKEYS), lambda tt, ii: (0, ii)),
                  tab_i, tab_i, tab_j, tab_j],
        out_specs=pl.BlockSpec((tm, D_MODEL), lambda tt, ii: (tt, 0)),
        out_shape=jax.ShapeDtypeStruct((t, D_MODEL), F32),
        scratch_shapes=[pltpu.VMEM((D_MODEL, tm), F32), pltpu.VMEM((ib * ng, pk, tm), F32),
                        pltpu.VMEM((ib * ng, pk, tm), BF16)],
        compiler_params=pltpu.CompilerParams(dimension_semantics=("parallel", "arbitrary"),
                                             vmem_limit_bytes=V7X_VMEM_LIMIT),
        name="peer_dense",
    )(x2d, xnt, u_bf, vt_bf, e1, f, e2, r)


def _t5_bucket(rel):
    n = jnp.maximum(-rel, 0)
    max_exact = REL_BUCKETS // 2
    nf = jnp.maximum(n, 1).astype(F32)
    large = max_exact + (jnp.log(nf / max_exact) / math.log(REL_MAX_DIST / max_exact)
                         * (REL_BUCKETS - max_exact)).astype(jnp.int32)
    large = jnp.minimum(large, REL_BUCKETS - 1)
    return jnp.where(n < max_exact, n, large)


def _bias_table(rel_bias, rel):
    bucket = _t5_bucket(rel)[None]
    col = lambda b: rel_bias[b].astype(F32).reshape((-1,) + (1,) * rel.ndim)
    out = jnp.broadcast_to(col(0), (rel_bias.shape[1],) + rel.shape)
    for b in range(1, REL_BUCKETS):
        out = jnp.where(bucket == b, col(b), out)
    return out


def _prompt_bias_tiles(rel_bias, tq):
    rel = jnp.arange(tq)[None, :] - jnp.arange(tq)[:, None]
    tiles = []
    for d in range(3):
        b = _bias_table(rel_bias, rel - d * tq)
        if d == 0:
            b = jnp.where(rel <= 0, b, -jnp.inf)
        tiles.append(jnp.concatenate([b, b], axis=1))
    return jnp.stack(tiles, axis=1)


def _prep_layer(l, W):
    row = lambda a: a.reshape(1, -1).astype(F32)
    w_in = W["w_in"][l]
    o_b = A_COLS
    o_c = A_COLS + 4 * MIX_W + 2 * N_HEADS
    w_cat = jnp.concatenate([
        w_in[:, :A_COLS],
        w_in[:, o_b:o_b + B_QKV],
        w_in[:, o_b + B_QKV + 2 * N_HEADS:o_c],
        w_in[:, o_b + B_QKV:o_b + B_QKV + 2 * N_HEADS],
        jnp.zeros((D_MODEL, 128 - 2 * N_HEADS), F32),
        w_in[:, o_c:],
    ], axis=1).astype(BF16)
    pad_lane = lambda a, off: jnp.zeros((1, 128), F32).at[0, off:off + a.shape[0]].set(a)
    alog = pad_lane(W["dn_a_log"][l], 0)
    dtb = pad_lane(W["dn_dt_bias"][l], 0)
    f32 = F32
    lam_init = 0.8 - 0.6 * math.exp(-0.3 * l)
    lam = (jnp.exp(jnp.sum(W["df_lq1"][l].astype(f32) * W["df_lk1"][l].astype(f32)))
           - jnp.exp(jnp.sum(W["df_lq2"][l].astype(f32) * W["df_lk2"][l].astype(f32))) + lam_init)
    return dict(
        rms_mix_g=row(W["rms_mix_g"][l]), w_cat=w_cat,
        qg=row(jnp.tile(W["df_qn_g"][l].reshape(-1), N_HEADS)),
        kg=row(jnp.tile(W["df_kn_g"][l].reshape(-1), N_HEADS)),
        rw_mu=row(W["rw_mu"][l]), rw_w0=row(W["rw_w0"][l]), rw_w2=W["rw_w2"][l], rw_a0=row(W["rw_a0"][l]),
        rw_a2=W["rw_a2"][l], rw_g2=W["rw_g2"][l], rw_kk=row(W["rw_kk"][l]), rw_ka=row(W["rw_ka"][l]),
        rw_rk=row(W["rw_rk"][l]), rw_ln_g=row(W["rw_ln_g"][l]), rw_ln_b=row(W["rw_ln_b"][l]),
        dn_conv_w=W["dn_conv_w"][l], dn_alog=alog, dn_dtb=dtb,
        dn_norm_g=row(jnp.tile(W["dn_norm_g"][l], N_HEADS)),
        lam=lam.reshape(1).astype(F32), lam_init=lam_init, subln_g=row(W["df_subln_g"][l]),
        w_out=W["w_out"][l].astype(BF16), rms_ffn_g=row(W["rms_ffn_g"][l]),
        pk_wq=W["pk_wq"][l].astype(BF16),
        pk_keys=W["pk_keys"][l], pk_u=W["pk_u"][l].astype(BF16), pk_vt=jnp.transpose(W["pk_v"][l]).astype(BF16),
    )


def _layer(x, shift_prev, s_rwkv, conv_prev, s_dn, attn_fn, P, c_len, bb):
    bn, seq, _ = x.shape
    t = bn * seq
    x2d = x.reshape(t, D_MODEL)
    h_a, h_b, q, k, v, k_by_head, v_by_head = _proj_in(x2d, P["rms_mix_g"], P["w_cat"], P["qg"], P["kg"])
    h_a3 = h_a.reshape(bn, seq, A_COLS)
    h_b3 = h_b.reshape(bn, seq, B_COLS_PAD)
    o_a, s_rwkv_new = _rwkv(h_a3, shift_prev, s_rwkv, P, c_len, bb)
    conv8 = jnp.concatenate([jnp.zeros((bn, 5, B_QKV), F32), conv_prev], axis=1)
    o_b, s_dn_new = _gdn(h_b3, conv8, s_dn, P["dn_conv_w"], P["dn_alog"], P["dn_dtb"], P["dn_norm_g"], c_len,
                         bb)
    q3, k3, v3 = (a.reshape(bn, seq, C_W) for a in (q, k, v))
    o_c = attn_fn(q3, k3, v3)
    x_new, xnt, pq = _proj_out(x2d, o_a.reshape(t, MIX_W), o_b.reshape(t, MIX_W), o_c.reshape(t, C_W),
                               P["w_out"], P["rms_ffn_g"], P["pk_wq"])
    e1, f, e2, r = _peer_route(pq, P["pk_keys"])
    y = _peer_dense(x_new, xnt, P["pk_u"], P["pk_vt"], e1, f, e2, r)
    k_rows = k_by_head.reshape(bn, seq, N_HEADS, 2 * HD)
    v_rows = v_by_head.reshape(bn, seq, N_HEADS, 2 * HD)
    shift_new = h_a3[:, -1]
    conv_new = h_b3[:, -3:, :B_QKV]
    return y.reshape(bn, seq, D_MODEL), k_rows, v_rows, s_rwkv_new, shift_new, s_dn_new, conv_new


def kernel(x_prompt, x_sample, cache_k, cache_v, state_rwkv, state_rwkv_shift, state_dn, state_dn_conv, page_table, rms_mix_g, w_in, w_out, rw_mu, rw_w0, rw_w2, rw_a0, rw_a2, rw_g2, rw_kk, rw_ka, rw_rk, rw_ln_g, rw_ln_b, dn_conv_w, dn_a_log, dn_dt_bias, dn_norm_g, df_qn_g, df_kn_g, df_lq1, df_lk1, df_lq2, df_lk2, df_subln_g, rel_bias, rms_ffn_g, pk_wq, pk_keys, pk_u, pk_v):
    W = dict(rms_mix_g=rms_mix_g, w_in=w_in, w_out=w_out, rw_mu=rw_mu, rw_w0=rw_w0, rw_w2=rw_w2, rw_a0=rw_a0,
             rw_a2=rw_a2, rw_g2=rw_g2, rw_kk=rw_kk, rw_ka=rw_ka, rw_rk=rw_rk, rw_ln_g=rw_ln_g, rw_ln_b=rw_ln_b,
             dn_conv_w=dn_conv_w, dn_a_log=dn_a_log, dn_dt_bias=dn_dt_bias, dn_norm_g=dn_norm_g,
             df_qn_g=df_qn_g, df_kn_g=df_kn_g, df_lq1=df_lq1, df_lk1=df_lk1, df_lq2=df_lq2, df_lk2=df_lk2,
             df_subln_g=df_subln_g, rms_ffn_g=rms_ffn_g, pk_wq=pk_wq, pk_keys=pk_keys, pk_u=pk_u, pk_v=pk_v)
    depth = w_in.shape[0]
    bp, seq_p, _ = x_prompt.shape
    bs, seq_s, _ = x_sample.shape
    n_pages = page_table.shape[1]
    page = cache_k.shape[2]
    past = n_pages * page
    tq = 256
    ck = cache_k.reshape(cache_k.shape[0], depth, page * N_HEADS, 2 * HD)
    cv = cache_v.reshape(cache_v.shape[0], depth, page * N_HEADS, 2 * HD)

    bias_near = _prompt_bias_tiles(rel_bias, tq)
    q_pos = past + jnp.arange(seq_s)
    same_head = jnp.eye(N_HEADS, dtype=bool)[:, None, None, :]

    def expand(b):
        bx = jnp.where(same_head, b[..., None], -jnp.inf)
        return jnp.broadcast_to(bx[:, None], (N_HEADS, 2) + bx.shape[1:]).reshape(2 * N_HEADS * seq_s, -1)

    bias_past = expand(_bias_table(rel_bias, jnp.arange(past)[None, :] - q_pos[:, None]))
    rel_new = q_pos[None, :] - q_pos[:, None]
    bias_new = expand(jnp.where(rel_new <= 0, _bias_table(rel_bias, rel_new), -jnp.inf))

    xp, xs = x_prompt, x_sample
    outs = [[] for _ in range(12)]
    zeros = lambda *s: jnp.zeros(s, F32)
    for l in range(depth):
        P = _prep_layer(l, W)
        out_scale = 1.0 - P["lam_init"]
        attn_p = lambda q, k, v: _attn_prompt(q, k, v, bias_near, P["lam"], P["subln_g"], out_scale, tq)
        by_head = lambda a: a.reshape(bs, seq_s * N_HEADS, 2 * HD)
        attn_s = lambda q, k, v: _attn_sample(q, by_head(k), by_head(v), ck, cv, page_table, l, bias_past, bias_new,
                                              P["lam"], P["subln_g"], out_scale)
        xp, kp, vp, sap, shp, sbp, cvp = _layer(
            xp, zeros(bp, A_COLS), zeros(bp, N_HEADS, HD, HD), zeros(bp, 3, B_QKV), zeros(bp, N_HEADS, HD, HD),
            attn_p, P, 64, 2)
        xs, ksn, vsn, sas, shs, sbs, cvs = _layer(
            xs, state_rwkv_shift[l], state_rwkv[l], state_dn_conv[l], state_dn[l], attn_s, P, seq_s, 8)
        for lst, val in zip(outs, (kp, vp, ksn, vsn, sap, sas, shp, shs, sbp, sbs, cvp, cvs)):
            lst.append(val)
    stack = lambda i, ax: jnp.stack(outs[i], axis=ax)
    return (xp, xs, stack(0, 1), stack(1, 1), stack(2, 1), stack(3, 1),
            stack(4, 0), stack(5, 0), stack(6, 0), stack(7, 0), stack(8, 0), stack(9, 0), stack(10, 0), stack(11, 0))
```

```python
import functools
import math

import jax
import jax.numpy as jnp
from jax import lax
from jax.experimental import pallas as pl
from jax.experimental.pallas import tpu as pltpu

F32 = jnp.float32
BF16 = jnp.bfloat16

D_MODEL = 1024
N_HEADS = 4
HD = 64
MIX_W = N_HEADS * HD
A_COLS = 1024
B_QKV = 3 * MIX_W
B_COLS_PAD = 1152
C_W = 512
NORM_EPS = 1e-6
A_LN_EPS = 64e-5
REL_BUCKETS = 32
REL_MAX_DIST = 128
PK_HEADS = 8
PK_NKEYS = 128
PK_TOPK = 16
V7X_VMEM_LIMIT = 56 * 1024 * 1024


def _parts(a, n):
    out = []
    r = a
    for i in range(n):
        p = r.astype(BF16)
        out.append(p)
        if i + 1 < n:
            r = r - p.astype(F32)
    return out


_NN = (((1,), (0,)), ((), ()))
_NT = (((1,), (1,)), ((), ()))
_TN = (((0,), (0,)), ((), ()))


def _mm(a, b, dims=_NN, na=2, nb=2, order=1):
    ap = _parts(a, na)
    bp = _parts(b, nb)
    acc = None
    for s in range(order, -1, -1):
        for i in range(na):
            j = s - i
            if 0 <= j < nb:
                t = lax.dot_general(ap[i], bp[j], dims, preferred_element_type=F32)
                acc = t if acc is None else acc + t
    return acc


def _mm1(a, b, dims=_NN):
    return _mm(a, b, dims, na=1, nb=1, order=0)


def _mm_f32(a, b, dims=_NN):
    return lax.dot_general(a, b, dims, precision=lax.Precision.HIGHEST, preferred_element_type=F32)


def _mm_mask_r(a, mask, dims=_NN):
    return _mm(a, mask, dims, na=2, nb=1, order=1)


def _mm_mask_l(mask, b, dims=_NN):
    return _mm(mask, b, dims, na=1, nb=3, order=2)


def _iota(shape, dim):
    return lax.broadcasted_iota(jnp.int32, shape, dim)


def _seg_ones(n, seg_shift):
    r = jnp.right_shift(_iota((n, n), 0), seg_shift)
    c = jnp.right_shift(_iota((n, n), 1), seg_shift)
    return (r == c).astype(F32)


def _tile4(z):
    return jnp.concatenate([z, z, z, z], axis=0)


def _fold4(y, c):
    return y[0:c] + y[c:2 * c] + y[2 * c:3 * c] + y[3 * c:4 * c]


def _neumann_inverse(n, steps):
    rows = n.shape[0]
    eye = (_iota((rows, rows), 0) == _iota((rows, rows), 1)).astype(F32)
    t = eye + n
    p = n
    for _ in range(steps - 1):
        p = _mm1(p, p)
        t = t + _mm1(t, p)
    return t


def _softplus(y):
    return jnp.maximum(y, 0.0) + jnp.log1p(jnp.exp(-jnp.abs(y)))


def _chunk_masks(c_len):
    r = 4 * c_len
    shift = int(math.log2(c_len))
    ri = _iota((r, r), 0)
    ci = _iota((r, r), 1)
    same = jnp.right_shift(ri, shift) == jnp.right_shift(ci, shift)
    incl = jnp.logical_and(same, ci <= ri)
    strict = jnp.logical_and(same, ci < ri)
    head = jnp.right_shift(_iota((r, MIX_W), 0), shift) == jnp.right_shift(_iota((r, MIX_W), 1), 6)
    return incl, strict, head


def _proj_in_kernel(x_ref, g_ref, w_ref, qg_ref, kg_ref, ha_ref, hb_ref, q_ref, k_ref, v_ref, kr_ref, vr_ref):
    x = x_ref[...]
    xn = x * lax.rsqrt(jnp.mean(x * x, axis=-1, keepdims=True) + NORM_EPS) * g_ref[...]
    h = jnp.dot(xn.astype(BF16), w_ref[...], preferred_element_type=F32)
    o = A_COLS
    ha_ref[...] = h[:, :o]
    hb_ref[...] = h[:, o:o + B_COLS_PAD]
    o += B_COLS_PAD
    seg = _seg_ones(C_W, 6)

    def qk_norm(t, g):
        ms = _mm_mask_r(t * t, seg) * (1.0 / HD)
        return t * lax.rsqrt(ms + NORM_EPS) * g

    q_ref[...] = qk_norm(h[:, o:o + C_W], qg_ref[...])
    k = qk_norm(h[:, o + C_W:o + 2 * C_W], kg_ref[...])
    v = h[:, o + 2 * C_W:o + 3 * C_W]
    k_ref[...] = k
    v_ref[...] = v
    tm = k.shape[0]
    for hh in range(N_HEADS):
        kr_ref[pl.ds(hh, tm, stride=N_HEADS), :] = k[:, hh * 2 * HD:(hh + 1) * 2 * HD]
        vr_ref[pl.ds(hh, tm, stride=N_HEADS), :] = v[:, hh * 2 * HD:(hh + 1) * 2 * HD]


def _proj_in(x2d, g, w_cat, qg, kg, tm=512):
    t = x2d.shape[0]
    assert t % tm == 0, (t, tm)
    n = w_cat.shape[1]
    row = lambda w: pl.BlockSpec((tm, w), lambda i: (i, 0))
    full = lambda a: pl.BlockSpec(a.shape, lambda i: (0,) * a.ndim)
    by_head = pl.BlockSpec((tm * N_HEADS, 2 * HD), lambda i: (i, 0))
    rows_shape = jax.ShapeDtypeStruct((t * N_HEADS, 2 * HD), F32)
    return pl.pallas_call(
        _proj_in_kernel,
        grid=(t // tm,),
        in_specs=[row(D_MODEL), full(g), full(w_cat), full(qg), full(kg)],
        out_specs=[row(A_COLS), row(B_COLS_PAD), row(C_W), row(C_W), row(C_W), by_head, by_head],
        out_shape=[jax.ShapeDtypeStruct((t, A_COLS), F32), jax.ShapeDtypeStruct((t, B_COLS_PAD), F32),
                   jax.ShapeDtypeStruct((t, C_W), F32), jax.ShapeDtypeStruct((t, C_W), F32),
                   jax.ShapeDtypeStruct((t, C_W), F32), rows_shape, rows_shape],
        compiler_params=pltpu.CompilerParams(dimension_semantics=("parallel",),
                                             vmem_limit_bytes=V7X_VMEM_LIMIT),
        name="proj_in",
    )(x2d, g, w_cat, qg, kg)


def _three_phase(seq_fn, refs, n_shared_from, n_shared_to, bb):
    c = pl.program_id(1)

    def per_seq(bi):
        return [r if n_shared_from <= n < n_shared_to else r.at[bi] for n, r in enumerate(refs)]

    @pl.when(c == 0)
    def _():
        for bi in range(bb):
            seq_fn("init", *per_seq(bi))

    for bi in range(bb):
        seq_fn("chunk", *per_seq(bi))

    @pl.when(c == pl.num_programs(1) - 1)
    def _():
        for bi in range(bb):
            seq_fn("final", *per_seq(bi))


def _rwkv_seq(phase, h_ref, shift_ref, s0_ref, mu_ref, w0_ref, w2_ref, a0_ref, a2_ref, g2_ref, kk_ref, ka_ref,
              rk_ref, lng_ref, lnb_ref, o_ref, sout_ref, sbd_ref, buf_ref, *, c_len):
    r_rows = 4 * c_len

    if phase == "init":
        sbd_ref[...] = jnp.zeros_like(sbd_ref)
        for hh in range(N_HEADS):
            sbd_ref[hh * HD:(hh + 1) * HD, hh * HD:(hh + 1) * HD] = s0_ref[hh]
        buf_ref[0:8, :] = jnp.broadcast_to(shift_ref[...], (8, A_COLS))
        return
    if phase == "final":
        for hh in range(N_HEADS):
            sout_ref[hh] = sbd_ref[hh * HD:(hh + 1) * HD, hh * HD:(hh + 1) * HD]
        return

    x = h_ref[...]
    buf_ref[8:8 + c_len, :] = x
    prev = buf_ref[7:7 + c_len, :]
    buf_ref[0:8, :] = x[c_len - 8:c_len, :]

    hs = x + (prev - x) * mu_ref[...]
    r = hs[:, 0:MIX_W]
    k = hs[:, MIX_W:2 * MIX_W]
    v = hs[:, 2 * MIX_W:3 * MIX_W]
    xw = hs[:, 768:832]
    xa = hs[:, 832:896]
    xg = hs[:, 896:1024]
    w = -_softplus(-(w0_ref[...] + _mm(jnp.tanh(xw), w2_ref[...]))) - 0.5
    a = jax.nn.sigmoid(a0_ref[...] + _mm(xa, a2_ref[...]))
    g = _mm(jax.nn.sigmoid(xg), g2_ref[...])
    seg = _seg_ones(MIX_W, 6)
    kkv = k * kk_ref[...]
    kkn = kkv * lax.rsqrt(_mm_mask_r(kkv * kkv, seg) + 1e-6)
    k2 = k * (1.0 + (a - 1.0) * ka_ref[...])
    logd = -jnp.exp(w)
    tri = (_iota((c_len, c_len), 1) <= _iota((c_len, c_len), 0)).astype(F32)
    cum = _mm_mask_l(tri, logd)
    p_in = jnp.exp(cum)
    p_inv = jnp.exp(-cum)
    p_prev = jnp.exp(cum - logd)

    incl, strict, head = _chunk_masks(c_len)
    hm = head.astype(F32)
    a_t = _tile4(-kkn * p_prev) * hm
    b_t = _tile4(kkn * a * p_inv) * hm
    k_t = _tile4(k2 * p_inv) * hm
    r_t = _tile4(r * p_in) * hm
    v_t = _tile4(v) * hm

    s0 = sbd_ref[...]
    ar = jnp.concatenate([a_t, r_t], axis=0)
    bk = jnp.concatenate([b_t, k_t], axis=0)
    gram = _mm1(ar, bk, _NT)
    h0 = _mm1(ar, s0, _NT)
    l_ab = jnp.where(strict, gram[:r_rows, :r_rows], 0.0)
    l_ak = jnp.where(strict, gram[:r_rows, r_rows:], 0.0)
    m_rb = jnp.where(incl, gram[r_rows:, :r_rows], 0.0)
    m_rk = jnp.where(incl, gram[r_rows:, r_rows:], 0.0)
    t_inv = _neumann_inverse(l_ab, int(math.log2(c_len)))
    u = _mm1(t_inv, h0[:r_rows] + _mm1(l_ak, v_t))
    uv = jnp.concatenate([u, v_t], axis=0)
    y = h0[r_rows:] + _mm1(jnp.concatenate([m_rb, m_rk], axis=1), uv)
    sbd_ref[...] = (s0 + _mm1(uv, bk, _TN)) * p_in[c_len - 1:c_len, :]

    y = _fold4(y, c_len)
    mean = _mm_mask_r(y, seg) * (1.0 / HD)
    yc = y - mean
    var = _mm_mask_r(yc * yc, seg) * (1.0 / HD)
    yn = yc * lax.rsqrt(var + A_LN_EPS) * lng_ref[...] + lnb_ref[...]
    bonus = _mm_mask_r(r * k2 * rk_ref[...], seg)
    o_ref[...] = (yn + bonus * v) * g


def _rwkv_kernel(*refs, c_len, bb):
    _three_phase(functools.partial(_rwkv_seq, c_len=c_len), refs, 3, 14, bb)


def _rwkv(h_a, shift_prev, s0, p, c_len, bb):
    bn, seq, _ = h_a.shape
    nc = seq // c_len
    params = [p[n] for n in ("rw_mu", "rw_w0", "rw_w2", "rw_a0", "rw_a2", "rw_g2", "rw_kk", "rw_ka", "rw_rk",
                             "rw_ln_g", "rw_ln_b")]
    full = lambda a: pl.BlockSpec(a.shape, lambda b, c: (0,) * a.ndim)
    return pl.pallas_call(
        functools.partial(_rwkv_kernel, c_len=c_len, bb=bb),
        grid=(bn // bb, nc),
        in_specs=[pl.BlockSpec((bb, c_len, A_COLS), lambda b, c: (b, c, 0)),
                  pl.BlockSpec((bb, 1, A_COLS), lambda b, c: (b, 0, 0)),
                  pl.BlockSpec((bb, N_HEADS, HD, HD), lambda b, c: (b, 0, 0, 0))] + [full(a) for a in params],
        out_specs=[pl.BlockSpec((bb, c_len, MIX_W), lambda b, c: (b, c, 0)),
                   pl.BlockSpec((bb, N_HEADS, HD, HD), lambda b, c: (b, 0, 0, 0))],
        out_shape=[jax.ShapeDtypeStruct((bn, seq, MIX_W), F32),
                   jax.ShapeDtypeStruct((bn, N_HEADS, HD, HD), F32)],
        scratch_shapes=[pltpu.VMEM((bb, MIX_W, MIX_W), F32), pltpu.VMEM((bb, 8 + c_len, A_COLS), F32)],
        compiler_params=pltpu.CompilerParams(dimension_semantics=("parallel", "arbitrary"),
                                             vmem_limit_bytes=V7X_VMEM_LIMIT),
        name="rwkv",
    )(h_a, shift_prev.reshape(bn, 1, A_COLS), s0, *params)


def _gdn_seq(phase, h_ref, conv0_ref, s0_ref, cw_ref, alog_ref, dtb_ref, ng_ref, o_ref, sout_ref, sbd_ref, buf_ref,
             *, c_len):
    r_rows = 4 * c_len

    if phase == "init":
        sbd_ref[...] = jnp.zeros_like(sbd_ref)
        for hh in range(N_HEADS):
            sbd_ref[hh * HD:(hh + 1) * HD, hh * HD:(hh + 1) * HD] = s0_ref[hh]
        buf_ref[0:8, :] = conv0_ref[...]
        return
    if phase == "final":
        for hh in range(N_HEADS):
            sout_ref[hh] = sbd_ref[hh * HD:(hh + 1) * HD, hh * HD:(hh + 1) * HD]
        return

    hb = h_ref[...]
    qkv = hb[:, :B_QKV]
    buf_ref[8:8 + c_len, :] = qkv
    cw = cw_ref[...]
    conv = cw[3:4, :] * qkv
    for j in range(3):
        conv = conv + cw[j:j + 1, :] * buf_ref[5 + j:5 + j + c_len, :]
    buf_ref[0:8, :] = qkv[c_len - 8:c_len, :]
    conv = conv * jax.nn.sigmoid(conv)
    seg = _seg_ones(MIX_W, 6)

    def l2n(t):
        return t * lax.rsqrt(_mm_mask_r(t * t, seg) + 1e-6)

    q = l2n(conv[:, :MIX_W]) * (HD ** -0.5)
    k = l2n(conv[:, MIX_W:2 * MIX_W])
    v = conv[:, 2 * MIX_W:]
    z = hb[:, B_QKV:B_QKV + MIX_W]
    ab = hb[:, B_QKV + MIX_W:]
    gfull = -jnp.exp(alog_ref[...]) * _softplus(ab + dtb_ref[...])
    bfull = jax.nn.sigmoid(ab)

    incl, strict, head = _chunk_masks(c_len)
    hm = head.astype(F32)
    shift = int(math.log2(c_len))
    hrow = jnp.right_shift(_iota((r_rows, 128), 0), shift)
    lane = _iota((r_rows, 128), 1)
    gcol = jnp.sum(jnp.where(lane == hrow, _tile4(gfull), 0.0), axis=1, keepdims=True)
    bcol = jnp.sum(jnp.where(lane == hrow + N_HEADS, _tile4(bfull), 0.0), axis=1, keepdims=True)

    incl_f = incl.astype(F32)
    strict_f = strict.astype(F32)
    dmat = _mm_mask_l(incl_f, jnp.broadcast_to(gcol, (r_rows, r_rows)) * strict_f)
    g128 = jnp.broadcast_to(gcol, (r_rows, 128))
    gc = _mm_mask_l(incl_f, g128)[:, 0:1]
    ri = _iota((r_rows, r_rows), 0)
    ci = _iota((r_rows, r_rows), 1)
    after_f = jnp.logical_and(jnp.right_shift(ri, shift) == jnp.right_shift(ci, shift), ci > ri).astype(F32)
    tail = _mm_mask_l(after_f, g128)[:, 0:1]
    e_gc = jnp.exp(gc)
    e_tail = jnp.exp(tail)
    first = jnp.bitwise_and(_iota((r_rows, MIX_W), 0), c_len - 1) == 0
    g_last = jnp.sum(jnp.where(first, jnp.exp(gc + tail) * hm, 0.0), axis=0, keepdims=True)

    q_h = _tile4(q) * hm
    k_h = _tile4(k) * hm
    v_h = _tile4(v) * hm
    kb = k_h * bcol
    e_d = jnp.exp(dmat)
    m1 = _mm1(jnp.concatenate([kb, q_h], axis=0), k_h, _NT)
    a_mat = jnp.where(strict, m1[:r_rows] * e_d, 0.0)
    att = jnp.where(incl, m1[r_rows:] * e_d, 0.0)
    t_inv = _neumann_inverse(-a_mat, int(math.log2(c_len)))
    uw = _mm1(t_inv, jnp.concatenate([v_h * bcol, kb * e_gc], axis=1))
    s0 = sbd_ref[...]
    v_new = uw[:, :MIX_W] - _mm1(uw[:, MIX_W:], s0)
    o = _mm1(q_h * e_gc, s0) + _mm1(att, v_new)
    sbd_ref[...] = s0 * g_last + _mm1(k_h * e_tail, v_new, _TN)

    o = _fold4(o, c_len)
    ms = _mm_mask_r(o * o, seg) * (1.0 / HD)
    o = o * lax.rsqrt(ms + NORM_EPS) * ng_ref[...]
    o_ref[...] = o * (z * jax.nn.sigmoid(z))


def _gdn_kernel(*refs, c_len, bb):
    _three_phase(functools.partial(_gdn_seq, c_len=c_len), refs, 3, 7, bb)


def _gdn(h_b, conv_prev8, s0, cw, alog, dtb, ng, c_len, bb):
    bn, seq, _ = h_b.shape
    nc = seq // c_len
    params = [cw, alog, dtb, ng]
    full = lambda a: pl.BlockSpec(a.shape, lambda b, c: (0,) * a.ndim)
    return pl.pallas_call(
        functools.partial(_gdn_kernel, c_len=c_len, bb=bb),
        grid=(bn // bb, nc),
        in_specs=[pl.BlockSpec((bb, c_len, B_COLS_PAD), lambda b, c: (b, c, 0)),
                  pl.BlockSpec((bb, 8, B_QKV), lambda b, c: (b, 0, 0)),
                  pl.BlockSpec((bb, N_HEADS, HD, HD), lambda b, c: (b, 0, 0, 0))] + [full(a) for a in params],
        out_specs=[pl.BlockSpec((bb, c_len, MIX_W), lambda b, c: (b, c, 0)),
                   pl.BlockSpec((bb, N_HEADS, HD, HD), lambda b, c: (b, 0, 0, 0))],
        out_shape=[jax.ShapeDtypeStruct((bn, seq, MIX_W), F32),
                   jax.ShapeDtypeStruct((bn, N_HEADS, HD, HD), F32)],
        scratch_shapes=[pltpu.VMEM((bb, MIX_W, MIX_W), F32), pltpu.VMEM((bb, 8 + c_len, B_QKV), F32)],
        compiler_params=pltpu.CompilerParams(dimension_semantics=("parallel", "arbitrary"),
                                             vmem_limit_bytes=V7X_VMEM_LIMIT),
        name="gdn",
    )(h_b, conv_prev8, s0, *params)


def _attn_prompt_kernel(lam_ref, q_ref, k_ref, v_ref, bias_ref, sg_ref, o_ref, q2_ref, m_ref, l_ref, acc_ref,
                        *, tq, scale, out_scale):
    i = pl.program_id(1)
    j = pl.program_id(2)

    @pl.when(j == 0)
    def _():
        m_ref[...] = jnp.full_like(m_ref, -jnp.inf)
        l_ref[...] = jnp.zeros_like(l_ref)
        acc_ref[...] = jnp.zeros_like(acc_ref)
        lane = _iota((tq, 2 * HD), 1)
        for hh in range(N_HEADS):
            qh = q_ref[:, hh * 2 * HD:(hh + 1) * 2 * HD] * scale
            q2_ref[hh, 0:tq, :] = jnp.where(lane < HD, qh, 0.0).astype(BF16)
            q2_ref[hh, tq:2 * tq, :] = jnp.where(lane >= HD, qh, 0.0).astype(BF16)

    @pl.when(j <= i)
    def _():
        for hh in range(N_HEADS):
            sl = slice(hh * 2 * HD, (hh + 1) * 2 * HD)
            s = lax.dot_general(q2_ref[hh], k_ref[:, sl].astype(BF16), _NT, preferred_element_type=F32)
            s = s + bias_ref[hh, jnp.minimum(i - j, 2)]
            m_old = m_ref[hh]
            m_new = jnp.maximum(m_old, jnp.max(s, axis=-1, keepdims=True))
            alpha = jnp.exp(m_old - m_new)
            p = jnp.exp(s - jnp.concatenate([m_new] * (tq // 128), axis=1))
            l_ref[hh] = alpha * l_ref[hh] + jnp.sum(p, axis=-1, keepdims=True)
            acc_ref[hh] = alpha * acc_ref[hh] + jnp.dot(p.astype(BF16), v_ref[:, sl].astype(BF16),
                                                        preferred_element_type=F32)
            m_ref[hh] = m_new

    @pl.when(j == i)
    def _():
        lam = lam_ref[0]
        for hh in range(N_HEADS):
            o = acc_ref[hh] / l_ref[hh]
            od = o[:tq] - lam * o[tq:]
            od = od * lax.rsqrt(jnp.mean(od * od, axis=-1, keepdims=True) + NORM_EPS) * sg_ref[...] * out_scale
            o_ref[:, hh * 2 * HD:(hh + 1) * 2 * HD] = od


def _attn_prompt(q, k, v, bias_near, lam, subln_g, out_scale, tq=256):
    bn, seq, _ = q.shape
    nq = seq // tq
    kv_spec = pl.BlockSpec((None, tq, C_W), lambda b, i, j: (b, jnp.minimum(i, j), 0))
    return pl.pallas_call(
        functools.partial(_attn_prompt_kernel, tq=tq, scale=HD ** -0.5, out_scale=out_scale),
        grid=(bn, nq, nq),
        in_specs=[pl.BlockSpec(memory_space=pltpu.SMEM),
                  pl.BlockSpec((None, tq, C_W), lambda b, i, j: (b, i, 0)),
                  kv_spec, kv_spec,
                  pl.BlockSpec(bias_near.shape, lambda b, i, j: (0, 0, 0, 0)),
                  pl.BlockSpec(subln_g.shape, lambda b, i, j: (0, 0))],
        out_specs=pl.BlockSpec((None, tq, C_W), lambda b, i, j: (b, i, 0)),
        out_shape=jax.ShapeDtypeStruct((bn, seq, C_W), F32),
        scratch_shapes=[pltpu.VMEM((N_HEADS, 2 * tq, 2 * HD), BF16), pltpu.VMEM((N_HEADS, 2 * tq, 2 * HD), F32),
                        pltpu.VMEM((N_HEADS, 2 * tq, 2 * HD), F32), pltpu.VMEM((N_HEADS, 2 * tq, 2 * HD), F32)],
        compiler_params=pltpu.CompilerParams(dimension_semantics=("parallel", "parallel", "arbitrary"),
                                             vmem_limit_bytes=V7X_VMEM_LIMIT),
        name="attn_prompt",
    )(lam, q, k, v, bias_near, subln_g)


def _attn_sample_kernel(pt_ref, lam_ref, q_ref, kn_ref, vn_ref, *rest, lq, scale, out_scale, ppb):
    del pt_ref
    kp_refs = rest[:ppb]
    vp_refs = rest[ppb:2 * ppb]
    bp_ref, bn_ref, sg_ref, o_ref, qz_ref, m_ref, l_ref, acc_ref = rest[2 * ppb:]
    p_idx = pl.program_id(1)

    @pl.when(p_idx == 0)
    def _():
        lane = _iota((lq, 2 * HD), 1)
        pieces = []
        for hh in range(N_HEADS):
            qh = q_ref[:, hh * 2 * HD:(hh + 1) * 2 * HD]
            pieces += [jnp.where(lane < HD, qh, 0.0), jnp.where(lane >= HD, qh, 0.0)]
        qz = jnp.concatenate(pieces, axis=0)
        qz_ref[...] = qz
        s = _mm_f32(qz, kn_ref[...], _NT) * scale + bn_ref[...]
        m = jnp.max(s, axis=-1, keepdims=True)
        p = jnp.exp(s - m)
        m_ref[...] = m
        l_ref[...] = jnp.sum(p, axis=-1, keepdims=True)
        acc_ref[...] = _mm_f32(p, vn_ref[...])

    qz = qz_ref[...].astype(BF16)
    cols = bp_ref.shape[1] // ppb
    scores = []
    for u in range(ppb):
        s = lax.dot_general(qz, kp_refs[u][...].astype(BF16), _NT, preferred_element_type=F32)
        scores.append(s * scale + bp_ref[:, u * cols:(u + 1) * cols])
    m_old = m_ref[...]
    m_new = m_old
    for s in scores:
        m_new = jnp.maximum(m_new, jnp.max(s, axis=-1, keepdims=True))
    alpha = jnp.exp(m_old - m_new)
    l_new = alpha * l_ref[...]
    acc = alpha * acc_ref[...]
    for u in range(ppb):
        p = jnp.exp(scores[u] - m_new)
        l_new = l_new + jnp.sum(p, axis=-1, keepdims=True)
        acc = acc + jnp.dot(p.astype(BF16), vp_refs[u][...].astype(BF16), preferred_element_type=F32)
    l_ref[...] = l_new
    acc_ref[...] = acc
    m_ref[...] = m_new

    @pl.when(p_idx == pl.num_programs(1) - 1)
    def _():
        lam = lam_ref[0]
        o = acc_ref[...] / l_ref[...]
        for hh in range(N_HEADS):
            o1 = o[(2 * hh) * lq:(2 * hh + 1) * lq]
            o2 = o[(2 * hh + 1) * lq:(2 * hh + 2) * lq]
            od = o1 - lam * o2
            od = od * lax.rsqrt(jnp.mean(od * od, axis=-1, keepdims=True) + NORM_EPS) * sg_ref[...] * out_scale
            o_ref[:, hh * 2 * HD:(hh + 1) * 2 * HD] = od


def _attn_sample(q, k_new, v_new, cache_k, cache_v, page_table, layer, bias_past, bias_new, lam, subln_g,
                 out_scale, ppb=16):
    bn, lq, _ = q.shape
    n_pages = page_table.shape[1]
    prow = cache_k.shape[2]
    rows = 2 * N_HEADS * lq
    seq_spec = pl.BlockSpec((None, lq, C_W), lambda b, p, pt: (b, 0, 0))
    new_spec = pl.BlockSpec((None, lq * N_HEADS, 2 * HD), lambda b, p, pt: (b, 0, 0))

    def page_spec(u):
        return pl.BlockSpec((None, None, prow, 2 * HD), lambda b, p, pt: (pt[b, p * ppb + u], layer, 0, 0))

    grid_spec = pltpu.PrefetchScalarGridSpec(
        num_scalar_prefetch=1,
        grid=(bn, n_pages // ppb),
        in_specs=[pl.BlockSpec(memory_space=pltpu.SMEM), seq_spec, new_spec, new_spec]
                 + [page_spec(u) for u in range(ppb)] * 2
                 + [pl.BlockSpec((rows, ppb * prow), lambda b, p, pt: (0, p)),
                    pl.BlockSpec(bias_new.shape, lambda b, p, pt: (0, 0)),
                    pl.BlockSpec(subln_g.shape, lambda b, p, pt: (0, 0))],
        out_specs=seq_spec,
        scratch_shapes=[pltpu.VMEM((rows, 2 * HD), F32), pltpu.VMEM((rows, 1), F32), pltpu.VMEM((rows, 1), F32),
                        pltpu.VMEM((rows, 2 * HD), F32)],
    )
    return pl.pallas_call(
        functools.partial(_attn_sample_kernel, lq=lq, scale=HD ** -0.5, out_scale=out_scale, ppb=ppb),
        grid_spec=grid_spec,
        out_shape=jax.ShapeDtypeStruct((bn, lq, C_W), F32),
        compiler_params=pltpu.CompilerParams(dimension_semantics=("parallel", "arbitrary")),
        name="attn_sample",
    )(page_table, lam, q, k_new, v_new, *([cache_k] * ppb), *([cache_v] * ppb), bias_past, bias_new, subln_g)


def _proj_out_kernel(x_ref, oa_ref, ob_ref, oc_ref, wo_ref, g_ref, wq_ref, xo_ref, xnt_ref, q_ref):
    dot = lambda a, w: jnp.dot(a.astype(BF16), w, preferred_element_type=F32)
    x = x_ref[...]
    x = x + dot(oa_ref[...], wo_ref[0:MIX_W, :]) + dot(ob_ref[...], wo_ref[MIX_W:2 * MIX_W, :]) \
        + dot(oc_ref[...], wo_ref[2 * MIX_W:, :])
    xo_ref[...] = x
    xn = x * lax.rsqrt(jnp.mean(x * x, axis=-1, keepdims=True) + NORM_EPS) * g_ref[...]
    xnt_ref[...] = jnp.transpose(xn).astype(BF16)
    q_ref[...] = jnp.dot(xn.astype(BF16), wq_ref[...], preferred_element_type=F32)


def _proj_out(x2d, o_a, o_b, o_c, w_out, g, wq, tm=512):
    t = x2d.shape[0]
    assert t % tm == 0, (t, tm)
    nq = wq.shape[1]
    row = lambda w: pl.BlockSpec((tm, w), lambda i: (i, 0))
    full = lambda a: pl.BlockSpec(a.shape, lambda i: (0,) * a.ndim)
    return pl.pallas_call(
        _proj_out_kernel,
        grid=(t // tm,),
        in_specs=[row(D_MODEL), row(MIX_W), row(MIX_W), row(C_W), full(w_out), full(g), full(wq)],
        out_specs=[row(D_MODEL), pl.BlockSpec((D_MODEL, tm), lambda i: (0, i)), row(nq)],
        out_shape=[jax.ShapeDtypeStruct((t, D_MODEL), F32), jax.ShapeDtypeStruct((D_MODEL, t), BF16),
                   jax.ShapeDtypeStruct((t, nq), F32)],
        compiler_params=pltpu.CompilerParams(dimension_semantics=("parallel",),
                                             vmem_limit_bytes=V7X_VMEM_LIMIT),
        name="proj_out",
    )(x2d, o_a, o_b, o_c, w_out, g, wq)


def _top16(s, rid=None):
    n_rows, tm = s.shape
    if rid is None:
        rid = _iota((n_rows, tm), 0).astype(F32)
    kid = _iota((PK_TOPK, tm), 0)
    vals = jnp.zeros((PK_TOPK, tm), F32)
    idxs = jnp.zeros((PK_TOPK, tm), F32)
    for kk in range(PK_TOPK):
        m = jnp.max(s, axis=0, keepdims=True)
        ix = jnp.min(jnp.where(s == m, rid, float(1 << 20)), axis=0, keepdims=True)
        vals = jnp.where(kid == kk, m, vals)
        idxs = jnp.where(kid == kk, ix, idxs)
        s = jnp.where(rid == ix, -jnp.inf, s)
    return vals, idxs


_CAND_A, _CAND_B = 4, 3


def _candidates(sv1, sv2):
    tm = sv1.shape[1]
    row16 = _iota((PK_TOPK, tm), 0)
    rows_a = [sv1[a:a + 1, :] + sv2 for a in range(_CAND_A)]
    rows_b = [jnp.where(row16 >= _CAND_A, sv1 + sv2[b:b + 1, :], -jnp.inf) for b in range(_CAND_B)]
    cand = jnp.concatenate(rows_a + rows_b, axis=0)
    rid_a = _iota((_CAND_A * PK_TOPK, tm), 0)
    row_b = _iota((_CAND_B * PK_TOPK, tm), 0)
    rid_b = jnp.bitwise_and(row_b, PK_TOPK - 1) * PK_TOPK + jnp.right_shift(row_b, 4)
    return cand, jnp.concatenate([rid_a, rid_b], axis=0).astype(F32)


def _top16_unique(s):
    n_rows, tm = s.shape
    kid = _iota((PK_TOPK, tm), 0)
    vals = jnp.zeros((PK_TOPK, tm), F32)
    rank = jnp.full((n_rows, tm), float(PK_TOPK), F32)
    for kk in range(PK_TOPK):
        m = jnp.max(s, axis=0, keepdims=True)
        hit = s == m
        vals = jnp.where(kid == kk, m, vals)
        rank = jnp.where(hit, float(kk), rank)
        s = jnp.where(hit, -jnp.inf, s)
    chosen = jnp.sum((rank < PK_TOPK).astype(F32), axis=0, keepdims=True)
    return vals, rank, chosen == PK_TOPK


def _route_tables_unique(s1, s2):
    tm = s1.shape[1]
    sv1, r1, ok1 = _top16_unique(s1)
    sv2, r2, ok2 = _top16_unique(s2)
    cand, _ = _candidates(sv1, sv2)
    tv, r3, ok3 = _top16_unique(cand)
    z = jnp.sum(jnp.exp(tv - tv[0:1, :]), axis=0, keepdims=True)
    chosen = (r3 < PK_TOPK).astype(F32)
    n_a = _CAND_A * PK_TOPK
    cnt = chosen[n_a:n_a + PK_TOPK]
    for b in range(1, _CAND_B):
        cnt = cnt + chosen[n_a + b * PK_TOPK:n_a + (b + 1) * PK_TOPK]
    row16 = _iota((PK_TOPK, tm), 0)
    for a in range(_CAND_A):
        cnt_a = jnp.sum(chosen[a * PK_TOPK:(a + 1) * PK_TOPK], axis=0, keepdims=True)
        cnt = jnp.where(row16 == a, cnt_a, cnt)
    f = jnp.zeros((PK_NKEYS, tm), F32)
    for a in range(PK_TOPK):
        f = jnp.where(r1 == float(a), cnt[a:a + 1, :], f)
    e1 = jnp.where(r1 < PK_TOPK, jnp.exp(s1 - sv1[0:1, :]), 0.0) / z
    e2 = jnp.where(r2 < PK_TOPK, jnp.exp(s2 - sv2[0:1, :]), 0.0)
    ok = jnp.logical_and(jnp.logical_and(ok1, ok2), ok3)
    return e1, f, e2, r2, ok


def _route_tables_exact(s1, s2):
    tm = s1.shape[1]
    sv1, si1 = _top16(s1)
    sv2, si2 = _top16(s2)
    cand, cand_ids = _candidates(sv1, sv2)
    tv, ti = _top16(cand, cand_ids)
    rank_a = jnp.floor(ti * (1.0 / PK_TOPK))
    z = jnp.sum(jnp.exp(tv - tv[0:1, :]), axis=0, keepdims=True)
    aid = _iota((PK_TOPK, tm), 0).astype(F32)
    cnt = jnp.zeros((PK_TOPK, tm), F32)
    for kk in range(PK_TOPK):
        cnt = cnt + (aid == rank_a[kk:kk + 1, :]).astype(F32)
    ex1 = jnp.exp(sv1 - sv1[0:1, :]) / z
    ex2 = jnp.exp(sv2 - sv2[0:1, :])
    kid = _iota((PK_NKEYS, tm), 0).astype(F32)
    e1 = jnp.zeros((PK_NKEYS, tm), F32)
    f = jnp.zeros((PK_NKEYS, tm), F32)
    e2 = jnp.zeros((PK_NKEYS, tm), F32)
    r = jnp.full((PK_NKEYS, tm), float(PK_TOPK), F32)
    for kk in range(PK_TOPK):
        hit1 = kid == si1[kk:kk + 1, :]
        e1 = jnp.where(hit1, ex1[kk:kk + 1, :], e1)
        f = jnp.where(hit1, cnt[kk:kk + 1, :], f)
        hit2 = kid == si2[kk:kk + 1, :]
        e2 = jnp.where(hit2, ex2[kk:kk + 1, :], e2)
        r = jnp.where(hit2, float(kk), r)
    return e1, f, e2, r


def _peer_route_kernel(q_ref, keys_ref, e1_ref, f_ref, e2_ref, r_ref):
    q = q_ref[...]
    s1 = _mm1(keys_ref[0], q[:, :PK_NKEYS], _NT)
    s2 = _mm1(keys_ref[1], q[:, PK_NKEYS:], _NT)

    def write(e1, f, e2, r):
        e1_ref[...] = 0.5 * e1
        f_ref[...] = f
        e2_ref[...] = e2.astype(BF16)
        r_ref[...] = r.astype(BF16)

    e1, f, e2, r, ok = _route_tables_unique(s1, s2)
    all_ok = jnp.min(ok.astype(F32)) > 0.5
    write(e1, f, e2, r)

    @pl.when(jnp.logical_not(all_ok))
    def _():
        write(*_route_tables_exact(s1, s2))


def _peer_route(q, keys, tm=512):
    t = q.shape[0]
    assert t % tm == 0, (t, tm)
    tab = jax.ShapeDtypeStruct((PK_HEADS, PK_NKEYS, t), F32)
    tab16 = jax.ShapeDtypeStruct((PK_HEADS, PK_NKEYS, t), BF16)
    tab_spec = pl.BlockSpec((None, PK_NKEYS, tm), lambda i, h: (h, 0, i))
    return pl.pallas_call(
        _peer_route_kernel,
        grid=(t // tm, PK_HEADS),
        in_specs=[pl.BlockSpec((tm, 2 * PK_NKEYS), lambda i, h: (i, h)),
                  pl.BlockSpec((None, 2, PK_NKEYS, PK_NKEYS), lambda i, h: (h, 0, 0, 0))],
        out_specs=[tab_spec] * 4,
        out_shape=[tab, tab, tab16, tab16],
        compiler_params=pltpu.CompilerParams(dimension_semantics=("parallel", "parallel")),
        name="peer_route",
    )(q, keys)


def _peer_dense_kernel(x_ref, xnt_ref, u_ref, vt_ref, e1_ref, f_ref, e2_ref, r_ref, o_ref, acc_ref, act_ref, c_ref,
                       *, ib, tm):
    ii = pl.program_id(1)

    @pl.when(ii == 0)
    def _():
        acc_ref[...] = jnp.zeros_like(acc_ref)

    pk = 16
    ng = PK_NKEYS // pk
    group = 2
    n_parts = ib // group
    n_cols = tm // 128
    half = tm // 2
    tile = lambda ref, hh, s, lanes: jnp.broadcast_to(ref[hh, s:s + 1, lanes], (pk, 128)).astype(BF16)

    def activations(part, n):
        rows = slice(part * group * PK_NKEYS, (part + 1) * group * PK_NKEYS)
        cols = slice(n * half, (n + 1) * half)
        act = jnp.dot(u_ref[rows, :], xnt_ref[:, cols], preferred_element_type=F32)
        act_ref[part * group * ng:(part + 1) * group * ng, :, cols] = act.reshape(group * ng, pk, half)

    def gates_of(part, c):
        lanes = slice(c * 128, (c + 1) * 128)
        s0 = part * group
        gates = [None] * group
        for hh in range(PK_HEADS):
            rank = r_ref[hh, :, :, lanes]
            e2 = e2_ref[hh, :, :, lanes]
            for ds in range(group):
                e1 = tile(e1_ref, hh, s0 + ds, lanes)
                f = tile(f_ref, hh, s0 + ds, lanes)
                term = jnp.maximum(jnp.minimum(f - rank, e1), 0.0) * e2
                gates[ds] = term if gates[ds] is None else gates[ds] + term
        for ds in range(group):
            rows = slice((s0 + ds) * ng, (s0 + ds + 1) * ng)
            a = act_ref[rows, :, lanes]
            a = a * (1.0 + lax.erf(a * math.sqrt(0.5)))
            c_ref[rows, :, lanes] = (gates[ds].astype(F32) * a).astype(BF16)

    activations(0, 0)
    activations(0, 1)
    for part in range(n_parts):
        for c in range(n_cols):
            if part + 1 < n_parts and c < 2:
                activations(part + 1, c)
            gates_of(part, c)
    acc_ref[...] += jnp.dot(vt_ref[...], c_ref[...].reshape(ib * PK_NKEYS, tm), preferred_element_type=F32)

    @pl.when(ii == pl.num_programs(1) - 1)
    def _():
        o_ref[...] = x_ref[...] + jnp.transpose(acc_ref[...])


def _peer_dense(x2d, xnt, u_bf, vt_bf, e1, f, e2, r, tm=512, ib=16):
    t = x2d.shape[0]
    assert t % tm == 0 and PK_NKEYS % ib == 0, (t, tm, ib)
    ni = PK_NKEYS // ib
    tab_i = pl.BlockSpec((PK_HEADS, ib, tm), lambda tt, ii: (0, ii, tt))
    pk = 16
    ng = PK_NKEYS // pk
    tab_j = pl.BlockSpec((PK_HEADS, ng, pk, tm), lambda tt, ii: (0, 0, 0, tt))
    e2 = e2.reshape(PK_HEADS, ng, pk, t)
    r = r.reshape(PK_HEADS, ng, pk, t)
    return pl.pallas_call(
        functools.partial(_peer_dense_kernel, ib=ib, tm=tm),
        grid=(t // tm, ni),
        in_specs=[pl.BlockSpec((tm, D_MODEL), lambda tt, ii: (tt, 0)),
                  pl.BlockSpec((D_MODEL, tm), lambda tt, ii: (0, tt)),
                  pl.BlockSpec((ib * PK_NKEYS, D_MODEL), lambda tt, ii: (ii, 0)),
                  pl.BlockSpec((D_MODEL, ib * PK_NKEYS), lambda tt, ii: (0, ii)),
                  tab_i, tab_i, tab_j, tab_j],
        out_specs=pl.BlockSpec((tm, D_MODEL), lambda tt, ii: (tt, 0)),
        out_shape=jax.ShapeDtypeStruct((t, D_MODEL), F32),
        scratch_shapes=[pltpu.VMEM((D_MODEL, tm), F32), pltpu.VMEM((ib * ng, pk, tm), F32),
                        pltpu.VMEM((ib * ng, pk, tm), BF16)],
        compiler_params=pltpu.CompilerParams(dimension_semantics=("parallel", "arbitrary"),
                                             vmem_limit_bytes=V7X_VMEM_LIMIT),
        name="peer_dense",
    )(x2d, xnt, u_bf, vt_bf, e1, f, e2, r)


def _t5_bucket(rel):
    n = jnp.maximum(-rel, 0)
    max_exact = REL_BUCKETS // 2
    nf = jnp.maximum(n, 1).astype(F32)
    large = max_exact + (jnp.log(nf / max_exact) / math.log(REL_MAX_DIST / max_exact)
                         * (REL_BUCKETS - max_exact)).astype(jnp.int32)
    large = jnp.minimum(large, REL_BUCKETS - 1)
    return jnp.where(n < max_exact, n, large)


def _bias_table(rel_bias, rel):
    bucket = _t5_bucket(rel)[None]
    col = lambda b: rel_bias[b].astype(F32).reshape((-1,) + (1,) * rel.ndim)
    out = jnp.broadcast_to(col(0), (rel_bias.shape[1],) + rel.shape)
    for b in range(1, REL_BUCKETS):
        out = jnp.where(bucket == b, col(b), out)
    return out


def _prompt_bias_tiles(rel_bias, tq):
    rel = jnp.arange(tq)[None, :] - jnp.arange(tq)[:, None]
    tiles = []
    for d in range(3):
        b = _bias_table(rel_bias, rel - d * tq)
        if d == 0:
            b = jnp.where(rel <= 0, b, -jnp.inf)
        tiles.append(jnp.concatenate([b, b], axis=1))
    return jnp.stack(tiles, axis=1)


def _prep_layer(l, W):
    row = lambda a: a.reshape(1, -1).astype(F32)
    w_in = W["w_in"][l]
    o_b = A_COLS
    o_c = A_COLS + 4 * MIX_W + 2 * N_HEADS
    w_cat = jnp.concatenate([
        w_in[:, :A_COLS],
        w_in[:, o_b:o_b + B_QKV],
        w_in[:, o_b + B_QKV + 2 * N_HEADS:o_c],
        w_in[:, o_b + B_QKV:o_b + B_QKV + 2 * N_HEADS],
        jnp.zeros((D_MODEL, 128 - 2 * N_HEADS), F32),
        w_in[:, o_c:],
    ], axis=1).astype(BF16)
    pad_lane = lambda a, off: jnp.zeros((1, 128), F32).at[0, off:off + a.shape[0]].set(a)
    alog = pad_lane(W["dn_a_log"][l], 0)
    dtb = pad_lane(W["dn_dt_bias"][l], 0)
    f32 = F32
    lam_init = 0.8 - 0.6 * math.exp(-0.3 * l)
    lam = (jnp.exp(jnp.sum(W["df_lq1"][l].astype(f32) * W["df_lk1"][l].astype(f32)))
           - jnp.exp(jnp.sum(W["df_lq2"][l].astype(f32) * W["df_lk2"][l].astype(f32))) + lam_init)
    return dict(
        rms_mix_g=row(W["rms_mix_g"][l]), w_cat=w_cat,
        qg=row(jnp.tile(W["df_qn_g"][l].reshape(-1), N_HEADS)),
        kg=row(jnp.tile(W["df_kn_g"][l].reshape(-1), N_HEADS)),
        rw_mu=row(W["rw_mu"][l]), rw_w0=row(W["rw_w0"][l]), rw_w2=W["rw_w2"][l], rw_a0=row(W["rw_a0"][l]),
        rw_a2=W["rw_a2"][l], rw_g2=W["rw_g2"][l], rw_kk=row(W["rw_kk"][l]), rw_ka=row(W["rw_ka"][l]),
        rw_rk=row(W["rw_rk"][l]), rw_ln_g=row(W["rw_ln_g"][l]), rw_ln_b=row(W["rw_ln_b"][l]),
        dn_conv_w=W["dn_conv_w"][l], dn_alog=alog, dn_dtb=dtb,
        dn_norm_g=row(jnp.tile(W["dn_norm_g"][l], N_HEADS)),
        lam=lam.reshape(1).astype(F32), lam_init=lam_init, subln_g=row(W["df_subln_g"][l]),
        w_out=W["w_out"][l].astype(BF16), rms_ffn_g=row(W["rms_ffn_g"][l]),
        pk_wq=W["pk_wq"][l].astype(BF16),
        pk_keys=W["pk_keys"][l], pk_u=W["pk_u"][l].astype(BF16), pk_vt=jnp.transpose(W["pk_v"][l]).astype(BF16),
    )


def _layer(x, shift_prev, s_rwkv, conv_prev, s_dn, attn_fn, P, c_len, bb):
    bn, seq, _ = x.shape
    t = bn * seq
    x2d = x.reshape(t, D_MODEL)
    h_a, h_b, q, k, v, k_by_head, v_by_head = _proj_in(x2d, P["rms_mix_g"], P["w_cat"], P["qg"], P["kg"])
    h_a3 = h_a.reshape(bn, seq, A_COLS)
    h_b3 = h_b.reshape(bn, seq, B_COLS_PAD)
    o_a, s_rwkv_new = _rwkv(h_a3, shift_prev, s_rwkv, P, c_len, bb)
    conv8 = jnp.concatenate([jnp.zeros((bn, 5, B_QKV), F32), conv_prev], axis=1)
    o_b, s_dn_new = _gdn(h_b3, conv8, s_dn, P["dn_conv_w"], P["dn_alog"], P["dn_dtb"], P["dn_norm_g"], c_len,
                         bb)
    q3, k3, v3 = (a.reshape(bn, seq, C_W) for a in (q, k, v))
    o_c = attn_fn(q3, k3, v3)
    x_new, xnt, pq = _proj_out(x2d, o_a.reshape(t, MIX_W), o_b.reshape(t, MIX_W), o_c.reshape(t, C_W),
                               P["w_out"], P["rms_ffn_g"], P["pk_wq"])
    e1, f, e2, r = _peer_route(pq, P["pk_keys"])
    y = _peer_dense(x_new, xnt, P["pk_u"], P["pk_vt"], e1, f, e2, r)
    k_rows = k_by_head.reshape(bn, seq, N_HEADS, 2 * HD)
    v_rows = v_by_head.reshape(bn, seq, N_HEADS, 2 * HD)
    shift_new = h_a3[:, -1]
    conv_new = h_b3[:, -3:, :B_QKV]
    return y.reshape(bn, seq, D_MODEL), k_rows, v_rows, s_rwkv_new, shift_new, s_dn_new, conv_new


def kernel(x_prompt, x_sample, cache_k, cache_v, state_rwkv, state_rwkv_shift, state_dn, state_dn_conv, page_table, rms_mix_g, w_in, w_out, rw_mu, rw_w0, rw_w2, rw_a0, rw_a2, rw_g2, rw_kk, rw_ka, rw_rk, rw_ln_g, rw_ln_b, dn_conv_w, dn_a_log, dn_dt_bias, dn_norm_g, df_qn_g, df_kn_g, df_lq1, df_lk1, df_lq2, df_lk2, df_subln_g, rel_bias, rms_ffn_g, pk_wq, pk_keys, pk_u, pk_v):
    W = dict(rms_mix_g=rms_mix_g, w_in=w_in, w_out=w_out, rw_mu=rw_mu, rw_w0=rw_w0, rw_w2=rw_w2, rw_a0=rw_a0,
             rw_a2=rw_a2, rw_g2=rw_g2, rw_kk=rw_kk, rw_ka=rw_ka, rw_rk=rw_rk, rw_ln_g=rw_ln_g, rw_ln_b=rw_ln_b,
             dn_conv_w=dn_conv_w, dn_a_log=dn_a_log, dn_dt_bias=dn_dt_bias, dn_norm_g=dn_norm_g,
             df_qn_g=df_qn_g, df_kn_g=df_kn_g, df_lq1=df_lq1, df_lk1=df_lk1, df_lq2=df_lq2, df_lk2=df_lk2,
             df_subln_g=df_subln_g, rms_ffn_g=rms_ffn_g, pk_wq=pk_wq, pk_keys=pk_keys, pk_u=pk_u, pk_v=pk_v)
    depth = w_in.shape[0]
    bp, seq_p, _ = x_prompt.shape
    bs, seq_s, _ = x_sample.shape
    n_pages = page_table.shape[1]
    page = cache_k.shape[2]
    past = n_pages * page
    tq = 256
    ck = cache_k.reshape(cache_k.shape[0], depth, page * N_HEADS, 2 * HD)
    cv = cache_v.reshape(cache_v.shape[0], depth, page * N_HEADS, 2 * HD)

    bias_near = _prompt_bias_tiles(rel_bias, tq)
    q_pos = past + jnp.arange(seq_s)
    same_head = jnp.eye(N_HEADS, dtype=bool)[:, None, None, :]

    def expand(b):
        bx = jnp.where(same_head, b[..., None], -jnp.inf)
        return jnp.broadcast_to(bx[:, None], (N_HEADS, 2) + bx.shape[1:]).reshape(2 * N_HEADS * seq_s, -1)

    bias_past = expand(_bias_table(rel_bias, jnp.arange(past)[None, :] - q_pos[:, None]))
    rel_new = q_pos[None, :] - q_pos[:, None]
    bias_new = expand(jnp.where(rel_new <= 0, _bias_table(rel_bias, rel_new), -jnp.inf))

    xp, xs = x_prompt, x_sample
    outs = [[] for _ in range(12)]
    zeros = lambda *s: jnp.zeros(s, F32)
    for l in range(depth):
        P = _prep_layer(l, W)
        out_scale = 1.0 - P["lam_init"]
        attn_p = lambda q, k, v: _attn_prompt(q, k, v, bias_near, P["lam"], P["subln_g"], out_scale, tq)
        by_head = lambda a: a.reshape(bs, seq_s * N_HEADS, 2 * HD)
        attn_s = lambda q, k, v: _attn_sample(q, by_head(k), by_head(v), ck, cv, page_table, l, bias_past, bias_new,
                                              P["lam"], P["subln_g"], out_scale)
        xp, kp, vp, sap, shp, sbp, cvp = _layer(
            xp, zeros(bp, A_COLS), zeros(bp, N_HEADS, HD, HD), zeros(bp, 3, B_QKV), zeros(bp, N_HEADS, HD, HD),
            attn_p, P, 64, 2)
        xs, ksn, vsn, sas, shs, sbs, cvs = _layer(
            xs, state_rwkv_shift[l], state_rwkv[l], state_dn_conv[l], state_dn[l], attn_s, P, seq_s, 8)
        for lst, val in zip(outs, (kp, vp, ksn, vsn, sap, sas, shp, shs, sbp, sbs, cvp, cvs)):
            lst.append(val)
    stack = lambda i, ax: jnp.stack(outs[i], axis=ax)
    return (xp, xs, stack(0, 1), stack(1, 1), stack(2, 1), stack(3, 1),
            stack(4, 0), stack(5, 0), stack(6, 0), stack(7, 0), stack(8, 0), stack(9, 0), stack(10, 0), stack(11, 0))
```

```python
import functools
import math

import jax
import jax.numpy as jnp
from jax import lax
from jax.experimental import pallas as pl
from jax.experimental.pallas import tpu as pltpu

F32 = jnp.float32
BF16 = jnp.bfloat16

D_MODEL = 1024
N_HEADS = 4
HD = 64
MIX_W = N_HEADS * HD
A_COLS = 1024
B_QKV = 3 * MIX_W
B_COLS_PAD = 1152
C_W = 512
NORM_EPS = 1e-6
A_LN_EPS = 64e-5
REL_BUCKETS = 32
REL_MAX_DIST = 128
PK_HEADS = 8
PK_NKEYS = 128
PK_TOPK = 16
V7X_VMEM_LIMIT = 56 * 1024 * 1024


def _parts(a, n):
    out = []
    r = a
    for i in range(n):
        p = r.astype(BF16)
        out.append(p)
        if i + 1 < n:
            r = r - p.astype(F32)
    return out


_NN = (((1,), (0,)), ((), ()))
_NT = (((1,), (1,)), ((), ()))
_TN = (((0,), (0,)), ((), ()))


def _mm(a, b, dims=_NN, na=2, nb=2, order=1):
    ap = _parts(a, na)
    bp = _parts(b, nb)
    acc = None
    for s in range(order, -1, -1):
        for i in range(na):
            j = s - i
            if 0 <= j < nb:
                t = lax.dot_general(ap[i], bp[j], dims, preferred_element_type=F32)
                acc = t if acc is None else acc + t
    return acc


def _mm1(a, b, dims=_NN):
    return _mm(a, b, dims, na=1, nb=1, order=0)


def _mm_f32(a, b, dims=_NN):
    return lax.dot_general(a, b, dims, precision=lax.Precision.HIGHEST, preferred_element_type=F32)


def _mm_mask_r(a, mask, dims=_NN):
    return _mm(a, mask, dims, na=2, nb=1, order=1)


def _mm_mask_l(mask, b, dims=_NN):
    return _mm(mask, b, dims, na=1, nb=3, order=2)


def _iota(shape, dim):
    return lax.broadcasted_iota(jnp.int32, shape, dim)


def _seg_ones(n, seg_shift):
    r = jnp.right_shift(_iota((n, n), 0), seg_shift)
    c = jnp.right_shift(_iota((n, n), 1), seg_shift)
    return (r == c).astype(F32)


def _tile4(z):
    return jnp.concatenate([z, z, z, z], axis=0)


def _fold4(y, c):
    return y[0:c] + y[c:2 * c] + y[2 * c:3 * c] + y[3 * c:4 * c]


def _neumann_inverse(n, steps):
    rows = n.shape[0]
    eye = (_iota((rows, rows), 0) == _iota((rows, rows), 1)).astype(F32)
    t = eye + n
    p = n
    for _ in range(steps - 1):
        p = _mm1(p, p)
        t = t + _mm1(t, p)
    return t


def _softplus(y):
    return jnp.maximum(y, 0.0) + jnp.log1p(jnp.exp(-jnp.abs(y)))


def _chunk_masks(c_len):
    r = 4 * c_len
    shift = int(math.log2(c_len))
    ri = _iota((r, r), 0)
    ci = _iota((r, r), 1)
    same = jnp.right_shift(ri, shift) == jnp.right_shift(ci, shift)
    incl = jnp.logical_and(same, ci <= ri)
    strict = jnp.logical_and(same, ci < ri)
    head = jnp.right_shift(_iota((r, MIX_W), 0), shift) == jnp.right_shift(_iota((r, MIX_W), 1), 6)
    return incl, strict, head


def _proj_in_kernel(x_ref, g_ref, w_ref, qg_ref, kg_ref, ha_ref, hb_ref, q_ref, k_ref, v_ref, kr_ref, vr_ref):
    x = x_ref[...]
    xn = x * lax.rsqrt(jnp.mean(x * x, axis=-1, keepdims=True) + NORM_EPS) * g_ref[...]
    h = jnp.dot(xn.astype(BF16), w_ref[...], preferred_element_type=F32)
    o = A_COLS
    ha_ref[...] = h[:, :o]
    hb_ref[...] = h[:, o:o + B_COLS_PAD]
    o += B_COLS_PAD
    seg = _seg_ones(C_W, 6)

    def qk_norm(t, g):
        ms = _mm_mask_r(t * t, seg) * (1.0 / HD)
        return t * lax.rsqrt(ms + NORM_EPS) * g

    q_ref[...] = qk_norm(h[:, o:o + C_W], qg_ref[...])
    k = qk_norm(h[:, o + C_W:o + 2 * C_W], kg_ref[...])
    v = h[:, o + 2 * C_W:o + 3 * C_W]
    k_ref[...] = k
    v_ref[...] = v
    tm = k.shape[0]
    for hh in range(N_HEADS):
        kr_ref[pl.ds(hh, tm, stride=N_HEADS), :] = k[:, hh * 2 * HD:(hh + 1) * 2 * HD]
        vr_ref[pl.ds(hh, tm, stride=N_HEADS), :] = v[:, hh * 2 * HD:(hh + 1) * 2 * HD]


def _proj_in(x2d, g, w_cat, qg, kg, tm=512):
    t = x2d.shape[0]
    assert t % tm == 0, (t, tm)
    n = w_cat.shape[1]
    row = lambda w: pl.BlockSpec((tm, w), lambda i: (i, 0))
    full = lambda a: pl.BlockSpec(a.shape, lambda i: (0,) * a.ndim)
    by_head = pl.BlockSpec((tm * N_HEADS, 2 * HD), lambda i: (i, 0))
    rows_shape = jax.ShapeDtypeStruct((t * N_HEADS, 2 * HD), F32)
    return pl.pallas_call(
        _proj_in_kernel,
        grid=(t // tm,),
        in_specs=[row(D_MODEL), full(g), full(w_cat), full(qg), full(kg)],
        out_specs=[row(A_COLS), row(B_COLS_PAD), row(C_W), row(C_W), row(C_W), by_head, by_head],
        out_shape=[jax.ShapeDtypeStruct((t, A_COLS), F32), jax.ShapeDtypeStruct((t, B_COLS_PAD), F32),
                   jax.ShapeDtypeStruct((t, C_W), F32), jax.ShapeDtypeStruct((t, C_W), F32),
                   jax.ShapeDtypeStruct((t, C_W), F32), rows_shape, rows_shape],
        compiler_params=pltpu.CompilerParams(dimension_semantics=("parallel",),
                                             vmem_limit_bytes=V7X_VMEM_LIMIT),
        name="proj_in",
    )(x2d, g, w_cat, qg, kg)


def _three_phase(seq_fn, refs, n_shared_from, n_shared_to, bb):
    c = pl.program_id(1)

    def per_seq(bi):
        return [r if n_shared_from <= n < n_shared_to else r.at[bi] for n, r in enumerate(refs)]

    @pl.when(c == 0)
    def _():
        for bi in range(bb):
            seq_fn("init", *per_seq(bi))

    for bi in range(bb):
        seq_fn("chunk", *per_seq(bi))

    @pl.when(c == pl.num_programs(1) - 1)
    def _():
        for bi in range(bb):
            seq_fn("final", *per_seq(bi))


def _rwkv_seq(phase, h_ref, shift_ref, s0_ref, mu_ref, w0_ref, w2_ref, a0_ref, a2_ref, g2_ref, kk_ref, ka_ref,
              rk_ref, lng_ref, lnb_ref, o_ref, sout_ref, sbd_ref, buf_ref, *, c_len):
    r_rows = 4 * c_len

    if phase == "init":
        sbd_ref[...] = jnp.zeros_like(sbd_ref)
        for hh in range(N_HEADS):
            sbd_ref[hh * HD:(hh + 1) * HD, hh * HD:(hh + 1) * HD] = s0_ref[hh]
        buf_ref[0:8, :] = jnp.broadcast_to(shift_ref[...], (8, A_COLS))
        return
    if phase == "final":
        for hh in range(N_HEADS):
            sout_ref[hh] = sbd_ref[hh * HD:(hh + 1) * HD, hh * HD:(hh + 1) * HD]
        return

    x = h_ref[...]
    buf_ref[8:8 + c_len, :] = x
    prev = buf_ref[7:7 + c_len, :]
    buf_ref[0:8, :] = x[c_len - 8:c_len, :]

    hs = x + (prev - x) * mu_ref[...]
    r = hs[:, 0:MIX_W]
    k = hs[:, MIX_W:2 * MIX_W]
    v = hs[:, 2 * MIX_W:3 * MIX_W]
    xw = hs[:, 768:832]
    xa = hs[:, 832:896]
    xg = hs[:, 896:1024]
    w = -_softplus(-(w0_ref[...] + _mm(jnp.tanh(xw), w2_ref[...]))) - 0.5
    a = jax.nn.sigmoid(a0_ref[...] + _mm(xa, a2_ref[...]))
    g = _mm(jax.nn.sigmoid(xg), g2_ref[...])
    seg = _seg_ones(MIX_W, 6)
    kkv = k * kk_ref[...]
    kkn = kkv * lax.rsqrt(_mm_mask_r(kkv * kkv, seg) + 1e-6)
    k2 = k * (1.0 + (a - 1.0) * ka_ref[...])
    logd = -jnp.exp(w)
    tri = (_iota((c_len, c_len), 1) <= _iota((c_len, c_len), 0)).astype(F32)
    cum = _mm_mask_l(tri, logd)
    p_in = jnp.exp(cum)
    p_inv = jnp.exp(-cum)
    p_prev = jnp.exp(cum - logd)

    incl, strict, head = _chunk_masks(c_len)
    hm = head.astype(F32)
    a_t = _tile4(-kkn * p_prev) * hm
    b_t = _tile4(kkn * a * p_inv) * hm
    k_t = _tile4(k2 * p_inv) * hm
    r_t = _tile4(r * p_in) * hm
    v_t = _tile4(v) * hm

    s0 = sbd_ref[...]
    ar = jnp.concatenate([a_t, r_t], axis=0)
    bk = jnp.concatenate([b_t, k_t], axis=0)
    gram = _mm1(ar, bk, _NT)
    h0 = _mm1(ar, s0, _NT)
    l_ab = jnp.where(strict, gram[:r_rows, :r_rows], 0.0)
    l_ak = jnp.where(strict, gram[:r_rows, r_rows:], 0.0)
    m_rb = jnp.where(incl, gram[r_rows:, :r_rows], 0.0)
    m_rk = jnp.where(incl, gram[r_rows:, r_rows:], 0.0)
    t_inv = _neumann_inverse(l_ab, int(math.log2(c_len)))
    u = _mm1(t_inv, h0[:r_rows] + _mm1(l_ak, v_t))
    uv = jnp.concatenate([u, v_t], axis=0)
    y = h0[r_rows:] + _mm1(jnp.concatenate([m_rb, m_rk], axis=1), uv)
    sbd_ref[...] = (s0 + _mm1(uv, bk, _TN)) * p_in[c_len - 1:c_len, :]

    y = _fold4(y, c_len)
    mean = _mm_mask_r(y, seg) * (1.0 / HD)
    yc = y - mean
    var = _mm_mask_r(yc * yc, seg) * (1.0 / HD)
    yn = yc * lax.rsqrt(var + A_LN_EPS) * lng_ref[...] + lnb_ref[...]
    bonus = _mm_mask_r(r * k2 * rk_ref[...], seg)
    o_ref[...] = (yn + bonus * v) * g


def _rwkv_kernel(*refs, c_len, bb):
    _three_phase(functools.partial(_rwkv_seq, c_len=c_len), refs, 3, 14, bb)


def _rwkv(h_a, shift_prev, s0, p, c_len, bb):
    bn, seq, _ = h_a.shape
    nc = seq // c_len
    params = [p[n] for n in ("rw_mu", "rw_w0", "rw_w2", "rw_a0", "rw_a2", "rw_g2", "rw_kk", "rw_ka", "rw_rk",
                             "rw_ln_g", "rw_ln_b")]
    full = lambda a: pl.BlockSpec(a.shape, lambda b, c: (0,) * a.ndim)
    return pl.pallas_call(
        functools.partial(_rwkv_kernel, c_len=c_len, bb=bb),
        grid=(bn // bb, nc),
        in_specs=[pl.BlockSpec((bb, c_len, A_COLS), lambda b, c: (b, c, 0)),
                  pl.BlockSpec((bb, 1, A_COLS), lambda b, c: (b, 0, 0)),
                  pl.BlockSpec((bb, N_HEADS, HD, HD), lambda b, c: (b, 0, 0, 0))] + [full(a) for a in params],
        out_specs=[pl.BlockSpec((bb, c_len, MIX_W), lambda b, c: (b, c, 0)),
                   pl.BlockSpec((bb, N_HEADS, HD, HD), lambda b, c: (b, 0, 0, 0))],
        out_shape=[jax.ShapeDtypeStruct((bn, seq, MIX_W), F32),
                   jax.ShapeDtypeStruct((bn, N_HEADS, HD, HD), F32)],
        scratch_shapes=[pltpu.VMEM((bb, MIX_W, MIX_W), F32), pltpu.VMEM((bb, 8 + c_len, A_COLS), F32)],
        compiler_params=pltpu.CompilerParams(dimension_semantics=("parallel", "arbitrary"),
                                             vmem_limit_bytes=V7X_VMEM_LIMIT),
        name="rwkv",
    )(h_a, shift_prev.reshape(bn, 1, A_COLS), s0, *params)


def _gdn_seq(phase, h_ref, conv0_ref, s0_ref, cw_ref, alog_ref, dtb_ref, ng_ref, o_ref, sout_ref, sbd_ref, buf_ref,
             *, c_len):
    r_rows = 4 * c_len

    if phase == "init":
        sbd_ref[...] = jnp.zeros_like(sbd_ref)
        for hh in range(N_HEADS):
            sbd_ref[hh * HD:(hh + 1) * HD, hh * HD:(hh + 1) * HD] = s0_ref[hh]
        buf_ref[0:8, :] = conv0_ref[...]
        return
    if phase == "final":
        for hh in range(N_HEADS):
            sout_ref[hh] = sbd_ref[hh * HD:(hh + 1) * HD, hh * HD:(hh + 1) * HD]
        return

    hb = h_ref[...]
    qkv = hb[:, :B_QKV]
    buf_ref[8:8 + c_len, :] = qkv
    cw = cw_ref[...]
    conv = cw[3:4, :] * qkv
    for j in range(3):
        conv = conv + cw[j:j + 1, :] * buf_ref[5 + j:5 + j + c_len, :]
    buf_ref[0:8, :] = qkv[c_len - 8:c_len, :]
    conv = conv * jax.nn.sigmoid(conv)
    seg = _seg_ones(MIX_W, 6)

    def l2n(t):
        return t * lax.rsqrt(_mm_mask_r(t * t, seg) + 1e-6)

    q = l2n(conv[:, :MIX_W]) * (HD ** -0.5)
    k = l2n(conv[:, MIX_W:2 * MIX_W])
    v = conv[:, 2 * MIX_W:]
    z = hb[:, B_QKV:B_QKV + MIX_W]
    ab = hb[:, B_QKV + MIX_W:]
    gfull = -jnp.exp(alog_ref[...]) * _softplus(ab + dtb_ref[...])
    bfull = jax.nn.sigmoid(ab)

    incl, strict, head = _chunk_masks(c_len)
    hm = head.astype(F32)
    shift = int(math.log2(c_len))
    hrow = jnp.right_shift(_iota((r_rows, 128), 0), shift)
    lane = _iota((r_rows, 128), 1)
    gcol = jnp.sum(jnp.where(lane == hrow, _tile4(gfull), 0.0), axis=1, keepdims=True)
    bcol = jnp.sum(jnp.where(lane == hrow + N_HEADS, _tile4(bfull), 0.0), axis=1, keepdims=True)

    incl_f = incl.astype(F32)
    strict_f = strict.astype(F32)
    dmat = _mm_mask_l(incl_f, jnp.broadcast_to(gcol, (r_rows, r_rows)) * strict_f)
    g128 = jnp.broadcast_to(gcol, (r_rows, 128))
    gc = _mm_mask_l(incl_f, g128)[:, 0:1]
    ri = _iota((r_rows, r_rows), 0)
    ci = _iota((r_rows, r_rows), 1)
    after_f = jnp.logical_and(jnp.right_shift(ri, shift) == jnp.right_shift(ci, shift), ci > ri).astype(F32)
    tail = _mm_mask_l(after_f, g128)[:, 0:1]
    e_gc = jnp.exp(gc)
    e_tail = jnp.exp(tail)
    first = jnp.bitwise_and(_iota((r_rows, MIX_W), 0), c_len - 1) == 0
    g_last = jnp.sum(jnp.where(first, jnp.exp(gc + tail) * hm, 0.0), axis=0, keepdims=True)

    q_h = _tile4(q) * hm
    k_h = _tile4(k) * hm
    v_h = _tile4(v) * hm
    kb = k_h * bcol
    e_d = jnp.exp(dmat)
    m1 = _mm1(jnp.concatenate([kb, q_h], axis=0), k_h, _NT)
    a_mat = jnp.where(strict, m1[:r_rows] * e_d, 0.0)
    att = jnp.where(incl, m1[r_rows:] * e_d, 0.0)
    t_inv = _neumann_inverse(-a_mat, int(math.log2(c_len)))
    uw = _mm1(t_inv, jnp.concatenate([v_h * bcol, kb * e_gc], axis=1))
    s0 = sbd_ref[...]
    v_new = uw[:, :MIX_W] - _mm1(uw[:, MIX_W:], s0)
    o = _mm1(q_h * e_gc, s0) + _mm1(att, v_new)
    sbd_ref[...] = s0 * g_last + _mm1(k_h * e_tail, v_new, _TN)

    o = _fold4(o, c_len)
    ms = _mm_mask_r(o * o, seg) * (1.0 / HD)
    o = o * lax.rsqrt(ms + NORM_EPS) * ng_ref[...]
    o_ref[...] = o * (z * jax.nn.sigmoid(z))


def _gdn_kernel(*refs, c_len, bb):
    _three_phase(functools.partial(_gdn_seq, c_len=c_len), refs, 3, 7, bb)


def _gdn(h_b, conv_prev8, s0, cw, alog, dtb, ng, c_len, bb):
    bn, seq, _ = h_b.shape
    nc = seq // c_len
    params = [cw, alog, dtb, ng]
    full = lambda a: pl.BlockSpec(a.shape, lambda b, c: (0,) * a.ndim)
    return pl.pallas_call(
        functools.partial(_gdn_kernel, c_len=c_len, bb=bb),
        grid=(bn // bb, nc),
        in_specs=[pl.BlockSpec((bb, c_len, B_COLS_PAD), lambda b, c: (b, c, 0)),
                  pl.BlockSpec((bb, 8, B_QKV), lambda b, c: (b, 0, 0)),
                  pl.BlockSpec((bb, N_HEADS, HD, HD), lambda b, c: (b, 0, 0, 0))] + [full(a) for a in params],
        out_specs=[pl.BlockSpec((bb, c_len, MIX_W), lambda b, c: (b, c, 0)),
                   pl.BlockSpec((bb, N_HEADS, HD, HD), lambda b, c: (b, 0, 0, 0))],
        out_shape=[jax.ShapeDtypeStruct((bn, seq, MIX_W), F32),
                   jax.ShapeDtypeStruct((bn, N_HEADS, HD, HD), F32)],
        scratch_shapes=[pltpu.VMEM((bb, MIX_W, MIX_W), F32), pltpu.VMEM((bb, 8 + c_len, B_QKV), F32)],
        compiler_params=pltpu.CompilerParams(dimension_semantics=("parallel", "arbitrary"),
                                             vmem_limit_bytes=V7X_VMEM_LIMIT),
        name="gdn",
    )(h_b, conv_prev8, s0, *params)


def _attn_prompt_kernel(lam_ref, q_ref, k_ref, v_ref, bias_ref, sg_ref, o_ref, q2_ref, m_ref, l_ref, acc_ref,
                        *, tq, scale, out_scale):
    i = pl.program_id(1)
    j = pl.program_id(2)

    @pl.when(j == 0)
    def _():
        m_ref[...] = jnp.full_like(m_ref, -jnp.inf)
        l_ref[...] = jnp.zeros_like(l_ref)
        acc_ref[...] = jnp.zeros_like(acc_ref)
        lane = _iota((tq, 2 * HD), 1)
        for hh in range(N_HEADS):
            qh = q_ref[:, hh * 2 * HD:(hh + 1) * 2 * HD] * scale
            q2_ref[hh, 0:tq, :] = jnp.where(lane < HD, qh, 0.0).astype(BF16)
            q2_ref[hh, tq:2 * tq, :] = jnp.where(lane >= HD, qh, 0.0).astype(BF16)

    @pl.when(j <= i)
    def _():
        for hh in range(N_HEADS):
            sl = slice(hh * 2 * HD, (hh + 1) * 2 * HD)
            s = lax.dot_general(q2_ref[hh], k_ref[:, sl].astype(BF16), _NT, preferred_element_type=F32)
            s = s + bias_ref[hh, jnp.minimum(i - j, 2)]
            m_old = m_ref[hh]
            m_new = jnp.maximum(m_old, jnp.max(s, axis=-1, keepdims=True))
            alpha = jnp.exp(m_old - m_new)
            p = jnp.exp(s - jnp.concatenate([m_new] * (tq // 128), axis=1))
            l_ref[hh] = alpha * l_ref[hh] + jnp.sum(p, axis=-1, keepdims=True)
            acc_ref[hh] = alpha * acc_ref[hh] + jnp.dot(p.astype(BF16), v_ref[:, sl].astype(BF16),
                                                        preferred_element_type=F32)
            m_ref[hh] = m_new

    @pl.when(j == i)
    def _():
        lam = lam_ref[0]
        for hh in range(N_HEADS):
            o = acc_ref[hh] / l_ref[hh]
            od = o[:tq] - lam * o[tq:]
            od = od * lax.rsqrt(jnp.mean(od * od, axis=-1, keepdims=True) + NORM_EPS) * sg_ref[...] * out_scale
            o_ref[:, hh * 2 * HD:(hh + 1) * 2 * HD] = od


def _attn_prompt(q, k, v, bias_near, lam, subln_g, out_scale, tq=256):
    bn, seq, _ = q.shape
    nq = seq // tq
    kv_spec = pl.BlockSpec((None, tq, C_W), lambda b, i, j: (b, jnp.minimum(i, j), 0))
    return pl.pallas_call(
        functools.partial(_attn_prompt_kernel, tq=tq, scale=HD ** -0.5, out_scale=out_scale),
        grid=(bn, nq, nq),
        in_specs=[pl.BlockSpec(memory_space=pltpu.SMEM),
                  pl.BlockSpec((None, tq, C_W), lambda b, i, j: (b, i, 0)),
                  kv_spec, kv_spec,
                  pl.BlockSpec(bias_near.shape, lambda b, i, j: (0, 0, 0, 0)),
                  pl.BlockSpec(subln_g.shape, lambda b, i, j: (0, 0))],
        out_specs=pl.BlockSpec((None, tq, C_W), lambda b, i, j: (b, i, 0)),
        out_shape=jax.ShapeDtypeStruct((bn, seq, C_W), F32),
        scratch_shapes=[pltpu.VMEM((N_HEADS, 2 * tq, 2 * HD), BF16), pltpu.VMEM((N_HEADS, 2 * tq, 2 * HD), F32),
                        pltpu.VMEM((N_HEADS, 2 * tq, 2 * HD), F32), pltpu.VMEM((N_HEADS, 2 * tq, 2 * HD), F32)],
        compiler_params=pltpu.CompilerParams(dimension_semantics=("parallel", "parallel", "arbitrary"),
                                             vmem_limit_bytes=V7X_VMEM_LIMIT),
        name="attn_prompt",
    )(lam, q, k, v, bias_near, subln_g)


def _attn_sample_kernel(pt_ref, lam_ref, q_ref, kn_ref, vn_ref, *rest, lq, scale, out_scale, ppb):
    del pt_ref
    kp_refs = rest[:ppb]
    vp_refs = rest[ppb:2 * ppb]
    bp_ref, bn_ref, sg_ref, o_ref, qz_ref, m_ref, l_ref, acc_ref = rest[2 * ppb:]
    p_idx = pl.program_id(1)

    @pl.when(p_idx == 0)
    def _():
        lane = _iota((lq, 2 * HD), 1)
        pieces = []
        for hh in range(N_HEADS):
            qh = q_ref[:, hh * 2 * HD:(hh + 1) * 2 * HD]
            pieces += [jnp.where(lane < HD, qh, 0.0), jnp.where(lane >= HD, qh, 0.0)]
        qz = jnp.concatenate(pieces, axis=0)
        qz_ref[...] = qz
        s = _mm_f32(qz, kn_ref[...], _NT) * scale + bn_ref[...]
        m = jnp.max(s, axis=-1, keepdims=True)
        p = jnp.exp(s - m)
        m_ref[...] = m
        l_ref[...] = jnp.sum(p, axis=-1, keepdims=True)
        acc_ref[...] = _mm_f32(p, vn_ref[...])

    qz = qz_ref[...].astype(BF16)
    cols = bp_ref.shape[1] // ppb
    scores = []
    for u in range(ppb):
        s = lax.dot_general(qz, kp_refs[u][...].astype(BF16), _NT, preferred_element_type=F32)
        scores.append(s * scale + bp_ref[:, u * cols:(u + 1) * cols])
    m_old = m_ref[...]
    m_new = m_old
    for s in scores:
        m_new = jnp.maximum(m_new, jnp.max(s, axis=-1, keepdims=True))
    alpha = jnp.exp(m_old - m_new)
    l_new = alpha * l_ref[...]
    acc = alpha * acc_ref[...]
    for u in range(ppb):
        p = jnp.exp(scores[u] - m_new)
        l_new = l_new + jnp.sum(p, axis=-1, keepdims=True)
        acc = acc + jnp.dot(p.astype(BF16), vp_refs[u][...].astype(BF16), preferred_element_type=F32)
    l_ref[...] = l_new
    acc_ref[...] = acc
    m_ref[...] = m_new

    @pl.when(p_idx == pl.num_programs(1) - 1)
    def _():
        lam = lam_ref[0]
        o = acc_ref[...] / l_ref[...]
        for hh in range(N_HEADS):
            o1 = o[(2 * hh) * lq:(2 * hh + 1) * lq]
            o2 = o[(2 * hh + 1) * lq:(2 * hh + 2) * lq]
            od = o1 - lam * o2
            od = od * lax.rsqrt(jnp.mean(od * od, axis=-1, keepdims=True) + NORM_EPS) * sg_ref[...] * out_scale
            o_ref[:, hh * 2 * HD:(hh + 1) * 2 * HD] = od


def _attn_sample(q, k_new, v_new, cache_k, cache_v, page_table, layer, bias_past, bias_new, lam, subln_g,
                 out_scale, ppb=16):
    bn, lq, _ = q.shape
    n_pages = page_table.shape[1]
    prow = cache_k.shape[2]
    rows = 2 * N_HEADS * lq
    seq_spec = pl.BlockSpec((None, lq, C_W), lambda b, p, pt: (b, 0, 0))
    new_spec = pl.BlockSpec((None, lq * N_HEADS, 2 * HD), lambda b, p, pt: (b, 0, 0))

    def page_spec(u):
        return pl.BlockSpec((None, None, prow, 2 * HD), lambda b, p, pt: (pt[b, p * ppb + u], layer, 0, 0))

    grid_spec = pltpu.PrefetchScalarGridSpec(
        num_scalar_prefetch=1,
        grid=(bn, n_pages // ppb),
        in_specs=[pl.BlockSpec(memory_space=pltpu.SMEM), seq_spec, new_spec, new_spec]
                 + [page_spec(u) for u in range(ppb)] * 2
                 + [pl.BlockSpec((rows, ppb * prow), lambda b, p, pt: (0, p)),
                    pl.BlockSpec(bias_new.shape, lambda b, p, pt: (0, 0)),
                    pl.BlockSpec(subln_g.shape, lambda b, p, pt: (0, 0))],
        out_specs=seq_spec,
        scratch_shapes=[pltpu.VMEM((rows, 2 * HD), F32), pltpu.VMEM((rows, 1), F32), pltpu.VMEM((rows, 1), F32),
                        pltpu.VMEM((rows, 2 * HD), F32)],
    )
    return pl.pallas_call(
        functools.partial(_attn_sample_kernel, lq=lq, scale=HD ** -0.5, out_scale=out_scale, ppb=ppb),
        grid_spec=grid_spec,
        out_shape=jax.ShapeDtypeStruct((bn, lq, C_W), F32),
        compiler_params=pltpu.CompilerParams(dimension_semantics=("parallel", "arbitrary")),
        name="attn_sample",
    )(page_table, lam, q, k_new, v_new, *([cache_k] * ppb), *([cache_v] * ppb), bias_past, bias_new, subln_g)


def _proj_out_kernel(x_ref, oa_ref, ob_ref, oc_ref, wo_ref, g_ref, wq_ref, xo_ref, xnt_ref, q_ref):
    dot = lambda a, w: jnp.dot(a.astype(BF16), w, preferred_element_type=F32)
    x = x_ref[...]
    x = x + dot(oa_ref[...], wo_ref[0:MIX_W, :]) + dot(ob_ref[...], wo_ref[MIX_W:2 * MIX_W, :]) \
        + dot(oc_ref[...], wo_ref[2 * MIX_W:, :])
    xo_ref[...] = x
    xn = x * lax.rsqrt(jnp.mean(x * x, axis=-1, keepdims=True) + NORM_EPS) * g_ref[...]
    xnt_ref[...] = jnp.transpose(xn).astype(BF16)
    q_ref[...] = jnp.dot(xn.astype(BF16), wq_ref[...], preferred_element_type=F32)


def _proj_out(x2d, o_a, o_b, o_c, w_out, g, wq, tm=512):
    t = x2d.shape[0]
    assert t % tm == 0, (t, tm)
    nq = wq.shape[1]
    row = lambda w: pl.BlockSpec((tm, w), lambda i: (i, 0))
    full = lambda a: pl.BlockSpec(a.shape, lambda i: (0,) * a.ndim)
    return pl.pallas_call(
        _proj_out_kernel,
        grid=(t // tm,),
        in_specs=[row(D_MODEL), row(MIX_W), row(MIX_W), row(C_W), full(w_out), full(g), full(wq)],
        out_specs=[row(D_MODEL), pl.BlockSpec((D_MODEL, tm), lambda i: (0, i)), row(nq)],
        out_shape=[jax.ShapeDtypeStruct((t, D_MODEL), F32), jax.ShapeDtypeStruct((D_MODEL, t), BF16),
                   jax.ShapeDtypeStruct((t, nq), F32)],
        compiler_params=pltpu.CompilerParams(dimension_semantics=("parallel",),
                                             vmem_limit_bytes=V7X_VMEM_LIMIT),
        name="proj_out",
    )(x2d, o_a, o_b, o_c, w_out, g, wq)


def _top16(s, rid=None):
    n_rows, tm = s.shape
    if rid is None:
        rid = _iota((n_rows, tm), 0).astype(F32)
    kid = _iota((PK_TOPK, tm), 0)
    vals = jnp.zeros((PK_TOPK, tm), F32)
    idxs = jnp.zeros((PK_TOPK, tm), F32)
    for kk in range(PK_TOPK):
        m = jnp.max(s, axis=0, keepdims=True)
        ix = jnp.min(jnp.where(s == m, rid, float(1 << 20)), axis=0, keepdims=True)
        vals = jnp.where(kid == kk, m, vals)
        idxs = jnp.where(kid == kk, ix, idxs)
        s = jnp.where(rid == ix, -jnp.inf, s)
    return vals, idxs


_CAND_A, _CAND_B = 4, 3


def _candidates(sv1, sv2):
    tm = sv1.shape[1]
    row16 = _iota((PK_TOPK, tm), 0)
    rows_a = [sv1[a:a + 1, :] + sv2 for a in range(_CAND_A)]
    rows_b = [jnp.where(row16 >= _CAND_A, sv1 + sv2[b:b + 1, :], -jnp.inf) for b in range(_CAND_B)]
    cand = jnp.concatenate(rows_a + rows_b, axis=0)
    rid_a = _iota((_CAND_A * PK_TOPK, tm), 0)
    row_b = _iota((_CAND_B * PK_TOPK, tm), 0)
    rid_b = jnp.bitwise_and(row_b, PK_TOPK - 1) * PK_TOPK + jnp.right_shift(row_b, 4)
    return cand, jnp.concatenate([rid_a, rid_b], axis=0).astype(F32)


def _top16_unique(s):
    n_rows, tm = s.shape
    kid = _iota((PK_TOPK, tm), 0)
    vals = jnp.zeros((PK_TOPK, tm), F32)
    rank = jnp.full((n_rows, tm), float(PK_TOPK), F32)
    for kk in range(PK_TOPK):
        m = jnp.max(s, axis=0, keepdims=True)
        hit = s == m
        vals = jnp.where(kid == kk, m, vals)
        rank = jnp.where(hit, float(kk), rank)
        s = jnp.where(hit, -jnp.inf, s)
    chosen = jnp.sum((rank < PK_TOPK).astype(F32), axis=0, keepdims=True)
    return vals, rank, chosen == PK_TOPK


def _route_tables_unique(s1, s2):
    tm = s1.shape[1]
    sv1, r1, ok1 = _top16_unique(s1)
    sv2, r2, ok2 = _top16_unique(s2)
    cand, _ = _candidates(sv1, sv2)
    tv, r3, ok3 = _top16_unique(cand)
    z = jnp.sum(jnp.exp(tv - tv[0:1, :]), axis=0, keepdims=True)
    chosen = (r3 < PK_TOPK).astype(F32)
    n_a = _CAND_A * PK_TOPK
    cnt = chosen[n_a:n_a + PK_TOPK]
    for b in range(1, _CAND_B):
        cnt = cnt + chosen[n_a + b * PK_TOPK:n_a + (b + 1) * PK_TOPK]
    row16 = _iota((PK_TOPK, tm), 0)
    for a in range(_CAND_A):
        cnt_a = jnp.sum(chosen[a * PK_TOPK:(a + 1) * PK_TOPK], axis=0, keepdims=True)
        cnt = jnp.where(row16 == a, cnt_a, cnt)
    f = jnp.zeros((PK_NKEYS, tm), F32)
    for a in range(PK_TOPK):
        f = jnp.where(r1 == float(a), cnt[a:a + 1, :], f)
    e1 = jnp.where(r1 < PK_TOPK, jnp.exp(s1 - sv1[0:1, :]), 0.0) / z
    e2 = jnp.where(r2 < PK_TOPK, jnp.exp(s2 - sv2[0:1, :]), 0.0)
    ok = jnp.logical_and(jnp.logical_and(ok1, ok2), ok3)
    return e1, f, e2, r2, ok


def _route_tables_exact(s1, s2):
    tm = s1.shape[1]
    sv1, si1 = _top16(s1)
    sv2, si2 = _top16(s2)
    cand, cand_ids = _candidates(sv1, sv2)
    tv, ti = _top16(cand, cand_ids)
    rank_a = jnp.floor(ti * (1.0 / PK_TOPK))
    z = jnp.sum(jnp.exp(tv - tv[0:1, :]), axis=0, keepdims=True)
    aid = _iota((PK_TOPK, tm), 0).astype(F32)
    cnt = jnp.zeros((PK_TOPK, tm), F32)
    for kk in range(PK_TOPK):
        cnt = cnt + (aid == rank_a[kk:kk + 1, :]).astype(F32)
    ex1 = jnp.exp(sv1 - sv1[0:1, :]) / z
    ex2 = jnp.exp(sv2 - sv2[0:1, :])
    kid = _iota((PK_NKEYS, tm), 0).astype(F32)
    e1 = jnp.zeros((PK_NKEYS, tm), F32)
    f = jnp.zeros((PK_NKEYS, tm), F32)
    e2 = jnp.zeros((PK_NKEYS, tm), F32)
    r = jnp.full((PK_NKEYS, tm), float(PK_TOPK), F32)
    for kk in range(PK_TOPK):
        hit1 = kid == si1[kk:kk + 1, :]
        e1 = jnp.where(hit1, ex1[kk:kk + 1, :], e1)
        f = jnp.where(hit1, cnt[kk:kk + 1, :], f)
        hit2 = kid == si2[kk:kk + 1, :]
        e2 = jnp.where(hit2, ex2[kk:kk + 1, :], e2)
        r = jnp.where(hit2, float(kk), r)
    return e1, f, e2, r


def _peer_route_kernel(q_ref, keys_ref, e1_ref, f_ref, e2_ref, r_ref):
    q = q_ref[...]
    s1 = _mm1(keys_ref[0], q[:, :PK_NKEYS], _NT)
    s2 = _mm1(keys_ref[1], q[:, PK_NKEYS:], _NT)

    def write(e1, f, e2, r):
        e1_ref[...] = 0.5 * e1
        f_ref[...] = f
        e2_ref[...] = e2.astype(BF16)
        r_ref[...] = r.astype(BF16)

    e1, f, e2, r, ok = _route_tables_unique(s1, s2)
    all_ok = jnp.min(ok.astype(F32)) > 0.5
    write(e1, f, e2, r)

    @pl.when(jnp.logical_not(all_ok))
    def _():
        write(*_route_tables_exact(s1, s2))


def _peer_route(q, keys, tm=512):
    t = q.shape[0]
    assert t % tm == 0, (t, tm)
    tab = jax.ShapeDtypeStruct((PK_HEADS, PK_NKEYS, t), F32)
    tab16 = jax.ShapeDtypeStruct((PK_HEADS, PK_NKEYS, t), BF16)
    tab_spec = pl.BlockSpec((None, PK_NKEYS, tm), lambda i, h: (h, 0, i))
    return pl.pallas_call(
        _peer_route_kernel,
        grid=(t // tm, PK_HEADS),
        in_specs=[pl.BlockSpec((tm, 2 * PK_NKEYS), lambda i, h: (i, h)),
                  pl.BlockSpec((None, 2, PK_NKEYS, PK_NKEYS), lambda i, h: (h, 0, 0, 0))],
        out_specs=[tab_spec] * 4,
        out_shape=[tab, tab, tab16, tab16],
        compiler_params=pltpu.CompilerParams(dimension_semantics=("parallel", "parallel")),
        name="peer_route",
    )(q, keys)


def _peer_dense_kernel(x_ref, xnt_ref, u_ref, vt_ref, e1_ref, f_ref, e2_ref, r_ref, o_ref, acc_ref, act_ref, c_ref,
                       *, ib, tm):
    ii = pl.program_id(1)

    @pl.when(ii == 0)
    def _():
        acc_ref[...] = jnp.zeros_like(acc_ref)

    pk = 16
    ng = PK_NKEYS // pk
    group = 2
    n_parts = ib // group
    n_cols = tm // 128
    half = tm // 2
    tile = lambda ref, hh, s, lanes: jnp.broadcast_to(ref[hh, s:s + 1, lanes], (pk, 128)).astype(BF16)

    def activations(part, n):
        rows = slice(part * group * PK_NKEYS, (part + 1) * group * PK_NKEYS)
        cols = slice(n * half, (n + 1) * half)
        act = jnp.dot(u_ref[rows, :], xnt_ref[:, cols], preferred_element_type=F32)
        act_ref[part * group * ng:(part + 1) * group * ng, :, cols] = act.reshape(group * ng, pk, half)

    def gates_of(part, c):
        lanes = slice(c * 128, (c + 1) * 128)
        s0 = part * group
        gates = [None] * group
        for hh in range(PK_HEADS):
            rank = r_ref[hh, :, :, lanes]
            e2 = e2_ref[hh, :, :, lanes]
            for ds in range(group):
                e1 = tile(e1_ref, hh, s0 + ds, lanes)
                f = tile(f_ref, hh, s0 + ds, lanes)
                term = jnp.maximum(jnp.minimum(f - rank, e1), 0.0) * e2
                gates[ds] = term if gates[ds] is None else gates[ds] + term
        for ds in range(group):
            rows = slice((s0 + ds) * ng, (s0 + ds + 1) * ng)
            a = act_ref[rows, :, lanes]
            a = a * (1.0 + lax.erf(a * math.sqrt(0.5)))
            c_ref[rows, :, lanes] = (gates[ds].astype(F32) * a).astype(BF16)

    activations(0, 0)
    activations(0, 1)
    for part in range(n_parts):
        for c in range(n_cols):
            if part + 1 < n_parts and c < 2:
                activations(part + 1, c)
            gates_of(part, c)
    acc_ref[...] += jnp.dot(vt_ref[...], c_ref[...].reshape(ib * PK_NKEYS, tm), preferred_element_type=F32)

    @pl.when(ii == pl.num_programs(1) - 1)
    def _():
        o_ref[...] = x_ref[...] + jnp.transpose(acc_ref[...])


def _peer_dense(x2d, xnt, u_bf, vt_bf, e1, f, e2, r, tm=512, ib=16):
    t = x2d.shape[0]
    assert t % tm == 0 and PK_NKEYS % ib == 0, (t, tm, ib)
    ni = PK_NKEYS // ib
    tab_i = pl.BlockSpec((PK_HEADS, ib, tm), lambda tt, ii: (0, ii, tt))
    pk = 16
    ng = PK_NKEYS // pk
    tab_j = pl.BlockSpec((PK_HEADS, ng, pk, tm), lambda tt, ii: (0, 0, 0, tt))
    e2 = e2.reshape(PK_HEADS, ng, pk, t)
    r = r.reshape(PK_HEADS, ng, pk, t)
    return pl.pallas_call(
        functools.partial(_peer_dense_kernel, ib=ib, tm=tm),
        grid=(t // tm, ni),
        in_specs=[pl.BlockSpec((tm, D_MODEL), lambda tt, ii: (tt, 0)),
                  pl.BlockSpec((D_MODEL, tm), lambda tt, ii: (0, tt)),
                  pl.BlockSpec((ib * PK_NKEYS, D_MODEL), lambda tt, ii: (ii, 0)),
                  pl.BlockSpec((D_MODEL, ib * PK_NKEYS), lambda tt, ii: (0, ii)),
                  tab_i, tab_i, tab_j, tab_j],
        out_specs=pl.BlockSpec((tm, D_MODEL), lambda tt, ii: (tt, 0)),
        out_shape=jax.ShapeDtypeStruct((t, D_MODEL), F32),
        scratch_shapes=[pltpu.VMEM((D_MODEL, tm), F32), pltpu.VMEM((ib * ng, pk, tm), F32),
                        pltpu.VMEM((ib * ng, pk, tm), BF16)],
        compiler_params=pltpu.CompilerParams(dimension_semantics=("parallel", "arbitrary"),
                                             vmem_limit_bytes=V7X_VMEM_LIMIT),
        name="peer_dense",
    )(x2d, xnt, u_bf, vt_bf, e1, f, e2, r)


def _t5_bucket(rel):
    n = jnp.maximum(-rel, 0)
    max_exact = REL_BUCKETS // 2
    nf = jnp.maximum(n, 1).astype(F32)
    large = max_exact + (jnp.log(nf / max_exact) / math.log(REL_MAX_DIST / max_exact)
                         * (REL_BUCKETS - max_exact)).astype(jnp.int32)
    large = jnp.minimum(large, REL_BUCKETS - 1)
    return jnp.where(n < max_exact, n, large)


def _bias_table(rel_bias, rel):
    bucket = _t5_bucket(rel)[None]
    col = lambda b: rel_bias[b].astype(F32).reshape((-1,) + (1,) * rel.ndim)
    out = jnp.broadcast_to(col(0), (rel_bias.shape[1],) + rel.shape)
    for b in range(1, REL_BUCKETS):
        out = jnp.where(bucket == b, col(b), out)
    return out


def _prompt_bias_tiles(rel_bias, tq):
    rel = jnp.arange(tq)[None, :] - jnp.arange(tq)[:, None]
    tiles = []
    for d in range(3):
        b = _bias_table(rel_bias, rel - d * tq)
        if d == 0:
            b = jnp.where(rel <= 0, b, -jnp.inf)
        tiles.append(jnp.concatenate([b, b], axis=1))
    return jnp.stack(tiles, axis=1)


def _prep_layer(l, W):
    row = lambda a: a.reshape(1, -1).astype(F32)
    w_in = W["w_in"][l]
    o_b = A_COLS
    o_c = A_COLS + 4 * MIX_W + 2 * N_HEADS
    w_cat = jnp.concatenate([
        w_in[:, :A_COLS],
        w_in[:, o_b:o_b + B_QKV],
        w_in[:, o_b + B_QKV + 2 * N_HEADS:o_c],
        w_in[:, o_b + B_QKV:o_b + B_QKV + 2 * N_HEADS],
        jnp.zeros((D_MODEL, 128 - 2 * N_HEADS), F32),
        w_in[:, o_c:],
    ], axis=1).astype(BF16)
    pad_lane = lambda a, off: jnp.zeros((1, 128), F32).at[0, off:off + a.shape[0]].set(a)
    alog = pad_lane(W["dn_a_log"][l], 0)
    dtb = pad_lane(W["dn_dt_bias"][l], 0)
    f32 = F32
    lam_init = 0.8 - 0.6 * math.exp(-0.3 * l)
    lam = (jnp.exp(jnp.sum(W["df_lq1"][l].astype(f32) * W["df_lk1"][l].astype(f32)))
           - jnp.exp(jnp.sum(W["df_lq2"][l].astype(f32) * W["df_lk2"][l].astype(f32))) + lam_init)
    return dict(
        rms_mix_g=row(W["rms_mix_g"][l]), w_cat=w_cat,
        qg=row(jnp.tile(W["df_qn_g"][l].reshape(-1), N_HEADS)),
        kg=row(jnp.tile(W["df_kn_g"][l].reshape(-1), N_HEADS)),
        rw_mu=row(W["rw_mu"][l]), rw_w0=row(W["rw_w0"][l]), rw_w2=W["rw_w2"][l], rw_a0=row(W["rw_a0"][l]),
        rw_a2=W["rw_a2"][l], rw_g2=W["rw_g2"][l], rw_kk=row(W["rw_kk"][l]), rw_ka=row(W["rw_ka"][l]),
        rw_rk=row(W["rw_rk"][l]), rw_ln_g=row(W["rw_ln_g"][l]), rw_ln_b=row(W["rw_ln_b"][l]),
        dn_conv_w=W["dn_conv_w"][l], dn_alog=alog, dn_dtb=dtb,
        dn_norm_g=row(jnp.tile(W["dn_norm_g"][l], N_HEADS)),
        lam=lam.reshape(1).astype(F32), lam_init=lam_init, subln_g=row(W["df_subln_g"][l]),
        w_out=W["w_out"][l].astype(BF16), rms_ffn_g=row(W["rms_ffn_g"][l]),
        pk_wq=W["pk_wq"][l].astype(BF16),
        pk_keys=W["pk_keys"][l], pk_u=W["pk_u"][l].astype(BF16), pk_vt=jnp.transpose(W["pk_v"][l]).astype(BF16),
    )


def _layer(x, shift_prev, s_rwkv, conv_prev, s_dn, attn_fn, P, c_len, bb):
    bn, seq, _ = x.shape
    t = bn * seq
    x2d = x.reshape(t, D_MODEL)
    h_a, h_b, q, k, v, k_by_head, v_by_head = _proj_in(x2d, P["rms_mix_g"], P["w_cat"], P["qg"], P["kg"])
    h_a3 = h_a.reshape(bn, seq, A_COLS)
    h_b3 = h_b.reshape(bn, seq, B_COLS_PAD)
    o_a, s_rwkv_new = _rwkv(h_a3, shift_prev, s_rwkv, P, c_len, bb)
    conv8 = jnp.concatenate([jnp.zeros((bn, 5, B_QKV), F32), conv_prev], axis=1)
    o_b, s_dn_new = _gdn(h_b3, conv8, s_dn, P["dn_conv_w"], P["dn_alog"], P["dn_dtb"], P["dn_norm_g"], c_len,
                         bb)
    q3, k3, v3 = (a.reshape(bn, seq, C_W) for a in (q, k, v))
    o_c = attn_fn(q3, k3, v3)
    x_new, xnt, pq = _proj_out(x2d, o_a.reshape(t, MIX_W), o_b.reshape(t, MIX_W), o_c.reshape(t, C_W),
                               P["w_out"], P["rms_ffn_g"], P["pk_wq"])
    e1, f, e2, r = _peer_route(pq, P["pk_keys"])
    y = _peer_dense(x_new, xnt, P["pk_u"], P["pk_vt"], e1, f, e2, r)
    k_rows = k_by_head.reshape(bn, seq, N_HEADS, 2 * HD)
    v_rows = v_by_head.reshape(bn, seq, N_HEADS, 2 * HD)
    shift_new = h_a3[:, -1]
    conv_new = h_b3[:, -3:, :B_QKV]
    return y.reshape(bn, seq, D_MODEL), k_rows, v_rows, s_rwkv_new, shift_new, s_dn_new, conv_new


def kernel(x_prompt, x_sample, cache_k, cache_v, state_rwkv, state_rwkv_shift, state_dn, state_dn_conv, page_table, rms_mix_g, w_in, w_out, rw_mu, rw_w0, rw_w2, rw_a0, rw_a2, rw_g2, rw_kk, rw_ka, rw_rk, rw_ln_g, rw_ln_b, dn_conv_w, dn_a_log, dn_dt_bias, dn_norm_g, df_qn_g, df_kn_g, df_lq1, df_lk1, df_lq2, df_lk2, df_subln_g, rel_bias, rms_ffn_g, pk_wq, pk_keys, pk_u, pk_v):
    W = dict(rms_mix_g=rms_mix_g, w_in=w_in, w_out=w_out, rw_mu=rw_mu, rw_w0=rw_w0, rw_w2=rw_w2, rw_a0=rw_a0,
             rw_a2=rw_a2, rw_g2=rw_g2, rw_kk=rw_kk, rw_ka=rw_ka, rw_rk=rw_rk, rw_ln_g=rw_ln_g, rw_ln_b=rw_ln_b,
             dn_conv_w=dn_conv_w, dn_a_log=dn_a_log, dn_dt_bias=dn_dt_bias, dn_norm_g=dn_norm_g,
             df_qn_g=df_qn_g, df_kn_g=df_kn_g, df_lq1=df_lq1, df_lk1=df_lk1, df_lq2=df_lq2, df_lk2=df_lk2,
             df_subln_g=df_subln_g, rms_ffn_g=rms_ffn_g, pk_wq=pk_wq, pk_keys=pk_keys, pk_u=pk_u, pk_v=pk_v)
    depth = w_in.shape[0]
    bp, seq_p, _ = x_prompt.shape
    bs, seq_s, _ = x_sample.shape
    n_pages = page_table.shape[1]
    page = cache_k.shape[2]
    past = n_pages * page
    tq = 256
    ck = cache_k.reshape(cache_k.shape[0], depth, page * N_HEADS, 2 * HD)
    cv = cache_v.reshape(cache_v.shape[0], depth, page * N_HEADS, 2 * HD)

    bias_near = _prompt_bias_tiles(rel_bias, tq)
    q_pos = past + jnp.arange(seq_s)
    same_head = jnp.eye(N_HEADS, dtype=bool)[:, None, None, :]

    def expand(b):
        bx = jnp.where(same_head, b[..., None], -jnp.inf)
        return jnp.broadcast_to(bx[:, None], (N_HEADS, 2) + bx.shape[1:]).reshape(2 * N_HEADS * seq_s, -1)

    bias_past = expand(_bias_table(rel_bias, jnp.arange(past)[None, :] - q_pos[:, None]))
    rel_new = q_pos[None, :] - q_pos[:, None]
    bias_new = expand(jnp.where(rel_new <= 0, _bias_table(rel_bias, rel_new), -jnp.inf))

    xp, xs = x_prompt, x_sample
    outs = [[] for _ in range(12)]
    zeros = lambda *s: jnp.zeros(s, F32)
    for l in range(depth):
        P = _prep_layer(l, W)
        out_scale = 1.0 - P["lam_init"]
        attn_p = lambda q, k, v: _attn_prompt(q, k, v, bias_near, P["lam"], P["subln_g"], out_scale, tq)
        by_head = lambda a: a.reshape(bs, seq_s * N_HEADS, 2 * HD)
        attn_s = lambda q, k, v: _attn_sample(q, by_head(k), by_head(v), ck, cv, page_table, l, bias_past, bias_new,
                                              P["lam"], P["subln_g"], out_scale)
        xp, kp, vp, sap, shp, sbp, cvp = _layer(
            xp, zeros(bp, A_COLS), zeros(bp, N_HEADS, HD, HD), zeros(bp, 3, B_QKV), zeros(bp, N_HEADS, HD, HD),
            attn_p, P, 64, 4)
        xs, ksn, vsn, sas, shs, sbs, cvs = _layer(
            xs, state_rwkv_shift[l], state_rwkv[l], state_dn_conv[l], state_dn[l], attn_s, P, seq_s, 8)
        for lst, val in zip(outs, (kp, vp, ksn, vsn, sap, sas, shp, shs, sbp, sbs, cvp, cvs)):
            lst.append(val)
    stack = lambda i, ax: jnp.stack(outs[i], axis=ax)
    return (xp, xs, stack(0, 1), stack(1, 1), stack(2, 1), stack(3, 1),
            stack(4, 0), stack(5, 0), stack(6, 0), stack(7, 0), stack(8, 0), stack(9, 0), stack(10, 0), stack(11, 0))
```
